```python
import math
import jax
import jax.numpy as jnp
from jax import lax
import numpy as np

D_MODEL = 2048
BATCH = 4
SEQ = 2048
DEPTH = 4

GRID_W = 64
CTX_LEN = 256
HEAD = 64
RW_W = D_MODEL // 4
RW_H = RW_W // HEAD
LORA_W = 64
LORA_A = 64
LORA_G = 128
W_DECAY_SCALE = 0.606531
RW_LN_EPS = 64e-5
NA_W = D_MODEL // 4
NA_H = NA_W // HEAD
NA_KH = 8
NA_KW = 16
MB_INNER = D_MODEL // 2
MB_H = MB_INNER // HEAD
MB_G = 4
MB_R = MB_H // MB_G
MB_N = 128
MB_CONV = 5
MB_CHUNK = 128
ROPE_BASE = 10000.0
MIX = RW_W + NA_W + MB_INNER
PEER_HEADS = 8
D_KEY = 256
N_KEYS = 128
N_EXPERTS = N_KEYS * N_KEYS
PEER_TOPK = 16
PEER_BLOCK = 64
NORM_EPS = 1e-6
NEG_INF = -1e30
RW_COLS = 3 * RW_W + 2 * LORA_W + 2 * LORA_A + LORA_G
NA_COLS = 3 * NA_W
MB_CONV_CH = MB_INNER + 2 * MB_G * MB_N
MB_COLS = MB_INNER + MB_CONV_CH + 2 * MB_H
IN_COLS = RW_COLS + NA_COLS + MB_COLS

kernel_name = 'hybrid_rwkv7_natten_mamba2_peer_dit'


def split_cols(z, sizes):
    cuts = [int(s) for s in np.cumsum(sizes)[:-1]]
    return jnp.split(z, cuts, axis=-1)


def rms_norm(x, w):
    xf = x.astype(jnp.float32)
    y = xf * lax.rsqrt(jnp.mean(xf * xf, axis=-1, keepdims=True) + NORM_EPS)
    return (y * w.astype(jnp.float32)).astype(x.dtype)


def modulate(h, shift, scale):
    return h * (1 + scale) + shift


def centred_shift(z, mu):
    prev = jnp.pad(z[:, :-1], ((0, 0), (1, 0), (0, 0)))
    nxt = jnp.pad(z[:, 1:], ((0, 0), (0, 1), (0, 0)))
    return z + mu[0] * (prev - z) + mu[1] * (nxt - z)


def centred_dwconv(z, w, b):
    k = w.shape[0]
    p = k // 2
    t = z.shape[1]
    zp = jnp.pad(z, ((0, 0), (p, p), (0, 0)))
    out = zp[:, 0:t] * w[0]
    for i in range(1, k):
        out = out + zp[:, i:i + t] * w[i]
    return out + b


def axial_rope(z, row, col):
    n = z.shape[-1]
    half = n // 2
    nf = half // 2
    inv = ROPE_BASE ** (-jnp.arange(nf, dtype=jnp.float32) / nf)
    zf = z.astype(jnp.float32)

    def rot(u, p):
        ang = p.astype(jnp.float32)[:, None] * inv
        cos = jnp.cos(ang)[None, :, None, :]
        sin = jnp.sin(ang)[None, :, None, :]
        u1, u2 = u[..., :nf], u[..., nf:]
        return jnp.concatenate([u1 * cos - u2 * sin, u1 * sin + u2 * cos], axis=-1)

    return jnp.concatenate([rot(zf[..., :half], row), rot(zf[..., half:], col)], axis=-1).astype(z.dtype)


def rwkv7_streams(z, mu, w0, w2, a0, a2, g2, k_k, k_a):
    b, t, _ = z.shape
    z = centred_shift(z, mu)
    r, k, v, lw_f, lw_b, la_f, la_b, lg = split_cols(
        z, (RW_W, RW_W, RW_W, LORA_W, LORA_W, LORA_A, LORA_A, LORA_G))
    hd = lambda u: u.reshape(b, t, RW_H, HEAD)
    kk = hd(k * k_k).astype(jnp.float32)
    kk = (kk / jnp.maximum(jnp.linalg.norm(kk, axis=-1, keepdims=True), 1e-12)).astype(z.dtype)
    g = jax.nn.sigmoid(lg) @ g2
    dirs = []
    for d, (lw, la) in enumerate(((lw_f, la_f), (lw_b, la_b))):
        decay = jnp.exp(-W_DECAY_SCALE * jax.nn.sigmoid(w0[d] + jnp.tanh(lw) @ w2[d]))
        a = jax.nn.sigmoid(a0[d] + la @ a2[d])
        k_d = k * (1 + (a - 1) * k_a)
        dirs.append((hd(decay), hd(k_d), -kk, kk * hd(a)))
    return hd(r), hd(k), hd(v), g, dirs


def rwkv7_scan(r, v, dir_streams, s0, reverse):
    decay, k, a_vec, b_vec = dir_streams

    def step(state, inp):
        r_t, w_t, k_t, v_t, a_t, b_t = inp
        sa = jnp.einsum('bhij,bhj->bhi', state, a_t)
        state = (state * w_t[:, :, None, :] + sa[..., None] * b_t[:, :, None, :]
                 + v_t[..., None] * k_t[:, :, None, :])
        return state, jnp.einsum('bhij,bhj->bhi', state, r_t)

    xs = tuple(jnp.swapaxes(u, 0, 1) for u in (r, decay, k, v, a_vec, b_vec))
    s_fin, ys = lax.scan(step, s0, xs, reverse=reverse)
    return jnp.swapaxes(ys, 0, 1), s_fin


def rwkv7_out(y, r, k, v, g, r_k, ln_w, ln_b):
    b, t = y.shape[:2]
    yf = y.astype(jnp.float32)
    mean = jnp.mean(yf, axis=-1, keepdims=True)
    var = jnp.mean(jnp.square(yf - mean), axis=-1, keepdims=True)
    yn = ((yf - mean) * lax.rsqrt(var + RW_LN_EPS)).astype(y.dtype).reshape(b, t, RW_W) * ln_w + ln_b
    bonus = (jnp.sum(r * k * r_k, axis=-1, keepdims=True) * v).reshape(b, t, RW_W)
    return (yn + bonus) * g


def rwkv7_mixer(zc, zl, mu, w0, w2, a0, a2, g2, k_k, k_a, r_k, ln_w, ln_b):
    rc, kc, vc, gc, dc = rwkv7_streams(zc, mu, w0, w2, a0, a2, g2, k_k, k_a)
    rl, kl, vl, gl, dl = rwkv7_streams(zl, mu, w0, w2, a0, a2, g2, k_k, k_a)
    s0 = jnp.zeros((zl.shape[0], RW_H, HEAD, HEAD), zl.dtype)
    ys_c, ys_l = [], []
    for d, rev in enumerate((False, True)):
        y_c, s_ctx = rwkv7_scan(rc, vc, dc[d], s0, rev)
        y_l, _ = rwkv7_scan(rl, vl, dl[d], s_ctx, rev)
        ys_c.append(y_c)
        ys_l.append(y_l)
    return (rwkv7_out(ys_c[0] + ys_c[1], rc, kc, vc, gc, r_k, ln_w, ln_b),
            rwkv7_out(ys_l[0] + ys_l[1], rl, kl, vl, gl, r_k, ln_w, ln_b))


def natten_mixer(zc, zl, rpb):
    b, t, _ = zl.shape
    tc = zc.shape[1]
    qc, kc, vc = [u.reshape(b, tc, NA_H, HEAD) for u in split_cols(zc, (NA_W, NA_W, NA_W))]
    ql, kl, vl = [u.reshape(b, t, NA_H, HEAD) for u in split_cols(zl, (NA_W, NA_W, NA_W))]
    scale = HEAD ** -0.5
    s = jnp.einsum('bqhd,bkhd->bhqk', qc, kc).astype(jnp.float32) * scale
    y_c = jnp.einsum('bhqk,bkhd->bqhd', jax.nn.softmax(s, axis=-1).astype(vc.dtype), vc).reshape(b, tc, NA_W)
    rows = t // GRID_W
    kh = min(NA_KH, rows)
    r = jnp.arange(rows)
    key_rows = jnp.clip(r - kh // 2, 0, rows - kh)[:, None] + jnp.arange(kh)[None, :]
    cidx = jnp.arange(GRID_W)
    c0 = jnp.clip(cidx - NA_KW // 2, 0, GRID_W - NA_KW)
    in_win = (cidx[None, :] >= c0[:, None]) & (cidx[None, :] < c0[:, None] + NA_KW)
    qg = ql.reshape(b, rows, GRID_W, NA_H, HEAD)
    kg = kl.reshape(b, rows, GRID_W, NA_H, HEAD)[:, key_rows]
    vg = vl.reshape(b, rows, GRID_W, NA_H, HEAD)[:, key_rows]
    dr = key_rows - r[:, None]
    dc = cidx[None, :] - cidx[:, None]
    bias = rpb[:, (dr + NA_KH - 1)[:, None, :, None],
               jnp.clip(dc + NA_KW - 1, 0, 2 * NA_KW - 2)[None, :, None, :]]
    s_win = jnp.einsum('brqhd,brkjhd->bhrqkj', qg, kg).astype(jnp.float32) * scale + bias.astype(jnp.float32)[None]
    s_win = jnp.where(in_win[:, None, :], s_win, NEG_INF)
    s_ctx = jnp.einsum('brqhd,bchd->bhrqc', qg, kc).astype(jnp.float32) * scale
    nwin = kh * GRID_W
    p = jax.nn.softmax(jnp.concatenate([s_win.reshape(b, NA_H, rows, GRID_W, nwin), s_ctx], axis=-1),
                       axis=-1).astype(vl.dtype)
    p_win = p[..., :nwin].reshape(b, NA_H, rows, GRID_W, kh, GRID_W)
    y_l = (jnp.einsum('bhrqkj,brkjhd->brqhd', p_win, vg)
           + jnp.einsum('bhrqc,bchd->brqhd', p[..., nwin:], vc))
    return y_c, y_l.reshape(b, t, NA_W)


def segsum(a):
    n = a.shape[-1]
    ax = jnp.broadcast_to(a[..., :, None], a.shape + (n,))
    ax = jnp.where(jnp.tril(jnp.ones((n, n), bool), -1), ax, 0.0)
    cs = jnp.cumsum(ax, axis=-2)
    return jnp.where(jnp.tril(jnp.ones((n, n), bool)), cs, -jnp.inf)


def ssd_chunked(xh, dt, a_neg, bm, cm, s0):
    bsz, t, g, rr, p = xh.shape
    nc = t // MB_CHUNK
    dtype = xh.dtype
    dA = dt.astype(jnp.float32) * a_neg
    xc = (xh * dt[..., None].astype(dtype)).reshape(bsz, nc, MB_CHUNK, g, rr, p)
    bc = bm.reshape(bsz, nc, MB_CHUNK, g, MB_N)
    cc = cm.reshape(bsz, nc, MB_CHUNK, g, MB_N)
    dAc = jnp.transpose(dA.reshape(bsz, nc, MB_CHUNK, g, rr), (0, 3, 4, 1, 2))
    acs = jnp.cumsum(dAc, axis=-1)
    lmat = jnp.exp(segsum(dAc)).astype(dtype)
    cb = jnp.einsum('bclgn,bcsgn->bgcls', cc, bc)
    y_diag = jnp.einsum('bgrcls,bcsgrp->bclgrp', cb[:, :, None] * lmat, xc)
    decay_states = jnp.exp(acs[..., -1:] - acs).astype(dtype)
    states = jnp.einsum('bcsgn,bgrcs,bcsgrp->bcgrpn', bc, decay_states, xc)
    states = jnp.concatenate([s0[:, None], states], axis=1)
    chunk_decay = jnp.exp(segsum(jnp.pad(acs[..., -1], ((0, 0), (0, 0), (0, 0), (1, 0))))).astype(dtype)
    states = jnp.einsum('bgrzc,bcgrpn->bzgrpn', chunk_decay, states)
    y_off = jnp.einsum('bclgn,bcgrpn,bgrcl->bclgrp', cc, states[:, :-1], jnp.exp(acs).astype(dtype))
    return (y_diag + y_off).reshape(bsz, t, g, rr, p), states[:, -1]


def mamba2_mixer(zc, zl, conv_w, conv_b, dt_bias, a_log, d_skip, norm_w, row, col):
    def streams(z, rope):
        b, t, _ = z.shape
        gate, xbc, dt_f, dt_b = split_cols(z, (MB_INNER, MB_CONV_CH, MB_H, MB_H))
        xbc = jax.nn.silu(centred_dwconv(xbc, conv_w, conv_b))
        xs, bm, cm = split_cols(xbc, (MB_INNER, MB_G * MB_N, MB_G * MB_N))
        bm = bm.reshape(b, t, MB_G, MB_N)
        cm = cm.reshape(b, t, MB_G, MB_N)
        if rope:
            bm = axial_rope(bm, row, col)
            cm = axial_rope(cm, row, col)
        dts = [jax.nn.softplus(dt + dt_bias[d]).reshape(b, t, MB_G, MB_R) for d, dt in enumerate((dt_f, dt_b))]
        return gate, xs.reshape(b, t, MB_G, MB_R, HEAD), bm, cm, dts

    gate_c, x_c, b_c, c_c, dt_c = streams(zc, False)
    gate_l, x_l, b_l, c_l, dt_l = streams(zl, True)
    a_neg = -jnp.exp(a_log.astype(jnp.float32)).reshape(2, MB_G, MB_R)
    s0 = jnp.zeros((zl.shape[0], MB_G, MB_R, HEAD, MB_N), zl.dtype)
    ys_c, ys_l = [], []
    for d in range(2):
        f = (lambda u: jnp.flip(u, axis=1)) if d == 1 else (lambda u: u)
        yc_d, s_ctx = ssd_chunked(f(x_c), f(dt_c[d]), a_neg[d], f(b_c), f(c_c), s0)
        yl_d, _ = ssd_chunked(f(x_l), f(dt_l[d]), a_neg[d], f(b_l), f(c_l), s_ctx)
        ys_c.append(f(yc_d))
        ys_l.append(f(yl_d))

    def out(y, xh, gate):
        b, t = y.shape[:2]
        y = (y + xh * d_skip.reshape(MB_G, MB_R, 1)).reshape(b, t, MB_INNER) * jax.nn.silu(gate)
        yf = y.astype(jnp.float32).reshape(b, t, MB_G, MB_INNER // MB_G)
        yf = yf * lax.rsqrt(jnp.mean(yf * yf, axis=-1, keepdims=True) + NORM_EPS)
        return (yf.reshape(b, t, MB_INNER) * norm_w.astype(jnp.float32)).astype(y.dtype)

    return out(ys_c[0] + ys_c[1], x_c, gate_c), out(ys_l[0] + ys_l[1], x_l, gate_l)


def peer_ffn(h, wq, sub_keys, u_tab, v_tab):
    n, d = h.shape
    q = (h @ wq).reshape(n, PEER_HEADS, 2, D_KEY // 2)
    s = jnp.einsum('nhpd,hpkd->nhpk', q, sub_keys).astype(jnp.float32)
    s_top, i_top = lax.top_k(s, PEER_TOPK)
    cand = s_top[:, :, 0, :, None] + s_top[:, :, 1, None, :]
    best, pos = lax.top_k(cand.reshape(n, PEER_HEADS, PEER_TOPK * PEER_TOPK), PEER_TOPK)
    i1 = jnp.take_along_axis(i_top[:, :, 0], pos // PEER_TOPK, axis=-1)
    i2 = jnp.take_along_axis(i_top[:, :, 1], pos % PEER_TOPK, axis=-1)
    ids = i1 * N_KEYS + i2
    gates = jax.nn.softmax(best, axis=-1).astype(h.dtype)

    def apply(blk):
        hb, idb, gb = blk
        act = jax.nn.gelu(jnp.einsum('pd,phkd->phk', hb, u_tab[idb]), approximate=False)
        return jnp.einsum('phk,phkd->pd', gb * act, v_tab[idb])

    nb = n // PEER_BLOCK
    out = lax.map(apply, (h.reshape(nb, PEER_BLOCK, d),
                          ids.reshape(nb, PEER_BLOCK, PEER_HEADS, PEER_TOPK),
                          gates.reshape(nb, PEER_BLOCK, PEER_HEADS, PEER_TOPK)))
    return out.reshape(n, d)


def setup_inputs(seed: int = 0) -> dict:
    key = jax.random.key(seed)
    ks = iter(jax.random.split(key, 48))
    L, D = DEPTH, D_MODEL

    def nrm(shape, s):
        return jax.random.normal(next(ks), shape, jnp.float32) * s

    dt0 = jnp.exp(jax.random.uniform(next(ks), (L, 2, MB_H), jnp.float32,
                                     minval=math.log(1e-3), maxval=math.log(1e-1)))
    dt_bias = dt0 + jnp.log(-jnp.expm1(-dt0))
    a_log = jnp.log(jax.random.uniform(next(ks), (L, 2, MB_H), jnp.float32, minval=1.0, maxval=16.0))
    rw_mu = jax.random.uniform(next(ks), (L, 2, RW_COLS), jnp.float32, minval=0.0, maxval=0.4)
    return {
        'x': nrm((BATCH, SEQ, D), 1.0),
        'c': nrm((BATCH, D), 1.0),
        'ctx': nrm((BATCH, CTX_LEN, D), 1.0),
        'c_ctx': nrm((D,), 1.0),
        'ada_w': nrm((L, D, 6 * D), 0.5 * D ** -0.5),
        'ada_b': nrm((L, 6 * D), 0.02),
        'norm1_w': 1.0 + nrm((L, D), 0.02),
        'norm2_w': 1.0 + nrm((L, D), 0.02),
        'w_in': nrm((L, D, IN_COLS), D ** -0.5),
        'w_out': nrm((L, MIX, D), MIX ** -0.5),
        'rw_mu': rw_mu,
        'rw_w0': nrm((L, 2, RW_W), 1.0),
        'rw_w2': nrm((L, 2, LORA_W, RW_W), 0.5 * LORA_W ** -0.5),
        'rw_a0': nrm((L, 2, RW_W), 0.5),
        'rw_a2': nrm((L, 2, LORA_A, RW_W), LORA_A ** -0.5),
        'rw_g2': nrm((L, LORA_G, RW_W), LORA_G ** -0.5),
        'rw_kk': 0.85 + nrm((L, RW_W), 0.05),
        'rw_ka': 1.0 + nrm((L, RW_W), 0.05),
        'rw_rk': nrm((L, RW_H, HEAD), 0.1),
        'rw_ln_w': 1.0 + nrm((L, RW_W), 0.02),
        'rw_ln_b': nrm((L, RW_W), 0.01),
        'na_rpb': nrm((L, NA_H, 2 * NA_KH - 1, 2 * NA_KW - 1), 0.1),
        'mb_conv_w': nrm((L, MB_CONV, MB_CONV_CH), MB_CONV ** -0.5),
        'mb_conv_b': nrm((L, MB_CONV_CH), 0.02),
        'mb_dt_bias': dt_bias,
        'mb_a_log': a_log,
        'mb_d': 1.0 + nrm((L, MB_H), 0.1),
        'mb_norm_w': 1.0 + nrm((L, MB_INNER), 0.02),
        'pe_wq': nrm((L, D, PEER_HEADS * D_KEY), D ** -0.5),
        'pe_keys': nrm((L, PEER_HEADS, 2, N_KEYS, D_KEY // 2), (D_KEY // 2) ** -0.5),
        'pe_u': nrm((L, N_EXPERTS, D), D ** -0.5),
        'pe_v': nrm((L, N_EXPERTS, D), 0.1),
        'final_norm_w': 1.0 + nrm((D,), 0.02),
    }


def reference(x, c, ctx, c_ctx, ada_w, ada_b, norm1_w, norm2_w, w_in, w_out,
              rw_mu, rw_w0, rw_w2, rw_a0, rw_a2, rw_g2, rw_kk, rw_ka, rw_rk, rw_ln_w, rw_ln_b,
              na_rpb, mb_conv_w, mb_conv_b, mb_dt_bias, mb_a_log, mb_d, mb_norm_w,
              pe_wq, pe_keys, pe_u, pe_v, final_norm_w):
    b, t, d = x.shape
    tc = ctx.shape[1]
    pos = jnp.arange(t)
    row, col = pos // GRID_W, pos % GRID_W
    silu_c = jax.nn.silu(c)[:, None, :]
    silu_cc = jax.nn.silu(c_ctx)
    xl, xc = x, ctx
    for l in range(DEPTH):
        update_ctx = l < DEPTH - 1
        m_l = jnp.split(silu_c @ ada_w[l] + ada_b[l], 6, axis=-1)
        m_c = jnp.split(silu_cc @ ada_w[l] + ada_b[l], 6, axis=-1)
        zl = modulate(rms_norm(xl, norm1_w[l]), m_l[0], m_l[1]) @ w_in[l]
        zc = modulate(rms_norm(xc, norm1_w[l]), m_c[0], m_c[1]) @ w_in[l]
        rw_c, na_c, mb_c = split_cols(zc, (RW_COLS, NA_COLS, MB_COLS))
        rw_l, na_l, mb_l = split_cols(zl, (RW_COLS, NA_COLS, MB_COLS))
        y_rw = rwkv7_mixer(rw_c, rw_l, rw_mu[l], rw_w0[l], rw_w2[l], rw_a0[l], rw_a2[l], rw_g2[l],
                           rw_kk[l], rw_ka[l], rw_rk[l], rw_ln_w[l], rw_ln_b[l])
        y_na = natten_mixer(na_c, na_l, na_rpb[l])
        y_mb = mamba2_mixer(mb_c, mb_l, mb_conv_w[l], mb_conv_b[l], mb_dt_bias[l], mb_a_log[l],
                            mb_d[l], mb_norm_w[l], row, col)
        xl = xl + m_l[2] * (jnp.concatenate([y_rw[1], y_na[1], y_mb[1]], axis=-1) @ w_out[l])
        hl = modulate(rms_norm(xl, norm2_w[l]), m_l[3], m_l[4]).reshape(b * t, d)
        if update_ctx:
            xc = xc + m_c[2] * (jnp.concatenate([y_rw[0], y_na[0], y_mb[0]], axis=-1) @ w_out[l])
            hc = modulate(rms_norm(xc, norm2_w[l]), m_c[3], m_c[4]).reshape(b * tc, d)
            f = peer_ffn(jnp.concatenate([hc, hl], axis=0), pe_wq[l], pe_keys[l], pe_u[l], pe_v[l])
            xc = xc + m_c[5] * f[:b * tc].reshape(b, tc, d)
            xl = xl + m_l[5] * f[b * tc:].reshape(b, t, d)
        else:
            xl = xl + m_l[5] * peer_ffn(hl, pe_wq[l], pe_keys[l], pe_u[l], pe_v[l]).reshape(b, t, d)
    return rms_norm(xl, final_norm_w)
```

```python
import functools
import math

import numpy as np
import jax
import jax.numpy as jnp
from jax import lax
from jax.experimental import pallas as pl
from jax.experimental.pallas import tpu as pltpu

D = 2048
NB = 4
T_LAT = 2048
T_CTX = 256
T_ALL = T_CTX + T_LAT
DEPTH = 4
GRID_W = 64
GRID_ROWS = T_LAT // GRID_W
HEAD = 64
RW_W = 512
RW_H = 8
LORA = 64
LORA_G = 128
W_DECAY_SCALE = 0.606531
RW_LN_EPS = 64e-5
NA_W = 512
NA_H = 8
NA_KH = 8
NA_KW = 16
MB_INNER = 1024
MB_H = 16
MB_G = 4
MB_R = 4
MB_N = 128
MB_CONV = 5
MB_CHUNK = 128
ROPE_BASE = 10000.0
PEER_HEADS = 8
N_KEYS = 128
PEER_TOPK = 16
NORM_EPS = 1e-6
NEG_INF = -1e30
RW_COLS = 3 * RW_W + 4 * LORA + LORA_G
NA_COLS = 3 * NA_W
MB_CONV_CH = MB_INNER + 2 * MB_G * MB_N
MB_COLS = MB_INNER + MB_CONV_CH + 2 * MB_H
IN_COLS = RW_COLS + NA_COLS + MB_COLS
NA_OFF = RW_COLS
MB_OFF = RW_COLS + NA_COLS
XBC_OFF = MB_OFF + MB_INNER
DT_OFF = XBC_OFF + MB_CONV_CH

LANES = 128
SUBLANES = 8
VMEM_LIMIT = 56 * 1024 * 1024

ROW_BLK = 256
N_ROW_BLK = T_ALL // ROW_BLK
BF16 = jnp.bfloat16
F32 = jnp.float32
HI = lax.Precision.HIGHEST


def _cparams(sem):
    return pltpu.CompilerParams(dimension_semantics=sem, vmem_limit_bytes=VMEM_LIMIT)


def _dot(a, b, precision=None):
    return jnp.dot(a, b, preferred_element_type=F32, precision=precision)


def _dot_nt(a, b, precision=None):
    return lax.dot_general(a, b, (((1,), (1,)), ((), ())), preferred_element_type=F32,
                           precision=precision)


def _cast_kernel(x_ref, o_ref):
    o_ref[...] = x_ref[...].astype(BF16)


def cast_bf16(w, rows_blk):
    r, c = w.shape
    return pl.pallas_call(
        _cast_kernel,
        grid=(r // rows_blk,),
        in_specs=[pl.BlockSpec((rows_blk, c), lambda i: (i, 0))],
        out_specs=pl.BlockSpec((rows_blk, c), lambda i: (i, 0)),
        out_shape=jax.ShapeDtypeStruct((r, c), BF16),
        compiler_params=_cparams(("arbitrary",)),
        name="cast_bf16",
    )(w)


ADA_TN = 1024


def _ada_kernel(c_ref, w_ref, b_ref, o_ref):
    c = c_ref[...]
    s = c * jax.nn.sigmoid(c)
    o_ref[0] = _dot(s, w_ref[0], precision=HI) + b_ref[0]


def ada_rows(cond, ada_w, ada_b):
    nl = ada_w.shape[0]
    return pl.pallas_call(
        _ada_kernel,
        grid=(nl, 6 * D // ADA_TN),
        in_specs=[pl.BlockSpec((SUBLANES, D), lambda l, j: (0, 0)),
                  pl.BlockSpec((1, D, ADA_TN), lambda l, j: (l, 0, j)),
                  pl.BlockSpec((1, 1, ADA_TN), lambda l, j: (l, 0, j))],
        out_specs=pl.BlockSpec((1, SUBLANES, ADA_TN), lambda l, j: (l, 0, j)),
        out_shape=jax.ShapeDtypeStruct((nl, SUBLANES, 6 * D), F32),
        compiler_params=_cparams(("arbitrary", "arbitrary")),
        name="ada_rows",
    )(cond, ada_w, ada_b.reshape(nl, 1, 6 * D))


def _mod_index(b, i):
    return 2 * b + jnp.minimum(i, 1)


def _norm_mod_kernel(which, x_ref, nw_ref, tab_ref, o_ref):
    x = x_ref[0]
    ms = jnp.mean(x * x, axis=-1, keepdims=True)
    y = x * lax.rsqrt(ms + NORM_EPS) * nw_ref[...]
    shift = tab_ref[0, which:which + 1, :]
    scale = tab_ref[0, which + 1:which + 2, :]
    o_ref[0] = (y * (1.0 + scale) + shift).astype(BF16)


def norm_mod(x, nw, tab, which):
    nb = x.shape[0]
    return pl.pallas_call(
        functools.partial(_norm_mod_kernel, which),
        grid=(nb, N_ROW_BLK),
        in_specs=[pl.BlockSpec((1, ROW_BLK, D), lambda b, i: (b, i, 0)),
                  pl.BlockSpec((1, D), lambda b, i: (0, 0)),
                  pl.BlockSpec((1, 6, D), lambda b, i: (_mod_index(b, i), 0, 0))],
        out_specs=pl.BlockSpec((1, ROW_BLK, D), lambda b, i: (b, i, 0)),
        out_shape=jax.ShapeDtypeStruct(x.shape, BF16),
        compiler_params=_cparams(("arbitrary", "arbitrary")),
        name="norm_mod",
    )(x, nw, tab)


def _final_norm_kernel(x_ref, nw_ref, o_ref):
    x = x_ref[0]
    ms = jnp.mean(x * x, axis=-1, keepdims=True)
    o_ref[0] = x * lax.rsqrt(ms + NORM_EPS) * nw_ref[...]


def final_norm(x, nw):
    nb = x.shape[0]
    return pl.pallas_call(
        _final_norm_kernel,
        grid=(nb, T_LAT // ROW_BLK),
        in_specs=[pl.BlockSpec((1, ROW_BLK, D), lambda b, i: (b, i + 1, 0)),
                  pl.BlockSpec((1, D), lambda b, i: (0, 0))],
        out_specs=pl.BlockSpec((1, ROW_BLK, D), lambda b, i: (b, i, 0)),
        out_shape=jax.ShapeDtypeStruct((nb, T_LAT, D), F32),
        compiler_params=_cparams(("arbitrary", "arbitrary")),
        name="final_norm",
    )(x, nw)


MM_TM = 1024
MM_TN = 512


def _mm_kernel(a_ref, w_ref, o_ref):
    o_ref[...] = _dot(a_ref[...], w_ref[...])


def matmul(a, w):
    m, k = a.shape
    n = w.shape[1]
    tm = MM_TM if m % MM_TM == 0 else ROW_BLK
    return pl.pallas_call(
        _mm_kernel,
        grid=(m // tm, pl.cdiv(n, MM_TN)),
        in_specs=[pl.BlockSpec((tm, k), lambda i, j: (i, 0)),
                  pl.BlockSpec((k, MM_TN), lambda i, j: (0, j))],
        out_specs=pl.BlockSpec((tm, MM_TN), lambda i, j: (i, j)),
        out_shape=jax.ShapeDtypeStruct((m, n), F32),
        compiler_params=_cparams(("arbitrary", "arbitrary")),
        name="matmul",
    )(a, w)


def _out_proj_kernel(x_ref, rw_ref, na_ref, mb_ref, w_ref, tab_ref, o_ref):
    acc = _dot(rw_ref[0], w_ref[0:RW_W, :])
    acc += _dot(na_ref[0], w_ref[RW_W:RW_W + NA_W, :])
    acc += _dot(mb_ref[0], w_ref[RW_W + NA_W:, :])
    o_ref[0] = x_ref[0] + tab_ref[0, 2:3, :] * acc


def out_proj(x, y_rw, y_na, y_mb, w, tab):
    nb = x.shape[0]
    row = lambda b, i: (b, i, 0)
    return pl.pallas_call(
        _out_proj_kernel,
        grid=(nb, N_ROW_BLK),
        in_specs=[pl.BlockSpec((1, ROW_BLK, D), row),
                  pl.BlockSpec((1, ROW_BLK, RW_W), row),
                  pl.BlockSpec((1, ROW_BLK, NA_W), row),
                  pl.BlockSpec((1, ROW_BLK, MB_INNER), row),
                  pl.BlockSpec((D, D), lambda b, i: (0, 0)),
                  pl.BlockSpec((1, 6, D), lambda b, i: (_mod_index(b, i), 0, 0))],
        out_specs=pl.BlockSpec((1, ROW_BLK, D), row),
        out_shape=jax.ShapeDtypeStruct(x.shape, F32),
        input_output_aliases={0: 0},
        compiler_params=_cparams(("arbitrary", "arbitrary")),
        name="out_proj",
    )(x, y_rw, y_na, y_mb, w, tab)


TOPK_TT = 128


def _extract_top(s, count):
    n, t = s.shape
    rows = lax.broadcasted_iota(jnp.int32, (n, t), 0)
    slot = lax.broadcasted_iota(jnp.int32, (count, t), 0)

    def body(r, carry):
        s, vals, idxs = carry
        m = jnp.max(s, axis=0, keepdims=True)
        pos = jnp.min(jnp.where(s == m, rows, n), axis=0, keepdims=True)
        vals = jnp.where(slot == r, m, vals)
        idxs = jnp.where(slot == r, pos, idxs)
        s = jnp.where(rows == pos, -jnp.inf, s)
        return s, vals, idxs

    init = (s, jnp.zeros((count, t), F32), jnp.zeros((count, t), jnp.int32))
    _, vals, idxs = lax.fori_loop(0, count, body, init)
    return vals, idxs


def _pick_rows(sel, table):
    out = jnp.zeros(sel.shape, table.dtype)
    for c in range(table.shape[0]):
        out = jnp.where(sel == c, table[c:c + 1, :], out)
    return out


def _peer_topk_kernel(q_ref, keys_ref, i1_ref, i2_ref, g_ref):
    gates, e1, e2 = [], [], []
    for h in range(PEER_HEADS):
        halves = []
        for p in range(2):
            c0 = (2 * h + p) * N_KEYS
            qs = q_ref[:, c0:c0 + N_KEYS].astype(BF16)
            st = _dot_nt(keys_ref[h, p].astype(BF16), qs)
            halves.append(_extract_top(st, PEER_TOPK))
        (v1, k1), (v2, k2) = halves
        cand = jnp.concatenate([v1[r:r + 1, :] + v2 for r in range(PEER_TOPK)], axis=0)
        best, pos = _extract_top(cand, PEER_TOPK)
        e1.append(_pick_rows(pos >> 4, k1).astype(F32))
        e2.append(_pick_rows(pos & (PEER_TOPK - 1), k2).astype(F32))
        ex = jnp.exp(best - jnp.max(best, axis=0, keepdims=True))
        gates.append(ex / jnp.sum(ex, axis=0, keepdims=True))
    i1_ref[...] = jnp.concatenate(e1, axis=0).T
    i2_ref[...] = jnp.concatenate(e2, axis=0).T
    g_ref[...] = jnp.concatenate(gates, axis=0).T


def peer_topk(q, keys):
    n = q.shape[0]
    out = jax.ShapeDtypeStruct((n, PEER_HEADS * PEER_TOPK), F32)
    spec = pl.BlockSpec((TOPK_TT, PEER_HEADS * PEER_TOPK), lambda i: (i, 0))
    return pl.pallas_call(
        _peer_topk_kernel,
        grid=(n // TOPK_TT,),
        in_specs=[pl.BlockSpec((TOPK_TT, D), lambda i: (i, 0)),
                  pl.BlockSpec(keys.shape, lambda i: (0, 0, 0, 0))],
        out_specs=[spec, spec, spec],
        out_shape=[out, out, out],
        compiler_params=_cparams(("arbitrary",)),
        name="peer_topk",
    )(q, keys)


EXP_BLK = 2 * N_KEYS
N_EXP_BLK = N_KEYS * N_KEYS // EXP_BLK
G_PITCH = N_KEYS + SUBLANES
SQRT_HALF = 0.7071067811865476


def _peer_expert_kernel(h_ref, i1_ref, i2_ref, g_ref, u_ref, v_ref, x_ref, tab_ref, o_ref,
                        gs_ref, acc_ref):
    jj = pl.program_id(2)

    @pl.when(jj == 0)
    def _():
        acc_ref[...] = jnp.zeros_like(acc_ref)
        rows = lax.broadcasted_iota(jnp.int32, (N_KEYS, N_KEYS), 0).astype(F32)

        def tok(t, carry):
            i1r = i1_ref[pl.ds(t, 1), :]
            i2r = i2_ref[pl.ds(t, 1), :]
            gr = g_ref[pl.ds(t, 1), :]
            at = jnp.where(rows == i1r, gr, 0.0).astype(BF16)
            bt = jnp.where(rows == i2r, 1.0, 0.0).astype(BF16)
            gs_ref[pl.ds(pl.multiple_of(t * G_PITCH, SUBLANES), N_KEYS), :] = _dot_nt(at, bt)
            return carry

        lax.fori_loop(0, ROW_BLK, tok, 0)

    s = _dot_nt(h_ref[0], u_ref[...])
    g = jnp.concatenate(
        [gs_ref[pl.ds(2 * jj, ROW_BLK, stride=G_PITCH), :],
         gs_ref[pl.ds(2 * jj + 1, ROW_BLK, stride=G_PITCH), :]], axis=1)
    act = 0.5 * s * (1.0 + lax.erf(s * SQRT_HALF))
    acc_ref[...] += _dot((g * act).astype(BF16), v_ref[...])

    @pl.when(jj == N_EXP_BLK - 1)
    def _():
        o_ref[0] = x_ref[0] + tab_ref[0, 5:6, :] * acc_ref[...]


def peer_expert(x, h, i1, i2, g, u, v, tab):
    nb = x.shape[0]
    row = lambda b, i, j: (b, i, 0)
    sel = pl.BlockSpec((None, ROW_BLK, PEER_HEADS * PEER_TOPK), row)
    return pl.pallas_call(
        _peer_expert_kernel,
        grid=(nb, N_ROW_BLK, N_EXP_BLK),
        in_specs=[pl.BlockSpec((1, ROW_BLK, D), row), sel, sel, sel,
                  pl.BlockSpec((EXP_BLK, D), lambda b, i, j: (j, 0)),
                  pl.BlockSpec((EXP_BLK, D), lambda b, i, j: (j, 0)),
                  pl.BlockSpec((1, ROW_BLK, D), row),
                  pl.BlockSpec((1, 6, D), lambda b, i, j: (_mod_index(b, i), 0, 0))],
        out_specs=pl.BlockSpec((1, ROW_BLK, D), row),
        out_shape=jax.ShapeDtypeStruct(x.shape, F32),
        scratch_shapes=[pltpu.VMEM((ROW_BLK * G_PITCH, N_KEYS), F32),
                        pltpu.VMEM((ROW_BLK, D), F32)],
        input_output_aliases={6: 0},
        compiler_params=_cparams(("arbitrary", "arbitrary", "arbitrary")),
        name="peer_expert",
    )(h, i1, i2, g, u, v, x, tab)


def peer_ffn(x, h, wq, keys, u, v, tab):
    nb = x.shape[0]
    q = matmul(h.reshape(nb * T_ALL, D), wq)
    i1, i2, g = peer_topk(q, keys)
    shp = (nb, T_ALL, PEER_HEADS * PEER_TOPK)
    return peer_expert(x, h, i1.reshape(shp), i2.reshape(shp), g.reshape(shp), u, v, tab)


def _head_ones(width, group):
    r = np.arange(width) // group
    return jnp.asarray((r[:, None] == r[None, :]).astype(np.float32))


def _rw_pre_kernel(z_ref, hp_ref, hn_ref, mu_ref, w0_ref, w2_ref, a0_ref, a2_ref, g2_ref,
                   kk_ref, ka_ref, ones_ref,
                   r_ref, k_ref, v_ref, g_ref, na_ref, wf_ref, wb_ref, bf_ref, bb_ref,
                   kf_ref, kb_ref):
    z = z_ref[0]
    row = lax.broadcasted_iota(jnp.int32, (ROW_BLK, 1), 0)
    prev = jnp.where(row == 0, hp_ref[0, 0], pltpu.roll(z, 1, 0))
    nxt = jnp.where(row == ROW_BLK - 1, hn_ref[0, 0], pltpu.roll(z, ROW_BLK - 1, 0))
    zs = z + mu_ref[0:1, :] * (prev - z) + mu_ref[1:2, :] * (nxt - z)
    r = zs[:, 0:RW_W]
    k = zs[:, RW_W:2 * RW_W]
    v = zs[:, 2 * RW_W:3 * RW_W]
    o = 3 * RW_W
    lw = jnp.tanh(zs[:, o:o + 2 * LORA])
    la = zs[:, o + 2 * LORA:o + 4 * LORA]
    lg = jax.nn.sigmoid(zs[:, o + 4 * LORA:o + 4 * LORA + LORA_G])
    r_ref[0] = r
    k_ref[0] = k
    v_ref[0] = v
    g_ref[0] = _dot(lg, g2_ref[...], precision=HI)
    kkr = k * kk_ref[...]
    ss = _dot(kkr * kkr, ones_ref[...], precision=HI)
    kkn = kkr / jnp.maximum(jnp.sqrt(ss), 1e-12)
    na_ref[0] = -kkn
    for d, (w_ref, b_ref, kd_ref) in enumerate(((wf_ref, bf_ref, kf_ref), (wb_ref, bb_ref, kb_ref))):
        dec = w0_ref[d:d + 1, :] + _dot(lw, w2_ref[d], precision=HI)
        w_ref[0] = jnp.exp(-W_DECAY_SCALE * jax.nn.sigmoid(dec))
        a = jax.nn.sigmoid(a0_ref[d:d + 1, :] + _dot(la, a2_ref[d], precision=HI))
        kd_ref[0] = k * (1.0 + (a - 1.0) * ka_ref[...])
        b_ref[0] = kkn * a


def rw_pre(z, mu, w0, w2, a0, a2, g2, kk, ka):
    nb = z.shape[0]
    zr = z[:, :, :RW_COLS]
    zero = jnp.zeros((nb, 1, RW_COLS), F32)
    last = zr[:, ROW_BLK - 1::ROW_BLK]
    first = zr[:, ::ROW_BLK]
    halo_prev = jnp.concatenate([zero, zero, last[:, 1:N_ROW_BLK - 1]], axis=1)
    halo_next = jnp.concatenate([zero, first[:, 2:], zero], axis=1)
    halo_prev = halo_prev.reshape(nb, N_ROW_BLK, 1, RW_COLS)
    halo_next = halo_next.reshape(nb, N_ROW_BLK, 1, RW_COLS)
    zpad = jnp.zeros((LORA, RW_W), F32)
    w2p = jnp.stack([jnp.concatenate([w2[0], zpad]), jnp.concatenate([zpad, w2[1]])])
    a2p = jnp.stack([jnp.concatenate([a2[0], zpad]), jnp.concatenate([zpad, a2[1]])])
    row = lambda b, i: (b, i, 0)
    full = lambda shape: pl.BlockSpec(shape, lambda b, i: (0,) * len(shape))
    out = jax.ShapeDtypeStruct((nb, T_ALL, RW_W), F32)
    ospec = pl.BlockSpec((1, ROW_BLK, RW_W), row)
    return pl.pallas_call(
        _rw_pre_kernel,
        grid=(nb, N_ROW_BLK),
        in_specs=[pl.BlockSpec((1, ROW_BLK, RW_COLS), row),
                  pl.BlockSpec((1, 1, 1, RW_COLS), lambda b, i: (b, i, 0, 0)),
                  pl.BlockSpec((1, 1, 1, RW_COLS), lambda b, i: (b, i, 0, 0)),
                  full((2, RW_COLS)), full((2, RW_W)), full((2, 2 * LORA, RW_W)),
                  full((2, RW_W)), full((2, 2 * LORA, RW_W)), full((LORA_G, RW_W)),
                  full((1, RW_W)), full((1, RW_W)), full((RW_W, RW_W))],
        out_specs=[ospec] * 11,
        out_shape=[out] * 11,
        compiler_params=_cparams(("arbitrary", "arbitrary")),
        name="rw_pre",
    )(z, halo_prev, halo_next, mu, w0, w2p, a0, a2p, g2, kk.reshape(1, RW_W),
      ka.reshape(1, RW_W), _head_ones(RW_W, HEAD))


SCAN_TB = 32
SCAN_I = HEAD // 2


def _rw_scan_kernel(a_ref, w_ref, b_ref, k_ref, r_ref, v_ref, y_ref, s_ref, sa_ref):
    @pl.when(pl.program_id(0) == 0)
    def _():
        s_ref[...] = jnp.zeros_like(s_ref)

    def step(t, carry):
        a = a_ref[t]
        for i in range(SCAN_I):
            sa_ref[pl.ds(i, 1), :] = jnp.sum(s_ref[i] * a, axis=0, keepdims=True)
        w = w_ref[t]
        b = b_ref[t]
        k = k_ref[t]
        r = r_ref[t]
        for i in range(SCAN_I):
            sn = s_ref[i] * w + sa_ref[pl.ds(i, 1), :] * b + v_ref[t, pl.ds(i, 1), :] * k
            s_ref[i] = sn
            y_ref[t, pl.ds(i, 1), :] = jnp.sum(sn * r, axis=0, keepdims=True)
        return carry

    lax.fori_loop(0, SCAN_TB, step, 0)


def rw_scan(a, w, b, k, r, v):
    t = a.shape[0]
    jspec = pl.BlockSpec((SCAN_TB, HEAD, LANES), lambda s: (s, 0, 0))
    ispec = pl.BlockSpec((SCAN_TB, SCAN_I, LANES), lambda s: (s, 0, 0))
    return pl.pallas_call(
        _rw_scan_kernel,
        grid=(t // SCAN_TB,),
        in_specs=[jspec] * 5 + [ispec],
        out_specs=ispec,
        out_shape=jax.ShapeDtypeStruct((t, SCAN_I, LANES), F32),
        scratch_shapes=[pltpu.VMEM((SCAN_I, HEAD, LANES), F32),
                        pltpu.VMEM((SCAN_I, LANES), F32)],
        compiler_params=_cparams(("arbitrary",)),
        name="rw_scan",
    )(a, w, b, k, r, v)


def _segflip(a):
    return jnp.concatenate([a[:, :T_CTX][:, ::-1], a[:, T_CTX:][:, ::-1]], axis=1)


def _scan_layout_j(fwd, bwd):
    nb = fwd.shape[0]
    s = jnp.stack([fwd, _segflip(bwd)], 0).reshape(2, nb, T_ALL, RW_H, HEAD)
    s = s.transpose(2, 4, 0, 1, 3).reshape(T_ALL, HEAD, 2 * nb * RW_H)
    return jnp.concatenate([s, s], axis=-1)


def _scan_layout_i(v):
    nb = v.shape[0]
    s = jnp.stack([v, _segflip(v)], 0).reshape(2, nb, T_ALL, RW_H, 2, SCAN_I)
    return s.transpose(2, 5, 4, 0, 1, 3).reshape(T_ALL, SCAN_I, 4 * nb * RW_H)


def _scan_unlayout_i(y, nb):
    s = y.reshape(T_ALL, SCAN_I, 2, 2, nb, RW_H).transpose(3, 4, 0, 5, 2, 1)
    s = s.reshape(2, nb, T_ALL, RW_W)
    return s[0], _segflip(s[1])


def _rw_post_kernel(yf_ref, yb_ref, r_ref, k_ref, v_ref, g_ref, rk_ref, lw_ref, lb_ref, ones_ref,
                    o_ref):
    y = yf_ref[0] + yb_ref[0]
    ones = ones_ref[...]
    mean = _dot(y, ones, precision=HI) * (1.0 / HEAD)
    yc = y - mean
    var = _dot(yc * yc, ones, precision=HI) * (1.0 / HEAD)
    yn = yc * lax.rsqrt(var + RW_LN_EPS) * lw_ref[...] + lb_ref[...]
    bonus = _dot(r_ref[0] * k_ref[0] * rk_ref[...], ones, precision=HI) * v_ref[0]
    o_ref[0] = ((yn + bonus) * g_ref[0]).astype(BF16)


def rw_post(yf, yb, r, k, v, g, rk, ln_w, ln_b):
    nb = yf.shape[0]
    row = lambda b, i: (b, i, 0)
    spec = pl.BlockSpec((1, ROW_BLK, RW_W), row)
    vec = pl.BlockSpec((1, RW_W), lambda b, i: (0, 0))
    return pl.pallas_call(
        _rw_post_kernel,
        grid=(nb, N_ROW_BLK),
        in_specs=[spec] * 6 + [vec] * 3 + [pl.BlockSpec((RW_W, RW_W), lambda b, i: (0, 0))],
        out_specs=spec,
        out_shape=jax.ShapeDtypeStruct((nb, T_ALL, RW_W), BF16),
        compiler_params=_cparams(("arbitrary", "arbitrary")),
        name="rw_post",
    )(yf, yb, r, k, v, g, rk.reshape(1, RW_W), ln_w.reshape(1, RW_W), ln_b.reshape(1, RW_W),
      _head_ones(RW_W, HEAD))


def rwkv_mixer(z, mu, w0, w2, a0, a2, g2, kk, ka, rk, ln_w, ln_b):
    nb = z.shape[0]
    r, k, v, g, na, wf, wb, bf, bb, kf, kb = rw_pre(z, mu, w0, w2, a0, a2, g2, kk, ka)
    y = rw_scan(_scan_layout_j(na, na), _scan_layout_j(wf, wb), _scan_layout_j(bf, bb),
                _scan_layout_j(kf, kb), _scan_layout_j(r, r), _scan_layout_i(v))
    yf, yb = _scan_unlayout_i(y, nb)
    return rw_post(yf, yb, r, k, v, g, rk, ln_w, ln_b)


NA_QROWS = ROW_BLK // GRID_W
NA_SLAB = NA_KH + NA_QROWS - 1
NA_SLAB_T = NA_SLAB * GRID_W
NA_SCALE = HEAD ** -0.5


def _na_bias_index():
    a = np.arange(NA_QROWS)[:, None]
    u = np.arange(NA_SLAB)[None, :]
    idx_r, valid = [], []
    for r0, u0 in ((0, 0), (NA_QROWS, 0), (GRID_ROWS - NA_QROWS, GRID_ROWS - NA_SLAB)):
        r = r0 + a
        kr = u0 + u
        kr0 = np.clip(r - NA_KH // 2, 0, GRID_ROWS - NA_KH)
        valid.append((kr >= kr0) & (kr < kr0 + NA_KH))
        idx_r.append(np.clip(kr - r + NA_KH - 1, 0, 2 * NA_KH - 2))
    idx_r = np.stack(idx_r)[:, :, None, :, None]
    valid = np.stack(valid)[:, :, None, :, None]
    qc = np.arange(GRID_W)[:, None]
    kc = np.arange(GRID_W)[None, :]
    c0 = np.clip(qc - NA_KW // 2, 0, GRID_W - NA_KW)
    in_win = ((kc >= c0) & (kc < c0 + NA_KW))[None, None, :, None, :]
    idx_c = np.clip(kc - qc + NA_KW - 1, 0, 2 * NA_KW - 2)[None, None, :, None, :]
    shape = (3, NA_QROWS, GRID_W, NA_SLAB, GRID_W)
    return (np.broadcast_to(idx_r, shape), np.broadcast_to(idx_c, shape),
            np.broadcast_to(valid & in_win, shape))


def na_bias_tables(rpb):
    ir, ic, mask = _na_bias_index()
    b = jnp.where(mask[None], rpb[:, ir, ic], NEG_INF)
    return b.reshape(NA_H, 3, ROW_BLK, NA_SLAB_T)


def _na_kernel(q_ref, k_ref, v_ref, bias_ref, o_ref):
    qi = pl.program_id(2)

    @pl.when(qi == 0)
    def _():
        ys = []
        for hh in range(2):
            ln = slice(hh * HEAD, (hh + 1) * HEAD)
            q = (q_ref[0, :, ln] * NA_SCALE).astype(BF16)
            s = _dot_nt(q, k_ref[0, 0:T_CTX, ln].astype(BF16))
            p = jnp.exp(s - jnp.max(s, axis=-1, keepdims=True))
            y = _dot(p.astype(BF16), v_ref[0, 0:T_CTX, ln].astype(BF16))
            ys.append(y / jnp.sum(p, axis=-1, keepdims=True))
        o_ref[0] = jnp.concatenate(ys, axis=1).astype(BF16)

    @pl.when(qi > 0)
    def _():
        u0 = jnp.clip(NA_QROWS * (qi - 1) - NA_KH // 2, 0, GRID_ROWS - NA_SLAB)
        start = pl.multiple_of(T_CTX + GRID_W * u0, GRID_W)
        ys = []
        for hh in range(2):
            ln = slice(hh * HEAD, (hh + 1) * HEAD)
            q = (q_ref[0, :, ln] * NA_SCALE).astype(BF16)
            sc = _dot_nt(q, k_ref[0, 0:T_CTX, ln].astype(BF16))
            sw = _dot_nt(q, k_ref[0, pl.ds(start, NA_SLAB_T), ln].astype(BF16)) + bias_ref[hh, 0]
            m = jnp.maximum(jnp.max(sc, axis=-1, keepdims=True), jnp.max(sw, axis=-1, keepdims=True))
            pc = jnp.exp(sc - m)
            pw = jnp.exp(sw - m)
            y = (_dot(pw.astype(BF16), v_ref[0, pl.ds(start, NA_SLAB_T), ln].astype(BF16))
                 + _dot(pc.astype(BF16), v_ref[0, 0:T_CTX, ln].astype(BF16)))
            den = jnp.sum(pc, axis=-1, keepdims=True) + jnp.sum(pw, axis=-1, keepdims=True)
            ys.append(y / den)
        o_ref[0] = jnp.concatenate(ys, axis=1).astype(BF16)


def natten_mixer(z, bias):
    nb = z.shape[0]
    qb, kb, vb = (NA_OFF // LANES, (NA_OFF + NA_W) // LANES, (NA_OFF + 2 * NA_W) // LANES)
    n_blk = N_ROW_BLK - 1

    def bias_idx(b, hp, qi):
        var = jnp.where(qi <= 1, 0, jnp.where(qi == n_blk, 2, 1))
        return (hp, var, 0, 0)

    return pl.pallas_call(
        _na_kernel,
        grid=(nb, NA_H // 2, N_ROW_BLK),
        in_specs=[pl.BlockSpec((1, ROW_BLK, LANES), lambda b, hp, qi: (b, qi, qb + hp)),
                  pl.BlockSpec((1, T_ALL, LANES), lambda b, hp, qi: (b, 0, kb + hp)),
                  pl.BlockSpec((1, T_ALL, LANES), lambda b, hp, qi: (b, 0, vb + hp)),
                  pl.BlockSpec((2, 1, ROW_BLK, NA_SLAB_T), bias_idx)],
        out_specs=pl.BlockSpec((1, ROW_BLK, LANES), lambda b, hp, qi: (b, qi, hp)),
        out_shape=jax.ShapeDtypeStruct((nb, T_ALL, NA_W), BF16),
        compiler_params=_cparams(("arbitrary", "arbitrary", "arbitrary")),
        name="natten",
    )(z, z, z, bias)


ROPE_NF = MB_N // 4


def rope_tables():
    pos = np.arange(T_LAT)
    inv = ROPE_BASE ** (-np.arange(ROPE_NF, dtype=np.float32) / ROPE_NF)
    lane = np.arange(MB_N)
    p = np.where(lane[None, :] < MB_N // 2, (pos // GRID_W)[:, None], (pos % GRID_W)[:, None])
    ang = p.astype(np.float32) * inv[lane % ROPE_NF][None, :]
    sign = np.where((lane % (2 * ROPE_NF)) < ROPE_NF, -1.0, 1.0)[None, :]
    cos = np.concatenate([np.ones((T_CTX, MB_N), np.float32), np.cos(ang)])
    sin = np.concatenate([np.zeros((T_CTX, MB_N), np.float32), np.sin(ang) * sign])
    return jnp.asarray(cos, F32), jnp.asarray(sin, F32)


def _mb_pre_kernel(z_ref, w_ref, b_ref, cos_ref, sin_ref, o_ref):
    z = z_ref[0]
    pos = lax.broadcasted_iota(jnp.int32, (T_ALL, 1), 0)
    lo = jnp.where(pos < T_CTX, 0, T_CTX)
    hi = jnp.where(pos < T_CTX, T_CTX, T_ALL)
    half = MB_CONV // 2
    acc = z * w_ref[half:half + 1, :] + b_ref[...]
    for d in (-2, -1, 1, 2):
        zr = pltpu.roll(z, (-d) % T_ALL, 0)
        ok = (pos + d >= lo) & (pos + d < hi)
        acc = acc + jnp.where(ok, zr, 0.0) * w_ref[half + d:half + d + 1, :]
    y = acc * jax.nn.sigmoid(acc)
    is_bc = pl.program_id(1) >= MB_INNER // LANES

    @pl.when(jnp.logical_not(is_bc))
    def _():
        o_ref[0] = y

    @pl.when(is_bc)
    def _():
        lane = lax.broadcasted_iota(jnp.int32, (1, MB_N), 1)
        first = (lane % (2 * ROPE_NF)) < ROPE_NF
        partner = jnp.where(first, pltpu.roll(y, MB_N - ROPE_NF, 1), pltpu.roll(y, ROPE_NF, 1))
        o_ref[0] = y * cos_ref[...] + partner * sin_ref[...]


def mb_pre(z, conv_w, conv_b, cos, sin):
    nb = z.shape[0]
    c0 = XBC_OFF // LANES
    return pl.pallas_call(
        _mb_pre_kernel,
        grid=(nb, MB_CONV_CH // LANES),
        in_specs=[pl.BlockSpec((1, T_ALL, LANES), lambda b, j: (b, 0, c0 + j)),
                  pl.BlockSpec((MB_CONV, LANES), lambda b, j: (0, j)),
                  pl.BlockSpec((1, LANES), lambda b, j: (0, j)),
                  pl.BlockSpec((T_ALL, MB_N), lambda b, j: (0, 0)),
                  pl.BlockSpec((T_ALL, MB_N), lambda b, j: (0, 0))],
        out_specs=pl.BlockSpec((1, T_ALL, LANES), lambda b, j: (b, 0, j)),
        out_shape=jax.ShapeDtypeStruct((nb, T_ALL, MB_CONV_CH), F32),
        compiler_params=_cparams(("arbitrary", "arbitrary")),
        name="mb_pre",
    )(z, conv_w, conv_b.reshape(1, MB_CONV_CH), cos, sin)


N_CHUNK = T_ALL // MB_CHUNK
N_CTX_CHUNK = T_CTX // MB_CHUNK
XB_BLK = MB_INNER // LANES
XC_BLK = XB_BLK + MB_G


def _ssd_direction(reverse, x_ref, b_ref, c_ref, dtc_ref, dtr_ref, bc_ref, br_ref, ac_ref, ar_ref,
                   st_ref, y_ref):
    L = MB_CHUNK
    dtc = jax.nn.softplus(dtc_ref[0, 0, 0] + bc_ref[0, 0])
    dtr = jax.nn.softplus(dtr_ref[0, 0, 0] + br_ref[0, 0])
    dac = dtc * (-jnp.exp(ac_ref[0, 0]))
    dar = dtr * (-jnp.exp(ar_ref[0, 0]))
    ri = lax.broadcasted_iota(jnp.int32, (L, L), 0)
    ci = lax.broadcasted_iota(jnp.int32, (L, L), 1)
    lower = (ri >= ci).astype(F32)
    upper = (ri <= ci).astype(F32)
    if reverse:
        cum_c = _dot(upper, dac, precision=HI)
        cum_r = _dot(dar, lower, precision=HI)
        mask = ri <= ci
        tot_row = 0
    else:
        cum_c = _dot(lower, dac, precision=HI)
        cum_r = _dot(dar, upper, precision=HI)
        mask = ri >= ci
        tot_row = L - 1
    bm = b_ref[0]
    cm = c_ref[0].astype(BF16)
    cb = _dot_nt(cm, bm.astype(BF16))
    bt = bm.T.astype(BF16)
    x = x_ref[0]
    ys = []
    for r in range(MB_R):
        cc = cum_c[:, r:r + 1]
        cr = cum_r[r:r + 1, :]
        tot = cc[tot_row:tot_row + 1, :]
        lmat = jnp.exp(jnp.where(mask, cc - cr, -jnp.inf))
        xc = x[:, r * HEAD:(r + 1) * HEAD] * dtc[:, r:r + 1]
        y_diag = _dot((cb * lmat).astype(BF16), xc.astype(BF16))
        st = st_ref[r]
        y_off = _dot(cm, st.astype(BF16)) * jnp.exp(cc)
        xdec = (xc * jnp.exp(tot - cc)).astype(BF16)
        st_ref[r] = jnp.exp(tot) * st + _dot(bt, xdec)
        ys.append(y_diag + y_off)
    y_ref[0] = jnp.concatenate(ys, axis=1)


def _mb_ssd_kernel(*refs):
    fwd, bwd = refs[0:9], refs[9:18]
    yf_ref, yb_ref, stf_ref, stb_ref = refs[18:22]

    @pl.when(pl.program_id(2) == 0)
    def _():
        stf_ref[...] = jnp.zeros_like(stf_ref)
        stb_ref[...] = jnp.zeros_like(stb_ref)

    _ssd_direction(False, *fwd, stf_ref, yf_ref)
    _ssd_direction(True, *bwd, stb_ref, yb_ref)


def _bwd_chunk(i):
    return jnp.where(i < N_CTX_CHUNK, N_CTX_CHUNK - 1 - i, N_CHUNK + N_CTX_CHUNK - 1 - i)


def mb_ssd(xbc, dt_raw, dt_bias, a_log):
    nb = xbc.shape[0]
    dt5 = dt_raw.reshape(nb, T_ALL, 2, MB_G, MB_R)
    dtc = dt5.transpose(0, 2, 3, 1, 4)
    dtr = dt5.transpose(0, 2, 3, 4, 1)
    b4 = dt_bias.reshape(2, MB_G, MB_R)
    a4 = a_log.reshape(2, MB_G, MB_R)
    ins, specs = [], []
    for d, cidx in ((0, lambda i: i), (1, _bwd_chunk)):
        ins += [xbc, xbc, xbc, dtc, dtr, b4[:, :, None, :], b4[:, :, :, None],
                a4[:, :, None, :], a4[:, :, :, None]]
        specs += [
            pl.BlockSpec((1, MB_CHUNK, MB_R * HEAD), lambda b, g, i, c=cidx: (b, c(i), g)),
            pl.BlockSpec((1, MB_CHUNK, MB_N), lambda b, g, i, c=cidx: (b, c(i), XB_BLK + g)),
            pl.BlockSpec((1, MB_CHUNK, MB_N), lambda b, g, i, c=cidx: (b, c(i), XC_BLK + g)),
            pl.BlockSpec((1, 1, 1, MB_CHUNK, MB_R), lambda b, g, i, c=cidx, d=d: (b, d, g, c(i), 0)),
            pl.BlockSpec((1, 1, 1, MB_R, MB_CHUNK), lambda b, g, i, c=cidx, d=d: (b, d, g, 0, c(i))),
            pl.BlockSpec((1, 1, 1, MB_R), lambda b, g, i, d=d: (d, g, 0, 0)),
            pl.BlockSpec((1, 1, MB_R, 1), lambda b, g, i, d=d: (d, g, 0, 0)),
            pl.BlockSpec((1, 1, 1, MB_R), lambda b, g, i, d=d: (d, g, 0, 0)),
            pl.BlockSpec((1, 1, MB_R, 1), lambda b, g, i, d=d: (d, g, 0, 0)),
        ]
    out = jax.ShapeDtypeStruct((nb, T_ALL, MB_INNER), F32)
    return pl.pallas_call(
        _mb_ssd_kernel,
        grid=(nb, MB_G, N_CHUNK),
        in_specs=specs,
        out_specs=[pl.BlockSpec((1, MB_CHUNK, MB_R * HEAD), lambda b, g, i: (b, i, g)),
                   pl.BlockSpec((1, MB_CHUNK, MB_R * HEAD), lambda b, g, i: (b, _bwd_chunk(i), g))],
        out_shape=[out, out],
        scratch_shapes=[pltpu.VMEM((MB_R, MB_N, HEAD), F32), pltpu.VMEM((MB_R, MB_N, HEAD), F32)],
        compiler_params=_cparams(("arbitrary", "arbitrary", "arbitrary")),
        name="mb_ssd",
    )(*ins)


MB_GW = MB_INNER // MB_G


def _mb_post_kernel(yf_ref, yb_ref, x_ref, glo_ref, ghi_ref, d_ref, nw_ref, o_ref):
    y = yf_ref[0] + yb_ref[0] + x_ref[0] * d_ref[...]
    gate = jnp.concatenate([glo_ref[0], ghi_ref[0]], axis=1)
    y = y * (gate * jax.nn.sigmoid(gate))
    ms = jnp.mean(y * y, axis=-1, keepdims=True)
    o_ref[0] = (y * lax.rsqrt(ms + NORM_EPS) * nw_ref[...]).astype(BF16)


def mb_post(yf, yb, xbc, z, d_skip, norm_w):
    nb = yf.shape[0]
    g0 = MB_OFF // LANES
    grp = lambda b, i, g: (b, i, g)
    return pl.pallas_call(
        _mb_post_kernel,
        grid=(nb, N_ROW_BLK, MB_G),
        in_specs=[pl.BlockSpec((1, ROW_BLK, MB_GW), grp),
                  pl.BlockSpec((1, ROW_BLK, MB_GW), grp),
                  pl.BlockSpec((1, ROW_BLK, MB_GW), grp),
                  pl.BlockSpec((1, ROW_BLK, LANES), lambda b, i, g: (b, i, g0 + 2 * g)),
                  pl.BlockSpec((1, ROW_BLK, LANES), lambda b, i, g: (b, i, g0 + 2 * g + 1)),
                  pl.BlockSpec((1, MB_GW), lambda b, i, g: (0, g)),
                  pl.BlockSpec((1, MB_GW), lambda b, i, g: (0, g))],
        out_specs=pl.BlockSpec((1, ROW_BLK, MB_GW), grp),
        out_shape=jax.ShapeDtypeStruct((nb, T_ALL, MB_INNER), BF16),
        compiler_params=_cparams(("arbitrary", "arbitrary", "arbitrary")),
        name="mb_post",
    )(yf, yb, xbc, z, z, jnp.repeat(d_skip, HEAD).reshape(1, MB_INNER), norm_w.reshape(1, MB_INNER))


def mamba_mixer(z, conv_w, conv_b, dt_bias, a_log, d_skip, norm_w, cos, sin):
    xbc = mb_pre(z, conv_w, conv_b, cos, sin)
    yf, yb = mb_ssd(xbc, z[:, :, DT_OFF:DT_OFF + 2 * MB_H], dt_bias, a_log)
    return mb_post(yf, yb, xbc, z, d_skip, norm_w)


def kernel(x, c, ctx, c_ctx, ada_w, ada_b, norm1_w, norm2_w, w_in, w_out, rw_mu, rw_w0, rw_w2, rw_a0, rw_a2, rw_g2, rw_kk, rw_ka, rw_rk, rw_ln_w, rw_ln_b, na_rpb, mb_conv_w, mb_conv_b, mb_dt_bias, mb_a_log, mb_d, mb_norm_w, pe_wq, pe_keys, pe_u, pe_v, final_norm_w):
    nb = x.shape[0]
    xs = jnp.concatenate([ctx, x], axis=1)
    cond = jnp.zeros((SUBLANES, D), F32).at[:nb].set(c).at[nb].set(c_ctx)
    mods = ada_rows(cond, ada_w, ada_b)
    cos, sin = rope_tables()
    for l in range(DEPTH):
        m = mods[l].reshape(SUBLANES, 6, D)
        tab = jnp.stack([jnp.broadcast_to(m[nb], (nb, 6, D)), m[:nb]], axis=1).reshape(2 * nb, 6, D)
        h1 = norm_mod(xs, norm1_w[l].reshape(1, D), tab, 0)
        z = matmul(h1.reshape(nb * T_ALL, D), cast_bf16(w_in[l], ROW_BLK)).reshape(nb, T_ALL, IN_COLS)
        y_rw = rwkv_mixer(z, rw_mu[l], rw_w0[l], rw_w2[l], rw_a0[l], rw_a2[l], rw_g2[l], rw_kk[l],
                          rw_ka[l], rw_rk[l].reshape(RW_W), rw_ln_w[l], rw_ln_b[l])
        y_na = natten_mixer(z, na_bias_tables(na_rpb[l]))
        y_mb = mamba_mixer(z, mb_conv_w[l], mb_conv_b[l], mb_dt_bias[l], mb_a_log[l], mb_d[l],
                           mb_norm_w[l], cos, sin)
        xs = out_proj(xs, y_rw, y_na, y_mb, cast_bf16(w_out[l], ROW_BLK), tab)
        h2 = norm_mod(xs, norm2_w[l].reshape(1, D), tab, 3)
        xs = peer_ffn(xs, h2, cast_bf16(pe_wq[l], ROW_BLK), pe_keys[l],
                      cast_bf16(pe_u[l], 1024), cast_bf16(pe_v[l], 1024), tab)
    return final_norm(xs, final_norm_w.reshape(1, D))
```

```python
import functools
import math

import numpy as np
import jax
import jax.numpy as jnp
from jax import lax
from jax.experimental import pallas as pl
from jax.experimental.pallas import tpu as pltpu

D = 2048
NB = 4
T_LAT = 2048
T_CTX = 256
T_ALL = T_CTX + T_LAT
DEPTH = 4
GRID_W = 64
GRID_ROWS = T_LAT // GRID_W
HEAD = 64
RW_W = 512
RW_H = 8
LORA = 64
LORA_G = 128
W_DECAY_SCALE = 0.606531
RW_LN_EPS = 64e-5
NA_W = 512
NA_H = 8
NA_KH = 8
NA_KW = 16
MB_INNER = 1024
MB_H = 16
MB_G = 4
MB_R = 4
MB_N = 128
MB_CONV = 5
MB_CHUNK = 128
ROPE_BASE = 10000.0
PEER_HEADS = 8
N_KEYS = 128
PEER_TOPK = 16
NORM_EPS = 1e-6
NEG_INF = -1e30
RW_COLS = 3 * RW_W + 4 * LORA + LORA_G
NA_COLS = 3 * NA_W
MB_CONV_CH = MB_INNER + 2 * MB_G * MB_N
MB_COLS = MB_INNER + MB_CONV_CH + 2 * MB_H
IN_COLS = RW_COLS + NA_COLS + MB_COLS
NA_OFF = RW_COLS
MB_OFF = RW_COLS + NA_COLS
XBC_OFF = MB_OFF + MB_INNER
DT_OFF = XBC_OFF + MB_CONV_CH

LANES = 128
SUBLANES = 8
VMEM_LIMIT = 56 * 1024 * 1024

ROW_BLK = 256
N_ROW_BLK = T_ALL // ROW_BLK
BF16 = jnp.bfloat16
F32 = jnp.float32
HI = lax.Precision.HIGHEST


def _cparams(sem):
    return pltpu.CompilerParams(dimension_semantics=sem, vmem_limit_bytes=VMEM_LIMIT)


def _dot(a, b, precision=None):
    return jnp.dot(a, b, preferred_element_type=F32, precision=precision)


def _dot_nt(a, b, precision=None):
    return lax.dot_general(a, b, (((1,), (1,)), ((), ())), preferred_element_type=F32,
                           precision=precision)


def _cast_kernel(x_ref, o_ref):
    o_ref[...] = x_ref[...].astype(BF16)


def cast_bf16(w, rows_blk):
    r, c = w.shape
    return pl.pallas_call(
        _cast_kernel,
        grid=(r // rows_blk,),
        in_specs=[pl.BlockSpec((rows_blk, c), lambda i: (i, 0))],
        out_specs=pl.BlockSpec((rows_blk, c), lambda i: (i, 0)),
        out_shape=jax.ShapeDtypeStruct((r, c), BF16),
        compiler_params=_cparams(("arbitrary",)),
        name="cast_bf16",
    )(w)


ADA_TN = 1024


def _ada_kernel(c_ref, w_ref, b_ref, o_ref):
    c = c_ref[...]
    s = c * jax.nn.sigmoid(c)
    o_ref[0] = _dot(s, w_ref[0], precision=HI) + b_ref[0]


def ada_rows(cond, ada_w, ada_b):
    nl = ada_w.shape[0]
    return pl.pallas_call(
        _ada_kernel,
        grid=(nl, 6 * D // ADA_TN),
        in_specs=[pl.BlockSpec((SUBLANES, D), lambda l, j: (0, 0)),
                  pl.BlockSpec((1, D, ADA_TN), lambda l, j: (l, 0, j)),
                  pl.BlockSpec((1, 1, ADA_TN), lambda l, j: (l, 0, j))],
        out_specs=pl.BlockSpec((1, SUBLANES, ADA_TN), lambda l, j: (l, 0, j)),
        out_shape=jax.ShapeDtypeStruct((nl, SUBLANES, 6 * D), F32),
        compiler_params=_cparams(("arbitrary", "arbitrary")),
        name="ada_rows",
    )(cond, ada_w, ada_b.reshape(nl, 1, 6 * D))


def _mod_index(b, i):
    return 2 * b + jnp.minimum(i, 1)


def _norm_mod_kernel(which, x_ref, nw_ref, tab_ref, o_ref):
    x = x_ref[0]
    ms = jnp.mean(x * x, axis=-1, keepdims=True)
    y = x * lax.rsqrt(ms + NORM_EPS) * nw_ref[...]
    shift = tab_ref[0, which:which + 1, :]
    scale = tab_ref[0, which + 1:which + 2, :]
    o_ref[0] = (y * (1.0 + scale) + shift).astype(BF16)


def norm_mod(x, nw, tab, which):
    nb = x.shape[0]
    return pl.pallas_call(
        functools.partial(_norm_mod_kernel, which),
        grid=(nb, N_ROW_BLK),
        in_specs=[pl.BlockSpec((1, ROW_BLK, D), lambda b, i: (b, i, 0)),
                  pl.BlockSpec((1, D), lambda b, i: (0, 0)),
                  pl.BlockSpec((1, 6, D), lambda b, i: (_mod_index(b, i), 0, 0))],
        out_specs=pl.BlockSpec((1, ROW_BLK, D), lambda b, i: (b, i, 0)),
        out_shape=jax.ShapeDtypeStruct(x.shape, BF16),
        compiler_params=_cparams(("arbitrary", "arbitrary")),
        name="norm_mod",
    )(x, nw, tab)


def _resid_norm_mod_kernel(x_ref, f_ref, ptab_ref, nw_ref, tab_ref, xo_ref, o_ref):
    x = x_ref[0] + ptab_ref[0, 5:6, :] * f_ref[0]
    xo_ref[0] = x
    ms = jnp.mean(x * x, axis=-1, keepdims=True)
    y = x * lax.rsqrt(ms + NORM_EPS) * nw_ref[...]
    o_ref[0] = (y * (1.0 + tab_ref[0, 1:2, :]) + tab_ref[0, 0:1, :]).astype(BF16)


def resid_norm_mod(x, f, prev_tab, nw, tab):
    nb = x.shape[0]
    row = lambda b, i: (b, i, 0)
    mod = lambda b, i: (_mod_index(b, i), 0, 0)
    return pl.pallas_call(
        _resid_norm_mod_kernel,
        grid=(nb, N_ROW_BLK),
        in_specs=[pl.BlockSpec((1, ROW_BLK, D), row),
                  pl.BlockSpec((1, ROW_BLK, D), row),
                  pl.BlockSpec((1, 6, D), mod),
                  pl.BlockSpec((1, D), lambda b, i: (0, 0)),
                  pl.BlockSpec((1, 6, D), mod)],
        out_specs=[pl.BlockSpec((1, ROW_BLK, D), row), pl.BlockSpec((1, ROW_BLK, D), row)],
        out_shape=[jax.ShapeDtypeStruct(x.shape, F32), jax.ShapeDtypeStruct(x.shape, BF16)],
        input_output_aliases={0: 0},
        compiler_params=_cparams(("arbitrary", "arbitrary")),
        name="resid_norm_mod",
    )(x, f, prev_tab, nw, tab)


def _final_norm_kernel(x_ref, f_ref, ptab_ref, nw_ref, o_ref):
    x = x_ref[0] + ptab_ref[0, 5:6, :] * f_ref[0]
    ms = jnp.mean(x * x, axis=-1, keepdims=True)
    o_ref[0] = x * lax.rsqrt(ms + NORM_EPS) * nw_ref[...]


def final_norm(x, f, prev_tab, nw):
    nb = x.shape[0]
    lat = lambda b, i: (b, i + 1, 0)
    return pl.pallas_call(
        _final_norm_kernel,
        grid=(nb, T_LAT // ROW_BLK),
        in_specs=[pl.BlockSpec((1, ROW_BLK, D), lat),
                  pl.BlockSpec((1, ROW_BLK, D), lat),
                  pl.BlockSpec((1, 6, D), lambda b, i: (2 * b + 1, 0, 0)),
                  pl.BlockSpec((1, D), lambda b, i: (0, 0))],
        out_specs=pl.BlockSpec((1, ROW_BLK, D), lambda b, i: (b, i, 0)),
        out_shape=jax.ShapeDtypeStruct((nb, T_LAT, D), F32),
        compiler_params=_cparams(("arbitrary", "arbitrary")),
        name="final_norm",
    )(x, f, prev_tab, nw)


MM_TM = 1024
MM_TN = 512


def _mm_kernel(a_ref, w_ref, o_ref):
    o_ref[...] = _dot(a_ref[...], w_ref[...])


def matmul(a, w):
    m, k = a.shape
    n = w.shape[1]
    tm = MM_TM if m % MM_TM == 0 else ROW_BLK
    return pl.pallas_call(
        _mm_kernel,
        grid=(m // tm, pl.cdiv(n, MM_TN)),
        in_specs=[pl.BlockSpec((tm, k), lambda i, j: (i, 0)),
                  pl.BlockSpec((k, MM_TN), lambda i, j: (0, j))],
        out_specs=pl.BlockSpec((tm, MM_TN), lambda i, j: (i, j)),
        out_shape=jax.ShapeDtypeStruct((m, n), F32),
        compiler_params=_cparams(("arbitrary", "arbitrary")),
        name="matmul",
    )(a, w)


def _out_proj_kernel(x_ref, rw_ref, na_ref, mb_ref, w_ref, tab_ref, o_ref):
    acc = _dot(rw_ref[0], w_ref[0:RW_W, :])
    acc += _dot(na_ref[0], w_ref[RW_W:RW_W + NA_W, :])
    acc += _dot(mb_ref[0], w_ref[RW_W + NA_W:, :])
    o_ref[0] = x_ref[0] + tab_ref[0, 2:3, :] * acc


def out_proj(x, y_rw, y_na, y_mb, w, tab):
    nb = x.shape[0]
    row = lambda b, i: (b, i, 0)
    return pl.pallas_call(
        _out_proj_kernel,
        grid=(nb, N_ROW_BLK),
        in_specs=[pl.BlockSpec((1, ROW_BLK, D), row),
                  pl.BlockSpec((1, ROW_BLK, RW_W), row),
                  pl.BlockSpec((1, ROW_BLK, NA_W), row),
                  pl.BlockSpec((1, ROW_BLK, MB_INNER), row),
                  pl.BlockSpec((D, D), lambda b, i: (0, 0)),
                  pl.BlockSpec((1, 6, D), lambda b, i: (_mod_index(b, i), 0, 0))],
        out_specs=pl.BlockSpec((1, ROW_BLK, D), row),
        out_shape=jax.ShapeDtypeStruct(x.shape, F32),
        input_output_aliases={0: 0},
        compiler_params=_cparams(("arbitrary", "arbitrary")),
        name="out_proj",
    )(x, y_rw, y_na, y_mb, w, tab)


TOPK_TT = 128


CAND_COUNTS = (16, 8, 5, 4, 3, 2, 2, 2)
CAND_ROWS = 16 + 8 * 7 + 8
N_HALVES = 2 * PEER_HEADS
N_SLOTS = PEER_HEADS * PEER_TOPK


def _first_max(s, rows, n):
    m = jnp.max(s, axis=0, keepdims=True)
    pos = jnp.min(jnp.where(s == m, rows, float(n)), axis=0, keepdims=True)
    return m, pos


def _peer_topk_kernel(q_ref, keys_ref, i1_ref, i2_ref, g_ref,
                      sc_ref, val_ref, idx_ref, cand_ref, c1_ref, c2_ref, best_ref, e1_ref, e2_ref):
    t = TOPK_TT
    for hp in range(N_HALVES):
        c0 = hp * N_KEYS
        qs = q_ref[:, c0:c0 + N_KEYS].astype(BF16)
        sc_ref[hp] = _dot_nt(keys_ref[hp // 2, hp % 2].astype(BF16), qs)

    rows = lax.broadcasted_iota(jnp.int32, (N_KEYS, t), 0).astype(F32)

    def stage1(r, carry):
        for hp in range(N_HALVES):
            s = sc_ref[hp]
            m, pos = _first_max(s, rows, N_KEYS)
            val_ref[hp, pl.ds(r, 1), :] = m
            idx_ref[hp, pl.ds(r, 1), :] = pos
            sc_ref[hp] = jnp.where(rows == pos, -jnp.inf, s)
        return carry

    lax.fori_loop(0, PEER_TOPK, stage1, 0)

    row8 = lax.broadcasted_iota(jnp.int32, (SUBLANES, t), 0)
    for h in range(PEER_HEADS):
        v1, v2 = val_ref[2 * h], val_ref[2 * h + 1]
        k1, k2 = idx_ref[2 * h], idx_ref[2 * h + 1]
        cand = [v1[0:1] + v2]
        c1 = [jnp.broadcast_to(k1[0:1], (PEER_TOPK, t))]
        c2 = [k2]
        for r1 in range(1, SUBLANES):
            cand.append(jnp.where(row8 < CAND_COUNTS[r1], v1[r1:r1 + 1] + v2[0:SUBLANES], -jnp.inf))
            c1.append(jnp.broadcast_to(k1[r1:r1 + 1], (SUBLANES, t)))
            c2.append(k2[0:SUBLANES])
        cand.append(v1[SUBLANES:] + v2[0:1])
        c1.append(k1[SUBLANES:])
        c2.append(jnp.broadcast_to(k2[0:1], (SUBLANES, t)))
        cand_ref[h] = jnp.concatenate(cand, axis=0)
        c1_ref[h] = jnp.concatenate(c1, axis=0)
        c2_ref[h] = jnp.concatenate(c2, axis=0)

    crow = lax.broadcasted_iota(jnp.int32, (CAND_ROWS, t), 0).astype(F32)

    def stage2(r, carry):
        for h in range(PEER_HEADS):
            cd = cand_ref[h]
            m, pos = _first_max(cd, crow, CAND_ROWS)
            sel = crow == pos
            slot = pl.ds(h * PEER_TOPK + r, 1)
            best_ref[slot, :] = m
            e1_ref[slot, :] = jnp.sum(jnp.where(sel, c1_ref[h], 0.0), axis=0, keepdims=True)
            e2_ref[slot, :] = jnp.sum(jnp.where(sel, c2_ref[h], 0.0), axis=0, keepdims=True)
            cand_ref[h] = jnp.where(sel, -jnp.inf, cd)
        return carry

    lax.fori_loop(0, PEER_TOPK, stage2, 0)

    for h in range(PEER_HEADS):
        sl = slice(h * PEER_TOPK, (h + 1) * PEER_TOPK)
        b = best_ref[sl, :]
        ex = jnp.exp(b - jnp.max(b, axis=0, keepdims=True))
        best_ref[sl, :] = ex / jnp.sum(ex, axis=0, keepdims=True)
    i1_ref[...] = e1_ref[...].T
    i2_ref[...] = e2_ref[...].T
    g_ref[...] = best_ref[...].T


def peer_topk(q, keys):
    n = q.shape[0]
    t = TOPK_TT
    out = jax.ShapeDtypeStruct((n, N_SLOTS), F32)
    spec = pl.BlockSpec((t, N_SLOTS), lambda i: (i, 0))
    return pl.pallas_call(
        _peer_topk_kernel,
        grid=(n // t,),
        in_specs=[pl.BlockSpec((t, D), lambda i: (i, 0)),
                  pl.BlockSpec(keys.shape, lambda i: (0, 0, 0, 0))],
        out_specs=[spec, spec, spec],
        out_shape=[out, out, out],
        scratch_shapes=[pltpu.VMEM((N_HALVES, N_KEYS, t), F32),
                        pltpu.VMEM((N_HALVES, PEER_TOPK, t), F32),
                        pltpu.VMEM((N_HALVES, PEER_TOPK, t), F32),
                        pltpu.VMEM((PEER_HEADS, CAND_ROWS, t), F32),
                        pltpu.VMEM((PEER_HEADS, CAND_ROWS, t), F32),
                        pltpu.VMEM((PEER_HEADS, CAND_ROWS, t), F32),
                        pltpu.VMEM((N_SLOTS, t), F32),
                        pltpu.VMEM((N_SLOTS, t), F32),
                        pltpu.VMEM((N_SLOTS, t), F32)],
        compiler_params=_cparams(("arbitrary",)),
        name="peer_topk",
    )(q, keys)


EXP_TM = 512
EXP_HALF = EXP_TM // 2
EXP_J = 4
EXP_BLK = EXP_J * N_KEYS
N_EXP_BLK = N_KEYS * N_KEYS // EXP_BLK
G_PITCH = N_KEYS + SUBLANES
SQRT_HALF = 0.7071067811865476


def _bf16_bits(x):
    u = lax.bitcast_convert_type(x, jnp.uint32)
    return (u + jnp.uint32(0x7FFF) + ((u >> 16) & jnp.uint32(1))) >> 16


def _peer_expert_kernel(h_ref, i1_ref, i2_ref, g_ref, u_ref, v_ref, o_ref, gs_ref, w_ref):
    jj = pl.program_id(1)

    @pl.when(jj == 0)
    def _():
        o_ref[...] = jnp.zeros_like(o_ref)
        w_ref[...] = jnp.zeros_like(w_ref)
        rows = lax.broadcasted_iota(jnp.int32, (N_KEYS, N_KEYS), 0).astype(F32)

        def gate_matrix(t):
            i1r = i1_ref[pl.ds(t, 1), :]
            i2r = i2_ref[pl.ds(t, 1), :]
            gr = g_ref[pl.ds(t, 1), :]
            at = jnp.where(rows == i1r, gr, 0.0).astype(BF16)
            bt = jnp.where(rows == i2r, 1.0, 0.0).astype(BF16)
            return _dot_nt(at, bt)

        def tok(t, carry):
            packed = _bf16_bits(gate_matrix(t)) | (_bf16_bits(gate_matrix(t + EXP_HALF)) << 16)
            gs_ref[pl.ds(pl.multiple_of(t * G_PITCH, SUBLANES), N_KEYS), :] = packed
            return carry

        lax.fori_loop(0, EXP_HALF, tok, 0, unroll=2)

    slot = jj % 2
    o_ref[...] += _dot(w_ref[1 - slot], v_ref[...])

    jb = jnp.minimum(jj, N_EXP_BLK - 1)
    s = _dot_nt(h_ref[...], u_ref[...])
    packed = jnp.concatenate(
        [gs_ref[pl.ds(EXP_J * jb + c, EXP_HALF, stride=G_PITCH), :] for c in range(EXP_J)], axis=1)
    g_lo = lax.bitcast_convert_type(packed << 16, F32)
    g_hi = lax.bitcast_convert_type(packed & jnp.uint32(0xFFFF0000), F32)
    g = jnp.concatenate([g_lo, g_hi], axis=0)
    act = 0.5 * s * (1.0 + lax.erf(s * SQRT_HALF))
    w_ref[slot] = (g * act).astype(BF16)


def peer_expert(h, i1, i2, g, u, v):
    n = h.shape[0]
    row = lambda i, j: (i, 0)
    sel = pl.BlockSpec((EXP_TM, N_SLOTS), row)
    return pl.pallas_call(
        _peer_expert_kernel,
        grid=(n // EXP_TM, N_EXP_BLK + 1),
        in_specs=[pl.BlockSpec((EXP_TM, D), row), sel, sel, sel,
                  pl.BlockSpec((EXP_BLK, D), lambda i, j: (jnp.minimum(j, N_EXP_BLK - 1), 0)),
                  pl.BlockSpec((EXP_BLK, D), lambda i, j: (jnp.maximum(j - 1, 0), 0))],
        out_specs=pl.BlockSpec((EXP_TM, D), row),
        out_shape=jax.ShapeDtypeStruct((n, D), F32),
        scratch_shapes=[pltpu.VMEM((EXP_HALF * G_PITCH, N_KEYS), jnp.uint32),
                        pltpu.VMEM((2, EXP_TM, EXP_BLK), BF16)],
        compiler_params=_cparams(("arbitrary", "arbitrary")),
        name="peer_expert",
    )(h, i1, i2, g, u, v)


def peer_ffn(h, wq, keys, u, v):
    nb = h.shape[0]
    hf = h.reshape(nb * T_ALL, D)
    i1, i2, g = peer_topk(matmul(hf, wq), keys)
    return peer_expert(hf, i1, i2, g, u, v).reshape(nb, T_ALL, D)


def _head_ones(width, group):
    r = np.arange(width) // group
    return jnp.asarray((r[:, None] == r[None, :]).astype(np.float32))


def _rw_pre_kernel(z_ref, hp_ref, hn_ref, mu_ref, w0_ref, w2_ref, a0_ref, a2_ref, g2_ref,
                   kk_ref, ka_ref, ones_ref,
                   r_ref, k_ref, v_ref, g_ref, na_ref, wf_ref, wb_ref, bf_ref, bb_ref,
                   kf_ref, kb_ref):
    z = z_ref[0]
    row = lax.broadcasted_iota(jnp.int32, (ROW_BLK, 1), 0)
    prev = jnp.where(row == 0, hp_ref[0, 0], pltpu.roll(z, 1, 0))
    nxt = jnp.where(row == ROW_BLK - 1, hn_ref[0, 0], pltpu.roll(z, ROW_BLK - 1, 0))
    zs = z + mu_ref[0:1, :] * (prev - z) + mu_ref[1:2, :] * (nxt - z)
    r = zs[:, 0:RW_W]
    k = zs[:, RW_W:2 * RW_W]
    v = zs[:, 2 * RW_W:3 * RW_W]
    o = 3 * RW_W
    lw = jnp.tanh(zs[:, o:o + 2 * LORA])
    la = zs[:, o + 2 * LORA:o + 4 * LORA]
    lg = jax.nn.sigmoid(zs[:, o + 4 * LORA:o + 4 * LORA + LORA_G])
    r_ref[0] = r
    k_ref[0] = k
    v_ref[0] = v
    g_ref[0] = _dot(lg, g2_ref[...], precision=HI)
    kkr = k * kk_ref[...]
    ss = _dot(kkr * kkr, ones_ref[...], precision=HI)
    kkn = kkr / jnp.maximum(jnp.sqrt(ss), 1e-12)
    na_ref[0] = -kkn
    for d, (w_ref, b_ref, kd_ref) in enumerate(((wf_ref, bf_ref, kf_ref), (wb_ref, bb_ref, kb_ref))):
        dec = w0_ref[d:d + 1, :] + _dot(lw, w2_ref[d], precision=HI)
        w_ref[0] = jnp.exp(-W_DECAY_SCALE * jax.nn.sigmoid(dec))
        a = jax.nn.sigmoid(a0_ref[d:d + 1, :] + _dot(la, a2_ref[d], precision=HI))
        kd_ref[0] = k * (1.0 + (a - 1.0) * ka_ref[...])
        b_ref[0] = kkn * a


def rw_pre(z, mu, w0, w2, a0, a2, g2, kk, ka):
    nb = z.shape[0]
    zr = z[:, :, :RW_COLS]
    zero = jnp.zeros((nb, 1, RW_COLS), F32)
    last = zr[:, ROW_BLK - 1::ROW_BLK]
    first = zr[:, ::ROW_BLK]
    halo_prev = jnp.concatenate([zero, zero, last[:, 1:N_ROW_BLK - 1]], axis=1)
    halo_next = jnp.concatenate([zero, first[:, 2:], zero], axis=1)
    halo_prev = halo_prev.reshape(nb, N_ROW_BLK, 1, RW_COLS)
    halo_next = halo_next.reshape(nb, N_ROW_BLK, 1, RW_COLS)
    zpad = jnp.zeros((LORA, RW_W), F32)
    w2p = jnp.stack([jnp.concatenate([w2[0], zpad]), jnp.concatenate([zpad, w2[1]])])
    a2p = jnp.stack([jnp.concatenate([a2[0], zpad]), jnp.concatenate([zpad, a2[1]])])
    row = lambda b, i: (b, i, 0)
    full = lambda shape: pl.BlockSpec(shape, lambda b, i: (0,) * len(shape))
    out = jax.ShapeDtypeStruct((nb, T_ALL, RW_W), F32)
    ospec = pl.BlockSpec((1, ROW_BLK, RW_W), row)
    return pl.pallas_call(
        _rw_pre_kernel,
        grid=(nb, N_ROW_BLK),
        in_specs=[pl.BlockSpec((1, ROW_BLK, RW_COLS), row),
                  pl.BlockSpec((1, 1, 1, RW_COLS), lambda b, i: (b, i, 0, 0)),
                  pl.BlockSpec((1, 1, 1, RW_COLS), lambda b, i: (b, i, 0, 0)),
                  full((2, RW_COLS)), full((2, RW_W)), full((2, 2 * LORA, RW_W)),
                  full((2, RW_W)), full((2, 2 * LORA, RW_W)), full((LORA_G, RW_W)),
                  full((1, RW_W)), full((1, RW_W)), full((RW_W, RW_W))],
        out_specs=[ospec] * 11,
        out_shape=[out] * 11,
        compiler_params=_cparams(("arbitrary", "arbitrary")),
        name="rw_pre",
    )(z, halo_prev, halo_next, mu, w0, w2p, a0, a2p, g2, kk.reshape(1, RW_W),
      ka.reshape(1, RW_W), _head_ones(RW_W, HEAD))


SCAN_TB = 32
SCAN_I = HEAD // 2


SCAN_NBLK = T_ALL // SCAN_TB
SCAN_CTX_BLK = T_CTX // SCAN_TB


def _rw_scan_kernel(a1, w1, b1, k1, r1, v1, a2, w2, b2, k2, r2, v2, yf_ref, yb_ref,
                    s_ref, sa_ref, vt_ref):
    @pl.when(pl.program_id(0) == 0)
    def _():
        s_ref[...] = jnp.zeros_like(s_ref)

    fwd = lax.broadcasted_iota(jnp.int32, (1, LANES), 1) < LANES // 2

    def step(t, carry):
        tb = SCAN_TB - 1 - t
        a = jnp.where(fwd, a1[t], a2[tb])
        for i in range(SCAN_I):
            sa_ref[pl.ds(i, 1), :] = jnp.sum(s_ref[i] * a, axis=0, keepdims=True)
        w = jnp.where(fwd, w1[t], w2[tb])
        b = jnp.where(fwd, b1[t], b2[tb])
        k = jnp.where(fwd, k1[t], k2[tb])
        r = jnp.where(fwd, r1[t], r2[tb])
        vt_ref[...] = jnp.where(fwd, v1[t], v2[tb])
        for i in range(SCAN_I):
            sn = s_ref[i] * w + sa_ref[pl.ds(i, 1), :] * b + vt_ref[pl.ds(i, 1), :] * k
            s_ref[i] = sn
            y = jnp.sum(sn * r, axis=0, keepdims=True)
            yf_ref[t, pl.ds(i, 1), :] = y
            yb_ref[tb, pl.ds(i, 1), :] = y
        return carry

    lax.fori_loop(0, SCAN_TB, step, 0)


def _mirror_block(g):
    return jnp.where(g < SCAN_CTX_BLK, SCAN_CTX_BLK - 1 - g, SCAN_NBLK + SCAN_CTX_BLK - 1 - g)


def rw_scan(a, w, b, k, r, v):
    t = a.shape[0]
    jf = pl.BlockSpec((SCAN_TB, HEAD, LANES), lambda g: (g, 0, 0))
    jb = pl.BlockSpec((SCAN_TB, HEAD, LANES), lambda g: (_mirror_block(g), 0, 0))
    vf = pl.BlockSpec((SCAN_TB, SCAN_I, LANES), lambda g: (g, 0, 0))
    vb = pl.BlockSpec((SCAN_TB, SCAN_I, LANES), lambda g: (_mirror_block(g), 0, 0))
    out = jax.ShapeDtypeStruct((t, SCAN_I, LANES), F32)
    return pl.pallas_call(
        _rw_scan_kernel,
        grid=(t // SCAN_TB,),
        in_specs=[jf] * 5 + [vf] + [jb] * 5 + [vb],
        out_specs=[vf, vb],
        out_shape=[out, out],
        scratch_shapes=[pltpu.VMEM((SCAN_I, HEAD, LANES), F32),
                        pltpu.VMEM((SCAN_I, LANES), F32),
                        pltpu.VMEM((SCAN_I, LANES), F32)],
        compiler_params=_cparams(("arbitrary",)),
        name="rw_scan",
    )(a, w, b, k, r, v, a, w, b, k, r, v)


def _scan_layout_j(fwd, bwd):
    nb = fwd.shape[0]
    s = jnp.stack([fwd, bwd], 0).reshape(2, nb, T_ALL, RW_H, HEAD)
    s = s.transpose(2, 4, 0, 1, 3)[:, :, :, None]
    return jnp.broadcast_to(s, (T_ALL, HEAD, 2, 2, nb, RW_H)).reshape(T_ALL, HEAD, 4 * nb * RW_H)


def _scan_layout_i(v):
    nb = v.shape[0]
    s = v.reshape(nb, T_ALL, RW_H, 2, SCAN_I).transpose(1, 4, 3, 0, 2)[:, :, None]
    return jnp.broadcast_to(s, (T_ALL, SCAN_I, 2, 2, nb, RW_H)).reshape(T_ALL, SCAN_I, 4 * nb * RW_H)


def _scan_unlayout_i(y, d, nb):
    half = LANES // 2
    s = y[:, :, d * half:(d + 1) * half].reshape(T_ALL, SCAN_I, 2, nb, RW_H)
    return s.transpose(3, 0, 4, 2, 1).reshape(nb, T_ALL, RW_W)


def _rw_post_kernel(yf_ref, yb_ref, r_ref, k_ref, v_ref, g_ref, rk_ref, lw_ref, lb_ref, ones_ref,
                    o_ref):
    y = yf_ref[0] + yb_ref[0]
    ones = ones_ref[...]
    mean = _dot(y, ones, precision=HI) * (1.0 / HEAD)
    yc = y - mean
    var = _dot(yc * yc, ones, precision=HI) * (1.0 / HEAD)
    yn = yc * lax.rsqrt(var + RW_LN_EPS) * lw_ref[...] + lb_ref[...]
    bonus = _dot(r_ref[0] * k_ref[0] * rk_ref[...], ones, precision=HI) * v_ref[0]
    o_ref[0] = ((yn + bonus) * g_ref[0]).astype(BF16)


def rw_post(yf, yb, r, k, v, g, rk, ln_w, ln_b):
    nb = yf.shape[0]
    row = lambda b, i: (b, i, 0)
    spec = pl.BlockSpec((1, ROW_BLK, RW_W), row)
    vec = pl.BlockSpec((1, RW_W), lambda b, i: (0, 0))
    return pl.pallas_call(
        _rw_post_kernel,
        grid=(nb, N_ROW_BLK),
        in_specs=[spec] * 6 + [vec] * 3 + [pl.BlockSpec((RW_W, RW_W), lambda b, i: (0, 0))],
        out_specs=spec,
        out_shape=jax.ShapeDtypeStruct((nb, T_ALL, RW_W), BF16),
        compiler_params=_cparams(("arbitrary", "arbitrary")),
        name="rw_post",
    )(yf, yb, r, k, v, g, rk.reshape(1, RW_W), ln_w.reshape(1, RW_W), ln_b.reshape(1, RW_W),
      _head_ones(RW_W, HEAD))


def rwkv_mixer(z, mu, w0, w2, a0, a2, g2, kk, ka, rk, ln_w, ln_b):
    nb = z.shape[0]
    r, k, v, g, na, wf, wb, bf, bb, kf, kb = rw_pre(z, mu, w0, w2, a0, a2, g2, kk, ka)
    y1, y2 = rw_scan(_scan_layout_j(na, na), _scan_layout_j(wf, wb), _scan_layout_j(bf, bb),
                     _scan_layout_j(kf, kb), _scan_layout_j(r, r), _scan_layout_i(v))
    yf, yb = _scan_unlayout_i(y1, 0, nb), _scan_unlayout_i(y2, 1, nb)
    return rw_post(yf, yb, r, k, v, g, rk, ln_w, ln_b)


NA_QROWS = ROW_BLK // GRID_W
NA_SLAB = NA_KH + NA_QROWS - 1
NA_SLAB_T = NA_SLAB * GRID_W
NA_SCALE = HEAD ** -0.5


def _na_bias_index():
    a = np.arange(NA_QROWS)[:, None]
    u = np.arange(NA_SLAB)[None, :]
    idx_r, valid = [], []
    for r0, u0 in ((0, 0), (NA_QROWS, 0), (GRID_ROWS - NA_QROWS, GRID_ROWS - NA_SLAB)):
        r = r0 + a
        kr = u0 + u
        kr0 = np.clip(r - NA_KH // 2, 0, GRID_ROWS - NA_KH)
        valid.append((kr >= kr0) & (kr < kr0 + NA_KH))
        idx_r.append(np.clip(kr - r + NA_KH - 1, 0, 2 * NA_KH - 2))
    idx_r = np.stack(idx_r)
    valid = np.stack(valid)[:, :, None, :, None]
    qc = np.arange(GRID_W)[:, None]
    kc = np.arange(GRID_W)[None, :]
    c0 = np.clip(qc - NA_KW // 2, 0, GRID_W - NA_KW)
    in_win = ((kc >= c0) & (kc < c0 + NA_KW))[None, None, :, None, :]
    idx_c = np.clip(kc - qc + NA_KW - 1, 0, 2 * NA_KW - 2)
    col_onehot = (idx_c[None] == np.arange(2 * NA_KW - 1)[:, None, None]).astype(np.float32)
    shape = (3, NA_QROWS, GRID_W, NA_SLAB, GRID_W)
    return idx_r, col_onehot, np.broadcast_to(valid & in_win, shape)


def na_bias_tables(rpb):
    idx_r, col_onehot, mask = _na_bias_index()
    rows = rpb[:, idx_r.reshape(-1), :].reshape(NA_H, 3, NA_QROWS, NA_SLAB, 2 * NA_KW - 1)
    b = jnp.einsum('hvauc,cqk->hvaquk', rows, jnp.asarray(col_onehot), precision=HI)
    b = jnp.where(mask[None], b, NEG_INF)
    return b.reshape(NA_H, 3, ROW_BLK, NA_SLAB_T)


def _na_kernel(q_ref, k_ref, v_ref, bias_ref, o_ref):
    qi = pl.program_id(2)

    @pl.when(qi == 0)
    def _():
        ys = []
        for hh in range(2):
            ln = slice(hh * HEAD, (hh + 1) * HEAD)
            q = (q_ref[0, :, ln] * NA_SCALE).astype(BF16)
            s = _dot_nt(q, k_ref[0, 0:T_CTX, ln].astype(BF16))
            p = jnp.exp(s - jnp.max(s, axis=-1, keepdims=True))
            y = _dot(p.astype(BF16), v_ref[0, 0:T_CTX, ln].astype(BF16))
            ys.append(y / jnp.sum(p, axis=-1, keepdims=True))
        o_ref[0] = jnp.concatenate(ys, axis=1).astype(BF16)

    @pl.when(qi > 0)
    def _():
        u0 = jnp.clip(NA_QROWS * (qi - 1) - NA_KH // 2, 0, GRID_ROWS - NA_SLAB)
        start = pl.multiple_of(T_CTX + GRID_W * u0, GRID_W)
        ys = []
        for hh in range(2):
            ln = slice(hh * HEAD, (hh + 1) * HEAD)
            q = (q_ref[0, :, ln] * NA_SCALE).astype(BF16)
            sc = _dot_nt(q, k_ref[0, 0:T_CTX, ln].astype(BF16))
            sw = _dot_nt(q, k_ref[0, pl.ds(start, NA_SLAB_T), ln].astype(BF16)) + bias_ref[hh, 0]
            m = jnp.maximum(jnp.max(sc, axis=-1, keepdims=True), jnp.max(sw, axis=-1, keepdims=True))
            pc = jnp.exp(sc - m)
            pw = jnp.exp(sw - m)
            y = (_dot(pw.astype(BF16), v_ref[0, pl.ds(start, NA_SLAB_T), ln].astype(BF16))
                 + _dot(pc.astype(BF16), v_ref[0, 0:T_CTX, ln].astype(BF16)))
            den = jnp.sum(pc, axis=-1, keepdims=True) + jnp.sum(pw, axis=-1, keepdims=True)
            ys.append(y / den)
        o_ref[0] = jnp.concatenate(ys, axis=1).astype(BF16)


def natten_mixer(z, bias):
    nb = z.shape[0]
    qb, kb, vb = (NA_OFF // LANES, (NA_OFF + NA_W) // LANES, (NA_OFF + 2 * NA_W) // LANES)
    n_blk = N_ROW_BLK - 1

    def bias_idx(b, hp, qi):
        var = jnp.where(qi <= 1, 0, jnp.where(qi == n_blk, 2, 1))
        return (hp, var, 0, 0)

    return pl.pallas_call(
        _na_kernel,
        grid=(nb, NA_H // 2, N_ROW_BLK),
        in_specs=[pl.BlockSpec((1, ROW_BLK, LANES), lambda b, hp, qi: (b, qi, qb + hp)),
                  pl.BlockSpec((1, T_ALL, LANES), lambda b, hp, qi: (b, 0, kb + hp)),
                  pl.BlockSpec((1, T_ALL, LANES), lambda b, hp, qi: (b, 0, vb + hp)),
                  pl.BlockSpec((2, 1, ROW_BLK, NA_SLAB_T), bias_idx)],
        out_specs=pl.BlockSpec((1, ROW_BLK, LANES), lambda b, hp, qi: (b, qi, hp)),
        out_shape=jax.ShapeDtypeStruct((nb, T_ALL, NA_W), BF16),
        compiler_params=_cparams(("arbitrary", "arbitrary", "arbitrary")),
        name="natten",
    )(z, z, z, bias)


ROPE_NF = MB_N // 4


def rope_tables():
    pos = np.arange(T_LAT)
    inv = ROPE_BASE ** (-np.arange(ROPE_NF, dtype=np.float32) / ROPE_NF)
    lane = np.arange(MB_N)
    p = np.where(lane[None, :] < MB_N // 2, (pos // GRID_W)[:, None], (pos % GRID_W)[:, None])
    ang = p.astype(np.float32) * inv[lane % ROPE_NF][None, :]
    sign = np.where((lane % (2 * ROPE_NF)) < ROPE_NF, -1.0, 1.0)[None, :]
    cos = np.concatenate([np.ones((T_CTX, MB_N), np.float32), np.cos(ang)])
    sin = np.concatenate([np.zeros((T_CTX, MB_N), np.float32), np.sin(ang) * sign])
    return jnp.asarray(cos, F32), jnp.asarray(sin, F32)


def _mb_pre_kernel(z_ref, w_ref, b_ref, cos_ref, sin_ref, o_ref):
    z = z_ref[0]
    pos = lax.broadcasted_iota(jnp.int32, (T_ALL, 1), 0)
    lo = jnp.where(pos < T_CTX, 0, T_CTX)
    hi = jnp.where(pos < T_CTX, T_CTX, T_ALL)
    half = MB_CONV // 2
    acc = z * w_ref[half:half + 1, :] + b_ref[...]
    for d in (-2, -1, 1, 2):
        zr = pltpu.roll(z, (-d) % T_ALL, 0)
        ok = (pos + d >= lo) & (pos + d < hi)
        acc = acc + jnp.where(ok, zr, 0.0) * w_ref[half + d:half + d + 1, :]
    y = acc * jax.nn.sigmoid(acc)
    is_bc = pl.program_id(1) >= MB_INNER // LANES

    @pl.when(jnp.logical_not(is_bc))
    def _():
        o_ref[0] = y

    @pl.when(is_bc)
    def _():
        lane = lax.broadcasted_iota(jnp.int32, (1, MB_N), 1)
        first = (lane % (2 * ROPE_NF)) < ROPE_NF
        partner = jnp.where(first, pltpu.roll(y, MB_N - ROPE_NF, 1), pltpu.roll(y, ROPE_NF, 1))
        o_ref[0] = y * cos_ref[...] + partner * sin_ref[...]


def mb_pre(z, conv_w, conv_b, cos, sin):
    nb = z.shape[0]
    c0 = XBC_OFF // LANES
    return pl.pallas_call(
        _mb_pre_kernel,
        grid=(nb, MB_CONV_CH // LANES),
        in_specs=[pl.BlockSpec((1, T_ALL, LANES), lambda b, j: (b, 0, c0 + j)),
                  pl.BlockSpec((MB_CONV, LANES), lambda b, j: (0, j)),
                  pl.BlockSpec((1, LANES), lambda b, j: (0, j)),
                  pl.BlockSpec((T_ALL, MB_N), lambda b, j: (0, 0)),
                  pl.BlockSpec((T_ALL, MB_N), lambda b, j: (0, 0))],
        out_specs=pl.BlockSpec((1, T_ALL, LANES), lambda b, j: (b, 0, j)),
        out_shape=jax.ShapeDtypeStruct((nb, T_ALL, MB_CONV_CH), F32),
        compiler_params=_cparams(("arbitrary", "arbitrary")),
        name="mb_pre",
    )(z, conv_w, conv_b.reshape(1, MB_CONV_CH), cos, sin)


N_CHUNK = T_ALL // MB_CHUNK
N_CTX_CHUNK = T_CTX // MB_CHUNK
XB_BLK = MB_INNER // LANES
XC_BLK = XB_BLK + MB_G


def _ssd_direction(reverse, x_ref, b_ref, c_ref, dtc_ref, dtr_ref, bc_ref, br_ref, ac_ref, ar_ref,
                   st_ref, y_ref):
    L = MB_CHUNK
    dtc = jax.nn.softplus(dtc_ref[0, 0, 0] + bc_ref[0, 0])
    dtr = jax.nn.softplus(dtr_ref[0, 0, 0] + br_ref[0, 0])
    dac = dtc * (-jnp.exp(ac_ref[0, 0]))
    dar = dtr * (-jnp.exp(ar_ref[0, 0]))
    ri = lax.broadcasted_iota(jnp.int32, (L, L), 0)
    ci = lax.broadcasted_iota(jnp.int32, (L, L), 1)
    lower = (ri >= ci).astype(F32)
    upper = (ri <= ci).astype(F32)
    if reverse:
        cum_c = _dot(upper, dac, precision=HI)
        cum_r = _dot(dar, lower, precision=HI)
        mask = ri <= ci
        tot_row = 0
    else:
        cum_c = _dot(lower, dac, precision=HI)
        cum_r = _dot(dar, upper, precision=HI)
        mask = ri >= ci
        tot_row = L - 1
    bm = b_ref[0]
    cm = c_ref[0].astype(BF16)
    cb = _dot_nt(cm, bm.astype(BF16))
    bt = bm.T.astype(BF16)
    x = x_ref[0]
    ys = []
    for r in range(MB_R):
        cc = cum_c[:, r:r + 1]
        cr = cum_r[r:r + 1, :]
        tot = cc[tot_row:tot_row + 1, :]
        lmat = jnp.exp(jnp.where(mask, cc - cr, -jnp.inf))
        xc = x[:, r * HEAD:(r + 1) * HEAD] * dtc[:, r:r + 1]
        y_diag = _dot((cb * lmat).astype(BF16), xc.astype(BF16))
        st = st_ref[r]
        y_off = _dot(cm, st.astype(BF16)) * jnp.exp(cc)
        xdec = (xc * jnp.exp(tot - cc)).astype(BF16)
        st_ref[r] = jnp.exp(tot) * st + _dot(bt, xdec)
        ys.append(y_diag + y_off)
    y_ref[0] = jnp.concatenate(ys, axis=1)


def _mb_ssd_kernel(*refs):
    fwd, bwd = refs[0:9], refs[9:18]
    yf_ref, yb_ref, stf_ref, stb_ref = refs[18:22]

    @pl.when(pl.program_id(2) == 0)
    def _():
        stf_ref[...] = jnp.zeros_like(stf_ref)
        stb_ref[...] = jnp.zeros_like(stb_ref)

    _ssd_direction(False, *fwd, stf_ref, yf_ref)
    _ssd_direction(True, *bwd, stb_ref, yb_ref)


def _bwd_chunk(i):
    return jnp.where(i < N_CTX_CHUNK, N_CTX_CHUNK - 1 - i, N_CHUNK + N_CTX_CHUNK - 1 - i)


def mb_ssd(xbc, dt_raw, dt_bias, a_log):
    nb = xbc.shape[0]
    dt5 = dt_raw.reshape(nb, T_ALL, 2, MB_G, MB_R)
    dtc = dt5.transpose(0, 2, 3, 1, 4)
    dtr = dt5.transpose(0, 2, 3, 4, 1)
    b4 = dt_bias.reshape(2, MB_G, MB_R)
    a4 = a_log.reshape(2, MB_G, MB_R)
    ins, specs = [], []
    for d, cidx in ((0, lambda i: i), (1, _bwd_chunk)):
        ins += [xbc, xbc, xbc, dtc, dtr, b4[:, :, None, :], b4[:, :, :, None],
                a4[:, :, None, :], a4[:, :, :, None]]
        specs += [
            pl.BlockSpec((1, MB_CHUNK, MB_R * HEAD), lambda b, g, i, c=cidx: (b, c(i), g)),
            pl.BlockSpec((1, MB_CHUNK, MB_N), lambda b, g, i, c=cidx: (b, c(i), XB_BLK + g)),
            pl.BlockSpec((1, MB_CHUNK, MB_N), lambda b, g, i, c=cidx: (b, c(i), XC_BLK + g)),
            pl.BlockSpec((1, 1, 1, MB_CHUNK, MB_R), lambda b, g, i, c=cidx, d=d: (b, d, g, c(i), 0)),
            pl.BlockSpec((1, 1, 1, MB_R, MB_CHUNK), lambda b, g, i, c=cidx, d=d: (b, d, g, 0, c(i))),
            pl.BlockSpec((1, 1, 1, MB_R), lambda b, g, i, d=d: (d, g, 0, 0)),
            pl.BlockSpec((1, 1, MB_R, 1), lambda b, g, i, d=d: (d, g, 0, 0)),
            pl.BlockSpec((1, 1, 1, MB_R), lambda b, g, i, d=d: (d, g, 0, 0)),
            pl.BlockSpec((1, 1, MB_R, 1), lambda b, g, i, d=d: (d, g, 0, 0)),
        ]
    out = jax.ShapeDtypeStruct((nb, T_ALL, MB_INNER), F32)
    return pl.pallas_call(
        _mb_ssd_kernel,
        grid=(nb, MB_G, N_CHUNK),
        in_specs=specs,
        out_specs=[pl.BlockSpec((1, MB_CHUNK, MB_R * HEAD), lambda b, g, i: (b, i, g)),
                   pl.BlockSpec((1, MB_CHUNK, MB_R * HEAD), lambda b, g, i: (b, _bwd_chunk(i), g))],
        out_shape=[out, out],
        scratch_shapes=[pltpu.VMEM((MB_R, MB_N, HEAD), F32), pltpu.VMEM((MB_R, MB_N, HEAD), F32)],
        compiler_params=_cparams(("arbitrary", "arbitrary", "arbitrary")),
        name="mb_ssd",
    )(*ins)


MB_GW = MB_INNER // MB_G


def _mb_post_kernel(yf_ref, yb_ref, x_ref, glo_ref, ghi_ref, d_ref, nw_ref, o_ref):
    y = yf_ref[0] + yb_ref[0] + x_ref[0] * d_ref[...]
    gate = jnp.concatenate([glo_ref[0], ghi_ref[0]], axis=1)
    y = y * (gate * jax.nn.sigmoid(gate))
    ms = jnp.mean(y * y, axis=-1, keepdims=True)
    o_ref[0] = (y * lax.rsqrt(ms + NORM_EPS) * nw_ref[...]).astype(BF16)


def mb_post(yf, yb, xbc, z, d_skip, norm_w):
    nb = yf.shape[0]
    g0 = MB_OFF // LANES
    grp = lambda b, i, g: (b, i, g)
    return pl.pallas_call(
        _mb_post_kernel,
        grid=(nb, N_ROW_BLK, MB_G),
        in_specs=[pl.BlockSpec((1, ROW_BLK, MB_GW), grp),
                  pl.BlockSpec((1, ROW_BLK, MB_GW), grp),
                  pl.BlockSpec((1, ROW_BLK, MB_GW), grp),
                  pl.BlockSpec((1, ROW_BLK, LANES), lambda b, i, g: (b, i, g0 + 2 * g)),
                  pl.BlockSpec((1, ROW_BLK, LANES), lambda b, i, g: (b, i, g0 + 2 * g + 1)),
                  pl.BlockSpec((1, MB_GW), lambda b, i, g: (0, g)),
                  pl.BlockSpec((1, MB_GW), lambda b, i, g: (0, g))],
        out_specs=pl.BlockSpec((1, ROW_BLK, MB_GW), grp),
        out_shape=jax.ShapeDtypeStruct((nb, T_ALL, MB_INNER), BF16),
        compiler_params=_cparams(("arbitrary", "arbitrary", "arbitrary")),
        name="mb_post",
    )(yf, yb, xbc, z, z, jnp.repeat(d_skip, HEAD).reshape(1, MB_INNER), norm_w.reshape(1, MB_INNER))


def mamba_mixer(z, conv_w, conv_b, dt_bias, a_log, d_skip, norm_w, cos, sin):
    xbc = mb_pre(z, conv_w, conv_b, cos, sin)
    yf, yb = mb_ssd(xbc, z[:, :, DT_OFF:DT_OFF + 2 * MB_H], dt_bias, a_log)
    return mb_post(yf, yb, xbc, z, d_skip, norm_w)


def kernel(x, c, ctx, c_ctx, ada_w, ada_b, norm1_w, norm2_w, w_in, w_out, rw_mu, rw_w0, rw_w2, rw_a0, rw_a2, rw_g2, rw_kk, rw_ka, rw_rk, rw_ln_w, rw_ln_b, na_rpb, mb_conv_w, mb_conv_b, mb_dt_bias, mb_a_log, mb_d, mb_norm_w, pe_wq, pe_keys, pe_u, pe_v, final_norm_w):
    nb = x.shape[0]
    xs = jnp.concatenate([ctx, x], axis=1)
    cond = jnp.zeros((SUBLANES, D), F32).at[:nb].set(c).at[nb].set(c_ctx)
    mods = ada_rows(cond, ada_w, ada_b)
    cos, sin = rope_tables()
    f = prev_tab = None
    for l in range(DEPTH):
        m = mods[l].reshape(SUBLANES, 6, D)
        tab = jnp.stack([jnp.broadcast_to(m[nb], (nb, 6, D)), m[:nb]], axis=1).reshape(2 * nb, 6, D)
        if f is None:
            h1 = norm_mod(xs, norm1_w[l].reshape(1, D), tab, 0)
        else:
            xs, h1 = resid_norm_mod(xs, f, prev_tab, norm1_w[l].reshape(1, D), tab)
        z = matmul(h1.reshape(nb * T_ALL, D), cast_bf16(w_in[l], ROW_BLK)).reshape(nb, T_ALL, IN_COLS)
        y_rw = rwkv_mixer(z, rw_mu[l], rw_w0[l], rw_w2[l], rw_a0[l], rw_a2[l], rw_g2[l], rw_kk[l],
                          rw_ka[l], rw_rk[l].reshape(RW_W), rw_ln_w[l], rw_ln_b[l])
        y_na = natten_mixer(z, na_bias_tables(na_rpb[l]))
        y_mb = mamba_mixer(z, mb_conv_w[l], mb_conv_b[l], mb_dt_bias[l], mb_a_log[l], mb_d[l],
                           mb_norm_w[l], cos, sin)
        xs = out_proj(xs, y_rw, y_na, y_mb, cast_bf16(w_out[l], ROW_BLK), tab)
        h2 = norm_mod(xs, norm2_w[l].reshape(1, D), tab, 3)
        f = peer_ffn(h2, cast_bf16(pe_wq[l], ROW_BLK), pe_keys[l],
                     cast_bf16(pe_u[l], 1024), cast_bf16(pe_v[l], 1024))
        prev_tab = tab
    return final_norm(xs, f, prev_tab, final_norm_w.reshape(1, D))
```

```python
import functools
import math

import numpy as np
import jax
import jax.numpy as jnp
from jax import lax
from jax.experimental import pallas as pl
from jax.experimental.pallas import tpu as pltpu

D = 2048
NB = 4
T_LAT = 2048
T_CTX = 256
T_ALL = T_CTX + T_LAT
DEPTH = 4
GRID_W = 64
GRID_ROWS = T_LAT // GRID_W
HEAD = 64
RW_W = 512
RW_H = 8
LORA = 64
LORA_G = 128
W_DECAY_SCALE = 0.606531
RW_LN_EPS = 64e-5
NA_W = 512
NA_H = 8
NA_KH = 8
NA_KW = 16
MB_INNER = 1024
MB_H = 16
MB_G = 4
MB_R = 4
MB_N = 128
MB_CONV = 5
MB_CHUNK = 128
ROPE_BASE = 10000.0
PEER_HEADS = 8
N_KEYS = 128
PEER_TOPK = 16
NORM_EPS = 1e-6
NEG_INF = -1e30
RW_COLS = 3 * RW_W + 4 * LORA + LORA_G
NA_COLS = 3 * NA_W
MB_CONV_CH = MB_INNER + 2 * MB_G * MB_N
MB_COLS = MB_INNER + MB_CONV_CH + 2 * MB_H
IN_COLS = RW_COLS + NA_COLS + MB_COLS
NA_OFF = RW_COLS
MB_OFF = RW_COLS + NA_COLS
XBC_OFF = MB_OFF + MB_INNER
DT_OFF = XBC_OFF + MB_CONV_CH

LANES = 128
SUBLANES = 8
VMEM_LIMIT = 56 * 1024 * 1024

ROW_BLK = 256
N_ROW_BLK = T_ALL // ROW_BLK
BF16 = jnp.bfloat16
F32 = jnp.float32
HI = lax.Precision.HIGHEST


def _cparams(sem):
    return pltpu.CompilerParams(dimension_semantics=sem, vmem_limit_bytes=VMEM_LIMIT)


def _dot(a, b, precision=None):
    return jnp.dot(a, b, preferred_element_type=F32, precision=precision)


def _dot_nt(a, b, precision=None):
    return lax.dot_general(a, b, (((1,), (1,)), ((), ())), preferred_element_type=F32,
                           precision=precision)


def _cast_kernel(x_ref, o_ref):
    o_ref[...] = x_ref[...].astype(BF16)


def cast_bf16(w, rows_blk):
    r, c = w.shape
    return pl.pallas_call(
        _cast_kernel,
        grid=(r // rows_blk,),
        in_specs=[pl.BlockSpec((rows_blk, c), lambda i: (i, 0))],
        out_specs=pl.BlockSpec((rows_blk, c), lambda i: (i, 0)),
        out_shape=jax.ShapeDtypeStruct((r, c), BF16),
        compiler_params=_cparams(("arbitrary",)),
        name="cast_bf16",
    )(w)


ADA_TN = 1024


def _ada_kernel(c_ref, w_ref, b_ref, o_ref):
    c = c_ref[...]
    s = c * jax.nn.sigmoid(c)
    o_ref[0] = _dot(s, w_ref[0], precision=HI) + b_ref[0]


def ada_rows(cond, ada_w, ada_b):
    nl = ada_w.shape[0]
    return pl.pallas_call(
        _ada_kernel,
        grid=(nl, 6 * D // ADA_TN),
        in_specs=[pl.BlockSpec((SUBLANES, D), lambda l, j: (0, 0)),
                  pl.BlockSpec((1, D, ADA_TN), lambda l, j: (l, 0, j)),
                  pl.BlockSpec((1, 1, ADA_TN), lambda l, j: (l, 0, j))],
        out_specs=pl.BlockSpec((1, SUBLANES, ADA_TN), lambda l, j: (l, 0, j)),
        out_shape=jax.ShapeDtypeStruct((nl, SUBLANES, 6 * D), F32),
        compiler_params=_cparams(("arbitrary", "arbitrary")),
        name="ada_rows",
    )(cond, ada_w, ada_b.reshape(nl, 1, 6 * D))


def _mod_index(b, i):
    return 2 * b + jnp.minimum(i, 1)


def _norm_mod_kernel(which, x_ref, nw_ref, tab_ref, o_ref):
    x = x_ref[0]
    ms = jnp.mean(x * x, axis=-1, keepdims=True)
    y = x * lax.rsqrt(ms + NORM_EPS) * nw_ref[...]
    shift = tab_ref[0, which:which + 1, :]
    scale = tab_ref[0, which + 1:which + 2, :]
    o_ref[0] = (y * (1.0 + scale) + shift).astype(BF16)


def norm_mod(x, nw, tab, which):
    nb = x.shape[0]
    return pl.pallas_call(
        functools.partial(_norm_mod_kernel, which),
        grid=(nb, N_ROW_BLK),
        in_specs=[pl.BlockSpec((1, ROW_BLK, D), lambda b, i: (b, i, 0)),
                  pl.BlockSpec((1, D), lambda b, i: (0, 0)),
                  pl.BlockSpec((1, 6, D), lambda b, i: (_mod_index(b, i), 0, 0))],
        out_specs=pl.BlockSpec((1, ROW_BLK, D), lambda b, i: (b, i, 0)),
        out_shape=jax.ShapeDtypeStruct(x.shape, BF16),
        compiler_params=_cparams(("arbitrary", "arbitrary")),
        name="norm_mod",
    )(x, nw, tab)


def _resid_norm_mod_kernel(x_ref, f_ref, ptab_ref, nw_ref, tab_ref, xo_ref, o_ref):
    x = x_ref[0] + ptab_ref[0, 5:6, :] * f_ref[0]
    xo_ref[0] = x
    ms = jnp.mean(x * x, axis=-1, keepdims=True)
    y = x * lax.rsqrt(ms + NORM_EPS) * nw_ref[...]
    o_ref[0] = (y * (1.0 + tab_ref[0, 1:2, :]) + tab_ref[0, 0:1, :]).astype(BF16)


def resid_norm_mod(x, f, prev_tab, nw, tab):
    nb = x.shape[0]
    row = lambda b, i: (b, i, 0)
    mod = lambda b, i: (_mod_index(b, i), 0, 0)
    return pl.pallas_call(
        _resid_norm_mod_kernel,
        grid=(nb, N_ROW_BLK),
        in_specs=[pl.BlockSpec((1, ROW_BLK, D), row),
                  pl.BlockSpec((1, ROW_BLK, D), row),
                  pl.BlockSpec((1, 6, D), mod),
                  pl.BlockSpec((1, D), lambda b, i: (0, 0)),
                  pl.BlockSpec((1, 6, D), mod)],
        out_specs=[pl.BlockSpec((1, ROW_BLK, D), row), pl.BlockSpec((1, ROW_BLK, D), row)],
        out_shape=[jax.ShapeDtypeStruct(x.shape, F32), jax.ShapeDtypeStruct(x.shape, BF16)],
        input_output_aliases={0: 0},
        compiler_params=_cparams(("arbitrary", "arbitrary")),
        name="resid_norm_mod",
    )(x, f, prev_tab, nw, tab)


def _final_norm_kernel(x_ref, f_ref, ptab_ref, nw_ref, o_ref):
    x = x_ref[0] + ptab_ref[0, 5:6, :] * f_ref[0]
    ms = jnp.mean(x * x, axis=-1, keepdims=True)
    o_ref[0] = x * lax.rsqrt(ms + NORM_EPS) * nw_ref[...]


def final_norm(x, f, prev_tab, nw):
    nb = x.shape[0]
    lat = lambda b, i: (b, i + 1, 0)
    return pl.pallas_call(
        _final_norm_kernel,
        grid=(nb, T_LAT // ROW_BLK),
        in_specs=[pl.BlockSpec((1, ROW_BLK, D), lat),
                  pl.BlockSpec((1, ROW_BLK, D), lat),
                  pl.BlockSpec((1, 6, D), lambda b, i: (2 * b + 1, 0, 0)),
                  pl.BlockSpec((1, D), lambda b, i: (0, 0))],
        out_specs=pl.BlockSpec((1, ROW_BLK, D), lambda b, i: (b, i, 0)),
        out_shape=jax.ShapeDtypeStruct((nb, T_LAT, D), F32),
        compiler_params=_cparams(("arbitrary", "arbitrary")),
        name="final_norm",
    )(x, f, prev_tab, nw)


MM_TM = 1024
MM_TN = 512


def _mm_kernel(a_ref, w_ref, o_ref):
    o_ref[...] = _dot(a_ref[...], w_ref[...])


def matmul(a, w):
    m, k = a.shape
    n = w.shape[1]
    tm = MM_TM if m % MM_TM == 0 else ROW_BLK
    return pl.pallas_call(
        _mm_kernel,
        grid=(m // tm, pl.cdiv(n, MM_TN)),
        in_specs=[pl.BlockSpec((tm, k), lambda i, j: (i, 0)),
                  pl.BlockSpec((k, MM_TN), lambda i, j: (0, j))],
        out_specs=pl.BlockSpec((tm, MM_TN), lambda i, j: (i, j)),
        out_shape=jax.ShapeDtypeStruct((m, n), F32),
        compiler_params=_cparams(("arbitrary", "arbitrary")),
        name="matmul",
    )(a, w)


def _out_proj_kernel(x_ref, rw_ref, na_ref, mb_ref, w_ref, tab_ref, o_ref):
    acc = _dot(rw_ref[0], w_ref[0:RW_W, :])
    acc += _dot(na_ref[0], w_ref[RW_W:RW_W + NA_W, :])
    acc += _dot(mb_ref[0], w_ref[RW_W + NA_W:, :])
    o_ref[0] = x_ref[0] + tab_ref[0, 2:3, :] * acc


def out_proj(x, y_rw, y_na, y_mb, w, tab):
    nb = x.shape[0]
    row = lambda b, i: (b, i, 0)
    return pl.pallas_call(
        _out_proj_kernel,
        grid=(nb, N_ROW_BLK),
        in_specs=[pl.BlockSpec((1, ROW_BLK, D), row),
                  pl.BlockSpec((1, ROW_BLK, RW_W), row),
                  pl.BlockSpec((1, ROW_BLK, NA_W), row),
                  pl.BlockSpec((1, ROW_BLK, MB_INNER), row),
                  pl.BlockSpec((D, D), lambda b, i: (0, 0)),
                  pl.BlockSpec((1, 6, D), lambda b, i: (_mod_index(b, i), 0, 0))],
        out_specs=pl.BlockSpec((1, ROW_BLK, D), row),
        out_shape=jax.ShapeDtypeStruct(x.shape, F32),
        input_output_aliases={0: 0},
        compiler_params=_cparams(("arbitrary", "arbitrary")),
        name="out_proj",
    )(x, y_rw, y_na, y_mb, w, tab)


TOPK_TT = 128


CAND_COUNTS = (16, 8, 5, 4, 3, 2, 2, 2)
CAND_ROWS = 16 + 8 * 7 + 8
N_HALVES = 2 * PEER_HEADS
N_SLOTS = PEER_HEADS * PEER_TOPK


def _first_max(s, rows, n):
    m = jnp.max(s, axis=0, keepdims=True)
    pos = jnp.min(jnp.where(s == m, rows, float(n)), axis=0, keepdims=True)
    return m, pos


def _peer_topk_kernel(q_ref, keys_ref, i1_ref, i2_ref, g_ref,
                      sc_ref, val_ref, idx_ref, cand_ref, c1_ref, c2_ref, best_ref, e1_ref, e2_ref):
    t = TOPK_TT
    for hp in range(N_HALVES):
        c0 = hp * N_KEYS
        qs = q_ref[:, c0:c0 + N_KEYS].astype(BF16)
        sc_ref[hp] = _dot_nt(keys_ref[hp // 2, hp % 2].astype(BF16), qs)

    rows = lax.broadcasted_iota(jnp.int32, (N_KEYS, t), 0).astype(F32)

    def stage1(r, carry):
        for hp in range(N_HALVES):
            s = sc_ref[hp]
            m, pos = _first_max(s, rows, N_KEYS)
            val_ref[hp, pl.ds(r, 1), :] = m
            idx_ref[hp, pl.ds(r, 1), :] = pos
            sc_ref[hp] = jnp.where(rows == pos, -jnp.inf, s)
        return carry

    lax.fori_loop(0, PEER_TOPK, stage1, 0)

    row8 = lax.broadcasted_iota(jnp.int32, (SUBLANES, t), 0)
    for h in range(PEER_HEADS):
        v1, v2 = val_ref[2 * h], val_ref[2 * h + 1]
        k1, k2 = idx_ref[2 * h], idx_ref[2 * h + 1]
        cand = [v1[0:1] + v2]
        c1 = [jnp.broadcast_to(k1[0:1], (PEER_TOPK, t))]
        c2 = [k2]
        for r1 in range(1, SUBLANES):
            cand.append(jnp.where(row8 < CAND_COUNTS[r1], v1[r1:r1 + 1] + v2[0:SUBLANES], -jnp.inf))
            c1.append(jnp.broadcast_to(k1[r1:r1 + 1], (SUBLANES, t)))
            c2.append(k2[0:SUBLANES])
        cand.append(v1[SUBLANES:] + v2[0:1])
        c1.append(k1[SUBLANES:])
        c2.append(jnp.broadcast_to(k2[0:1], (SUBLANES, t)))
        cand_ref[h] = jnp.concatenate(cand, axis=0)
        c1_ref[h] = jnp.concatenate(c1, axis=0)
        c2_ref[h] = jnp.concatenate(c2, axis=0)

    crow = lax.broadcasted_iota(jnp.int32, (CAND_ROWS, t), 0).astype(F32)

    def stage2(r, carry):
        for h in range(PEER_HEADS):
            cd = cand_ref[h]
            m, pos = _first_max(cd, crow, CAND_ROWS)
            sel = crow == pos
            slot = pl.ds(h * PEER_TOPK + r, 1)
            best_ref[slot, :] = m
            e1_ref[slot, :] = jnp.sum(jnp.where(sel, c1_ref[h], 0.0), axis=0, keepdims=True)
            e2_ref[slot, :] = jnp.sum(jnp.where(sel, c2_ref[h], 0.0), axis=0, keepdims=True)
            cand_ref[h] = jnp.where(sel, -jnp.inf, cd)
        return carry

    lax.fori_loop(0, PEER_TOPK, stage2, 0)

    for h in range(PEER_HEADS):
        sl = slice(h * PEER_TOPK, (h + 1) * PEER_TOPK)
        b = best_ref[sl, :]
        ex = jnp.exp(b - jnp.max(b, axis=0, keepdims=True))
        best_ref[sl, :] = ex / jnp.sum(ex, axis=0, keepdims=True)
    i1_ref[...] = e1_ref[...].T
    i2_ref[...] = e2_ref[...].T
    g_ref[...] = best_ref[...].T


def peer_topk(q, keys):
    n = q.shape[0]
    t = TOPK_TT
    out = jax.ShapeDtypeStruct((n, N_SLOTS), F32)
    spec = pl.BlockSpec((t, N_SLOTS), lambda i: (i, 0))
    return pl.pallas_call(
        _peer_topk_kernel,
        grid=(n // t,),
        in_specs=[pl.BlockSpec((t, D), lambda i: (i, 0)),
                  pl.BlockSpec(keys.shape, lambda i: (0, 0, 0, 0))],
        out_specs=[spec, spec, spec],
        out_shape=[out, out, out],
        scratch_shapes=[pltpu.VMEM((N_HALVES, N_KEYS, t), F32),
                        pltpu.VMEM((N_HALVES, PEER_TOPK, t), F32),
                        pltpu.VMEM((N_HALVES, PEER_TOPK, t), F32),
                        pltpu.VMEM((PEER_HEADS, CAND_ROWS, t), F32),
                        pltpu.VMEM((PEER_HEADS, CAND_ROWS, t), F32),
                        pltpu.VMEM((PEER_HEADS, CAND_ROWS, t), F32),
                        pltpu.VMEM((N_SLOTS, t), F32),
                        pltpu.VMEM((N_SLOTS, t), F32),
                        pltpu.VMEM((N_SLOTS, t), F32)],
        compiler_params=_cparams(("arbitrary",)),
        name="peer_topk",
    )(q, keys)


EXP_TM = 512
EXP_HALF = EXP_TM // 2
EXP_J = 8
EXP_BLK = EXP_J * N_KEYS
N_EXP_BLK = N_KEYS * N_KEYS // EXP_BLK
G_PITCH = N_KEYS + SUBLANES
SQRT_HALF = 0.7071067811865476


def _bf16_bits(x):
    u = lax.bitcast_convert_type(x, jnp.uint32)
    return (u + jnp.uint32(0x7FFF) + ((u >> 16) & jnp.uint32(1))) >> 16


def _peer_expert_kernel(h_ref, i1_ref, i2_ref, g_ref, u_ref, v_ref, o_ref, gs_ref, w_ref):
    jj = pl.program_id(1)

    @pl.when(jj == 0)
    def _():
        o_ref[...] = jnp.zeros_like(o_ref)
        w_ref[...] = jnp.zeros_like(w_ref)
        rows = lax.broadcasted_iota(jnp.int32, (N_KEYS, N_KEYS), 0).astype(F32)

        def gate_matrix(t):
            i1r = i1_ref[pl.ds(t, 1), :]
            i2r = i2_ref[pl.ds(t, 1), :]
            gr = g_ref[pl.ds(t, 1), :]
            at = jnp.where(rows == i1r, gr, 0.0).astype(BF16)
            bt = jnp.where(rows == i2r, 1.0, 0.0).astype(BF16)
            return _dot_nt(at, bt)

        def tok(t, carry):
            packed = _bf16_bits(gate_matrix(t)) | (_bf16_bits(gate_matrix(t + EXP_HALF)) << 16)
            gs_ref[pl.ds(pl.multiple_of(t * G_PITCH, SUBLANES), N_KEYS), :] = packed
            return carry

        lax.fori_loop(0, EXP_HALF, tok, 0, unroll=4)

    slot = jj % 2
    o_ref[...] += _dot(w_ref[1 - slot], v_ref[...])

    jb = jnp.minimum(jj, N_EXP_BLK - 1)
    s = _dot_nt(h_ref[...], u_ref[...])
    packed = jnp.concatenate(
        [gs_ref[pl.ds(EXP_J * jb + c, EXP_HALF, stride=G_PITCH), :] for c in range(EXP_J)], axis=1)
    g_lo = lax.bitcast_convert_type(packed << 16, F32)
    g_hi = lax.bitcast_convert_type(packed & jnp.uint32(0xFFFF0000), F32)
    g = jnp.concatenate([g_lo, g_hi], axis=0)
    act = 0.5 * s * (1.0 + lax.erf(s * SQRT_HALF))
    w_ref[slot] = (g * act).astype(BF16)


def peer_expert(h, i1, i2, g, u, v):
    n = h.shape[0]
    row = lambda i, j: (i, 0)
    sel = pl.BlockSpec((EXP_TM, N_SLOTS), row)
    return pl.pallas_call(
        _peer_expert_kernel,
        grid=(n // EXP_TM, N_EXP_BLK + 1),
        in_specs=[pl.BlockSpec((EXP_TM, D), row), sel, sel, sel,
                  pl.BlockSpec((EXP_BLK, D), lambda i, j: (jnp.minimum(j, N_EXP_BLK - 1), 0)),
                  pl.BlockSpec((EXP_BLK, D), lambda i, j: (jnp.maximum(j - 1, 0), 0))],
        out_specs=pl.BlockSpec((EXP_TM, D), row),
        out_shape=jax.ShapeDtypeStruct((n, D), F32),
        scratch_shapes=[pltpu.VMEM((EXP_HALF * G_PITCH, N_KEYS), jnp.uint32),
                        pltpu.VMEM((2, EXP_TM, EXP_BLK), BF16)],
        compiler_params=_cparams(("arbitrary", "arbitrary")),
        name="peer_expert",
    )(h, i1, i2, g, u, v)


def peer_ffn(h, wq, keys, u, v):
    nb = h.shape[0]
    hf = h.reshape(nb * T_ALL, D)
    i1, i2, g = peer_topk(matmul(hf, wq), keys)
    return peer_expert(hf, i1, i2, g, u, v).reshape(nb, T_ALL, D)


def _head_ones(width, group):
    r = np.arange(width) // group
    return jnp.asarray((r[:, None] == r[None, :]).astype(np.float32))


def _store_head_pairs(o_ref, x, y):
    for h in range(RW_H):
        ln = slice(h * HEAD, (h + 1) * HEAD)
        o_ref[pl.ds(h, ROW_BLK, stride=RW_H), :] = jnp.concatenate([x[:, ln], y[:, ln]], axis=1)


def _rw_pre_kernel(z_ref, hp_ref, hn_ref, mu_ref, w0_ref, w2_ref, a0_ref, a2_ref, g2_ref,
                   kk_ref, ka_ref, ones_ref,
                   r_ref, k_ref, v_ref, g_ref, rv_ref, awf_ref, awb_ref, bkf_ref, bkb_ref):
    z = z_ref[0]
    row = lax.broadcasted_iota(jnp.int32, (ROW_BLK, 1), 0)
    prev = jnp.where(row == 0, hp_ref[0, 0], pltpu.roll(z, 1, 0))
    nxt = jnp.where(row == ROW_BLK - 1, hn_ref[0, 0], pltpu.roll(z, ROW_BLK - 1, 0))
    zs = z + mu_ref[0:1, :] * (prev - z) + mu_ref[1:2, :] * (nxt - z)
    r = zs[:, 0:RW_W]
    k = zs[:, RW_W:2 * RW_W]
    v = zs[:, 2 * RW_W:3 * RW_W]
    o = 3 * RW_W
    lw = jnp.tanh(zs[:, o:o + 2 * LORA])
    la = zs[:, o + 2 * LORA:o + 4 * LORA]
    lg = jax.nn.sigmoid(zs[:, o + 4 * LORA:o + 4 * LORA + LORA_G])
    r_ref[0] = r
    k_ref[0] = k
    v_ref[0] = v
    g_ref[0] = _dot(lg, g2_ref[...], precision=HI)
    kkr = k * kk_ref[...]
    ss = _dot(kkr * kkr, ones_ref[...], precision=HI)
    kkn = kkr / jnp.maximum(jnp.sqrt(ss), 1e-12)
    _store_head_pairs(rv_ref, r, v)
    for d, (aw_ref, bk_ref) in enumerate(((awf_ref, bkf_ref), (awb_ref, bkb_ref))):
        dec = w0_ref[d:d + 1, :] + _dot(lw, w2_ref[d], precision=HI)
        a = jax.nn.sigmoid(a0_ref[d:d + 1, :] + _dot(la, a2_ref[d], precision=HI))
        _store_head_pairs(aw_ref, -kkn, jnp.exp(-W_DECAY_SCALE * jax.nn.sigmoid(dec)))
        _store_head_pairs(bk_ref, kkn * a, k * (1.0 + (a - 1.0) * ka_ref[...]))


def rw_pre(z, mu, w0, w2, a0, a2, g2, kk, ka):
    nb = z.shape[0]
    zr = z[:, :, :RW_COLS]
    zero = jnp.zeros((nb, 1, RW_COLS), F32)
    last = zr[:, ROW_BLK - 1::ROW_BLK]
    first = zr[:, ::ROW_BLK]
    halo_prev = jnp.concatenate([zero, zero, last[:, 1:N_ROW_BLK - 1]], axis=1)
    halo_next = jnp.concatenate([zero, first[:, 2:], zero], axis=1)
    halo_prev = halo_prev.reshape(nb, N_ROW_BLK, 1, RW_COLS)
    halo_next = halo_next.reshape(nb, N_ROW_BLK, 1, RW_COLS)
    zpad = jnp.zeros((LORA, RW_W), F32)
    w2p = jnp.stack([jnp.concatenate([w2[0], zpad]), jnp.concatenate([zpad, w2[1]])])
    a2p = jnp.stack([jnp.concatenate([a2[0], zpad]), jnp.concatenate([zpad, a2[1]])])
    row = lambda b, i: (b, i, 0)
    full = lambda shape: pl.BlockSpec(shape, lambda b, i: (0,) * len(shape))
    out = jax.ShapeDtypeStruct((nb, T_ALL, RW_W), F32)
    ospec = pl.BlockSpec((1, ROW_BLK, RW_W), row)
    pout = jax.ShapeDtypeStruct((nb, T_ALL * RW_H, LANES), F32)
    pspec = pl.BlockSpec((None, ROW_BLK * RW_H, LANES), row)
    return pl.pallas_call(
        _rw_pre_kernel,
        grid=(nb, N_ROW_BLK),
        in_specs=[pl.BlockSpec((1, ROW_BLK, RW_COLS), row),
                  pl.BlockSpec((1, 1, 1, RW_COLS), lambda b, i: (b, i, 0, 0)),
                  pl.BlockSpec((1, 1, 1, RW_COLS), lambda b, i: (b, i, 0, 0)),
                  full((2, RW_COLS)), full((2, RW_W)), full((2, 2 * LORA, RW_W)),
                  full((2, RW_W)), full((2, 2 * LORA, RW_W)), full((LORA_G, RW_W)),
                  full((1, RW_W)), full((1, RW_W)), full((RW_W, RW_W))],
        out_specs=[ospec] * 4 + [pspec] * 5,
        out_shape=[out] * 4 + [pout] * 5,
        compiler_params=_cparams(("arbitrary", "arbitrary")),
        name="rw_pre",
    )(z, halo_prev, halo_next, mu, w0, w2p, a0, a2p, g2, kk.reshape(1, RW_W),
      ka.reshape(1, RW_W), _head_ones(RW_W, HEAD))


SCAN_TB = 32
SCAN_I = HEAD // 2


SCAN_NBLK = T_ALL // SCAN_TB
SCAN_CTX_BLK = T_CTX // SCAN_TB


def _rw_scan_kernel(awf_ref, bkf_ref, rvf_ref, awb_ref, bkb_ref, rvb_ref, yf_ref, yb_ref,
                    s_ref, sa_ref, *tiles):
    @pl.when(pl.program_id(0) == 0)
    def _():
        s_ref[...] = jnp.zeros_like(s_ref)

    low_half = lax.broadcasted_iota(jnp.int32, (1, LANES), 1) < LANES // 2
    sets = (tiles[:6], tiles[6:])

    def prep(t, tile_set):
        a_s, w_s, b_s, k_s, r_s, v_s = tile_set
        rf = pl.ds(pl.multiple_of(t * RW_H, RW_H), RW_H)
        rb = pl.ds(pl.multiple_of((SCAN_TB - 1 - t) * RW_H, RW_H), RW_H)

        def transposed(f_ref, b_ref):
            rows = [f_ref[b, rf, :] for b in range(NB)] + [b_ref[b, rb, :] for b in range(NB)]
            return jnp.concatenate(rows + rows, axis=0).T

        aw = transposed(awf_ref, awb_ref)
        a_s[...] = aw[:HEAD]
        w_s[...] = aw[HEAD:]
        bk = transposed(bkf_ref, bkb_ref)
        b_s[...] = bk[:HEAD]
        k_s[...] = bk[HEAD:]
        rv = transposed(rvf_ref, rvb_ref)
        r_s[...] = rv[:HEAD]
        v_s[...] = jnp.where(low_half, rv[HEAD:HEAD + SCAN_I], rv[HEAD + SCAN_I:])

    def step(t, tile_set):
        a_s, w_s, b_s, k_s, r_s, v_s = tile_set
        a = a_s[...]
        for i in range(SCAN_I):
            sa_ref[pl.ds(i, 1), :] = jnp.sum(s_ref[i] * a, axis=0, keepdims=True)
        w = w_s[...]
        b = b_s[...]
        k = k_s[...]
        r = r_s[...]
        for i in range(SCAN_I):
            sn = s_ref[i] * w + sa_ref[pl.ds(i, 1), :] * b + v_s[pl.ds(i, 1), :] * k
            s_ref[i] = sn
            y = jnp.sum(sn * r, axis=0, keepdims=True)
            yf_ref[t, pl.ds(i, 1), :] = y
            yb_ref[SCAN_TB - 1 - t, pl.ds(i, 1), :] = y

    prep(0, sets[0])

    def two_steps(u, carry):
        t = 2 * u
        prep(t + 1, sets[1])
        step(t, sets[0])
        prep(jnp.minimum(t + 2, SCAN_TB - 1), sets[0])
        step(t + 1, sets[1])
        return carry

    lax.fori_loop(0, SCAN_TB // 2, two_steps, 0)


def _mirror_block(g):
    return jnp.where(g < SCAN_CTX_BLK, SCAN_CTX_BLK - 1 - g, SCAN_NBLK + SCAN_CTX_BLK - 1 - g)


def rw_scan(awf, bkf, awb, bkb, rv):
    assert awf.shape[0] == NB
    fspec = pl.BlockSpec((NB, SCAN_TB * RW_H, LANES), lambda g: (0, g, 0))
    bspec = pl.BlockSpec((NB, SCAN_TB * RW_H, LANES), lambda g: (0, _mirror_block(g), 0))
    yf = pl.BlockSpec((SCAN_TB, SCAN_I, LANES), lambda g: (g, 0, 0))
    yb = pl.BlockSpec((SCAN_TB, SCAN_I, LANES), lambda g: (_mirror_block(g), 0, 0))
    out = jax.ShapeDtypeStruct((T_ALL, SCAN_I, LANES), F32)
    tile_set = [pltpu.VMEM((HEAD, LANES), F32)] * 5 + [pltpu.VMEM((SCAN_I, LANES), F32)]
    return pl.pallas_call(
        _rw_scan_kernel,
        grid=(SCAN_NBLK,),
        in_specs=[fspec] * 3 + [bspec] * 3,
        out_specs=[yf, yb],
        out_shape=[out, out],
        scratch_shapes=[pltpu.VMEM((SCAN_I, HEAD, LANES), F32),
                        pltpu.VMEM((SCAN_I, LANES), F32)] + tile_set + tile_set,
        compiler_params=_cparams(("arbitrary",)),
        name="rw_scan",
    )(awf, bkf, rv, awb, bkb, rv)


def _scan_unlayout_i(y, d):
    s = y.reshape(T_ALL, SCAN_I, 2, 2, NB, RW_H)[:, :, :, d]
    return s.transpose(3, 0, 4, 2, 1).reshape(NB, T_ALL, RW_W)


def _rw_post_kernel(yf_ref, yb_ref, r_ref, k_ref, v_ref, g_ref, rk_ref, lw_ref, lb_ref, ones_ref,
                    o_ref):
    y = yf_ref[0] + yb_ref[0]
    ones = ones_ref[...]
    mean = _dot(y, ones, precision=HI) * (1.0 / HEAD)
    yc = y - mean
    var = _dot(yc * yc, ones, precision=HI) * (1.0 / HEAD)
    yn = yc * lax.rsqrt(var + RW_LN_EPS) * lw_ref[...] + lb_ref[...]
    bonus = _dot(r_ref[0] * k_ref[0] * rk_ref[...], ones, precision=HI) * v_ref[0]
    o_ref[0] = ((yn + bonus) * g_ref[0]).astype(BF16)


def rw_post(yf, yb, r, k, v, g, rk, ln_w, ln_b):
    nb = yf.shape[0]
    row = lambda b, i: (b, i, 0)
    spec = pl.BlockSpec((1, ROW_BLK, RW_W), row)
    vec = pl.BlockSpec((1, RW_W), lambda b, i: (0, 0))
    return pl.pallas_call(
        _rw_post_kernel,
        grid=(nb, N_ROW_BLK),
        in_specs=[spec] * 6 + [vec] * 3 + [pl.BlockSpec((RW_W, RW_W), lambda b, i: (0, 0))],
        out_specs=spec,
        out_shape=jax.ShapeDtypeStruct((nb, T_ALL, RW_W), BF16),
        compiler_params=_cparams(("arbitrary", "arbitrary")),
        name="rw_post",
    )(yf, yb, r, k, v, g, rk.reshape(1, RW_W), ln_w.reshape(1, RW_W), ln_b.reshape(1, RW_W),
      _head_ones(RW_W, HEAD))


def rwkv_mixer(z, mu, w0, w2, a0, a2, g2, kk, ka, rk, ln_w, ln_b):
    r, k, v, g, rv, awf, awb, bkf, bkb = rw_pre(z, mu, w0, w2, a0, a2, g2, kk, ka)
    y1, y2 = rw_scan(awf, bkf, awb, bkb, rv)
    return rw_post(_scan_unlayout_i(y1, 0), _scan_unlayout_i(y2, 1), r, k, v, g, rk, ln_w, ln_b)


NA_QROWS = ROW_BLK // GRID_W
NA_SLAB = NA_KH + NA_QROWS - 1
NA_SLAB_T = NA_SLAB * GRID_W
NA_SCALE = HEAD ** -0.5


def _na_bias_index():
    a = np.arange(NA_QROWS)[:, None]
    u = np.arange(NA_SLAB)[None, :]
    idx_r, valid = [], []
    for r0, u0 in ((0, 0), (NA_QROWS, 0), (GRID_ROWS - NA_QROWS, GRID_ROWS - NA_SLAB)):
        r = r0 + a
        kr = u0 + u
        kr0 = np.clip(r - NA_KH // 2, 0, GRID_ROWS - NA_KH)
        valid.append((kr >= kr0) & (kr < kr0 + NA_KH))
        idx_r.append(np.clip(kr - r + NA_KH - 1, 0, 2 * NA_KH - 2))
    idx_r = np.stack(idx_r)
    valid = np.stack(valid)[:, :, None, :, None]
    qc = np.arange(GRID_W)[:, None]
    kc = np.arange(GRID_W)[None, :]
    c0 = np.clip(qc - NA_KW // 2, 0, GRID_W - NA_KW)
    in_win = ((kc >= c0) & (kc < c0 + NA_KW))[None, None, :, None, :]
    idx_c = np.clip(kc - qc + NA_KW - 1, 0, 2 * NA_KW - 2)
    col_onehot = (idx_c[None] == np.arange(2 * NA_KW - 1)[:, None, None]).astype(np.float32)
    shape = (3, NA_QROWS, GRID_W, NA_SLAB, GRID_W)
    return idx_r, col_onehot, np.broadcast_to(valid & in_win, shape)


def na_bias_tables(rpb):
    idx_r, col_onehot, mask = _na_bias_index()
    rows = rpb[:, idx_r.reshape(-1), :].reshape(NA_H, 3, NA_QROWS, NA_SLAB, 2 * NA_KW - 1)
    b = jnp.einsum('hvauc,cqk->hvaquk', rows, jnp.asarray(col_onehot), precision=HI)
    b = jnp.where(mask[None], b, NEG_INF)
    return b.reshape(NA_H, 3, ROW_BLK, NA_SLAB_T)


def _na_kernel(q_ref, k_ref, v_ref, bias_ref, o_ref):
    qi = pl.program_id(2)

    @pl.when(qi == 0)
    def _():
        ys = []
        for hh in range(2):
            ln = slice(hh * HEAD, (hh + 1) * HEAD)
            q = (q_ref[0, :, ln] * NA_SCALE).astype(BF16)
            s = _dot_nt(q, k_ref[0, 0:T_CTX, ln].astype(BF16))
            p = jnp.exp(s - jnp.max(s, axis=-1, keepdims=True))
            y = _dot(p.astype(BF16), v_ref[0, 0:T_CTX, ln].astype(BF16))
            ys.append(y / jnp.sum(p, axis=-1, keepdims=True))
        o_ref[0] = jnp.concatenate(ys, axis=1).astype(BF16)

    @pl.when(qi > 0)
    def _():
        u0 = jnp.clip(NA_QROWS * (qi - 1) - NA_KH // 2, 0, GRID_ROWS - NA_SLAB)
        start = pl.multiple_of(T_CTX + GRID_W * u0, GRID_W)
        ys = []
        for hh in range(2):
            ln = slice(hh * HEAD, (hh + 1) * HEAD)
            q = (q_ref[0, :, ln] * NA_SCALE).astype(BF16)
            sc = _dot_nt(q, k_ref[0, 0:T_CTX, ln].astype(BF16))
            sw = _dot_nt(q, k_ref[0, pl.ds(start, NA_SLAB_T), ln].astype(BF16)) + bias_ref[hh, 0]
            m = jnp.maximum(jnp.max(sc, axis=-1, keepdims=True), jnp.max(sw, axis=-1, keepdims=True))
            pc = jnp.exp(sc - m)
            pw = jnp.exp(sw - m)
            y = (_dot(pw.astype(BF16), v_ref[0, pl.ds(start, NA_SLAB_T), ln].astype(BF16))
                 + _dot(pc.astype(BF16), v_ref[0, 0:T_CTX, ln].astype(BF16)))
            den = jnp.sum(pc, axis=-1, keepdims=True) + jnp.sum(pw, axis=-1, keepdims=True)
            ys.append(y / den)
        o_ref[0] = jnp.concatenate(ys, axis=1).astype(BF16)


def natten_mixer(z, bias):
    nb = z.shape[0]
    qb, kb, vb = (NA_OFF // LANES, (NA_OFF + NA_W) // LANES, (NA_OFF + 2 * NA_W) // LANES)
    n_blk = N_ROW_BLK - 1

    def bias_idx(b, hp, qi):
        var = jnp.where(qi <= 1, 0, jnp.where(qi == n_blk, 2, 1))
        return (hp, var, 0, 0)

    return pl.pallas_call(
        _na_kernel,
        grid=(nb, NA_H // 2, N_ROW_BLK),
        in_specs=[pl.BlockSpec((1, ROW_BLK, LANES), lambda b, hp, qi: (b, qi, qb + hp)),
                  pl.BlockSpec((1, T_ALL, LANES), lambda b, hp, qi: (b, 0, kb + hp)),
                  pl.BlockSpec((1, T_ALL, LANES), lambda b, hp, qi: (b, 0, vb + hp)),
                  pl.BlockSpec((2, 1, ROW_BLK, NA_SLAB_T), bias_idx)],
        out_specs=pl.BlockSpec((1, ROW_BLK, LANES), lambda b, hp, qi: (b, qi, hp)),
        out_shape=jax.ShapeDtypeStruct((nb, T_ALL, NA_W), BF16),
        compiler_params=_cparams(("arbitrary", "arbitrary", "arbitrary")),
        name="natten",
    )(z, z, z, bias)


ROPE_NF = MB_N // 4


def rope_tables():
    pos = np.arange(T_LAT)
    inv = ROPE_BASE ** (-np.arange(ROPE_NF, dtype=np.float32) / ROPE_NF)
    lane = np.arange(MB_N)
    p = np.where(lane[None, :] < MB_N // 2, (pos // GRID_W)[:, None], (pos % GRID_W)[:, None])
    ang = p.astype(np.float32) * inv[lane % ROPE_NF][None, :]
    sign = np.where((lane % (2 * ROPE_NF)) < ROPE_NF, -1.0, 1.0)[None, :]
    cos = np.concatenate([np.ones((T_CTX, MB_N), np.float32), np.cos(ang)])
    sin = np.concatenate([np.zeros((T_CTX, MB_N), np.float32), np.sin(ang) * sign])
    return jnp.asarray(cos, F32), jnp.asarray(sin, F32)


def _mb_pre_kernel(z_ref, w_ref, b_ref, cos_ref, sin_ref, o_ref):
    z = z_ref[0]
    pos = lax.broadcasted_iota(jnp.int32, (T_ALL, 1), 0)
    lo = jnp.where(pos < T_CTX, 0, T_CTX)
    hi = jnp.where(pos < T_CTX, T_CTX, T_ALL)
    half = MB_CONV // 2
    acc = z * w_ref[half:half + 1, :] + b_ref[...]
    for d in (-2, -1, 1, 2):
        zr = pltpu.roll(z, (-d) % T_ALL, 0)
        ok = (pos + d >= lo) & (pos + d < hi)
        acc = acc + jnp.where(ok, zr, 0.0) * w_ref[half + d:half + d + 1, :]
    y = acc * jax.nn.sigmoid(acc)
    is_bc = pl.program_id(1) >= MB_INNER // LANES

    @pl.when(jnp.logical_not(is_bc))
    def _():
        o_ref[0] = y

    @pl.when(is_bc)
    def _():
        lane = lax.broadcasted_iota(jnp.int32, (1, MB_N), 1)
        first = (lane % (2 * ROPE_NF)) < ROPE_NF
        partner = jnp.where(first, pltpu.roll(y, MB_N - ROPE_NF, 1), pltpu.roll(y, ROPE_NF, 1))
        o_ref[0] = y * cos_ref[...] + partner * sin_ref[...]


def mb_pre(z, conv_w, conv_b, cos, sin):
    nb = z.shape[0]
    c0 = XBC_OFF // LANES
    return pl.pallas_call(
        _mb_pre_kernel,
        grid=(nb, MB_CONV_CH // LANES),
        in_specs=[pl.BlockSpec((1, T_ALL, LANES), lambda b, j: (b, 0, c0 + j)),
                  pl.BlockSpec((MB_CONV, LANES), lambda b, j: (0, j)),
                  pl.BlockSpec((1, LANES), lambda b, j: (0, j)),
                  pl.BlockSpec((T_ALL, MB_N), lambda b, j: (0, 0)),
                  pl.BlockSpec((T_ALL, MB_N), lambda b, j: (0, 0))],
        out_specs=pl.BlockSpec((1, T_ALL, LANES), lambda b, j: (b, 0, j)),
        out_shape=jax.ShapeDtypeStruct((nb, T_ALL, MB_CONV_CH), F32),
        compiler_params=_cparams(("arbitrary", "arbitrary")),
        name="mb_pre",
    )(z, conv_w, conv_b.reshape(1, MB_CONV_CH), cos, sin)


N_CHUNK = T_ALL // MB_CHUNK
N_CTX_CHUNK = T_CTX // MB_CHUNK
XB_BLK = MB_INNER // LANES
XC_BLK = XB_BLK + MB_G


def _ssd_direction(reverse, x_ref, b_ref, c_ref, dtc_ref, dtr_ref, bc_ref, br_ref, ac_ref, ar_ref,
                   st_ref, y_ref):
    L = MB_CHUNK
    dtc = jax.nn.softplus(dtc_ref[0, 0, 0] + bc_ref[0, 0])
    dtr = jax.nn.softplus(dtr_ref[0, 0, 0] + br_ref[0, 0])
    dac = dtc * (-jnp.exp(ac_ref[0, 0]))
    dar = dtr * (-jnp.exp(ar_ref[0, 0]))
    ri = lax.broadcasted_iota(jnp.int32, (L, L), 0)
    ci = lax.broadcasted_iota(jnp.int32, (L, L), 1)
    lower = (ri >= ci).astype(F32)
    upper = (ri <= ci).astype(F32)
    if reverse:
        cum_c = _dot(upper, dac, precision=HI)
        cum_r = _dot(dar, lower, precision=HI)
        mask = ri <= ci
        tot_row = 0
    else:
        cum_c = _dot(lower, dac, precision=HI)
        cum_r = _dot(dar, upper, precision=HI)
        mask = ri >= ci
        tot_row = L - 1
    bm = b_ref[0]
    cm = c_ref[0].astype(BF16)
    cb = _dot_nt(cm, bm.astype(BF16))
    bt = bm.T.astype(BF16)
    x = x_ref[0]
    ys = []
    for r in range(MB_R):
        cc = cum_c[:, r:r + 1]
        cr = cum_r[r:r + 1, :]
        tot = cc[tot_row:tot_row + 1, :]
        lmat = jnp.exp(jnp.where(mask, cc - cr, -jnp.inf))
        xc = x[:, r * HEAD:(r + 1) * HEAD] * dtc[:, r:r + 1]
        y_diag = _dot((cb * lmat).astype(BF16), xc.astype(BF16))
        st = st_ref[r]
        y_off = _dot(cm, st.astype(BF16)) * jnp.exp(cc)
        xdec = (xc * jnp.exp(tot - cc)).astype(BF16)
        st_ref[r] = jnp.exp(tot) * st + _dot(bt, xdec)
        ys.append(y_diag + y_off)
    y_ref[0] = jnp.concatenate(ys, axis=1)


def _mb_ssd_kernel(*refs):
    fwd, bwd = refs[0:9], refs[9:18]
    yf_ref, yb_ref, stf_ref, stb_ref = refs[18:22]

    @pl.when(pl.program_id(2) == 0)
    def _():
        stf_ref[...] = jnp.zeros_like(stf_ref)
        stb_ref[...] = jnp.zeros_like(stb_ref)

    _ssd_direction(False, *fwd, stf_ref, yf_ref)
    _ssd_direction(True, *bwd, stb_ref, yb_ref)


def _bwd_chunk(i):
    return jnp.where(i < N_CTX_CHUNK, N_CTX_CHUNK - 1 - i, N_CHUNK + N_CTX_CHUNK - 1 - i)


def mb_ssd(xbc, dt_raw, dt_bias, a_log):
    nb = xbc.shape[0]
    dt5 = dt_raw.reshape(nb, T_ALL, 2, MB_G, MB_R)
    dtc = dt5.transpose(0, 2, 3, 1, 4)
    dtr = dt5.transpose(0, 2, 3, 4, 1)
    b4 = dt_bias.reshape(2, MB_G, MB_R)
    a4 = a_log.reshape(2, MB_G, MB_R)
    ins, specs = [], []
    for d, cidx in ((0, lambda i: i), (1, _bwd_chunk)):
        ins += [xbc, xbc, xbc, dtc, dtr, b4[:, :, None, :], b4[:, :, :, None],
                a4[:, :, None, :], a4[:, :, :, None]]
        specs += [
            pl.BlockSpec((1, MB_CHUNK, MB_R * HEAD), lambda b, g, i, c=cidx: (b, c(i), g)),
            pl.BlockSpec((1, MB_CHUNK, MB_N), lambda b, g, i, c=cidx: (b, c(i), XB_BLK + g)),
            pl.BlockSpec((1, MB_CHUNK, MB_N), lambda b, g, i, c=cidx: (b, c(i), XC_BLK + g)),
            pl.BlockSpec((1, 1, 1, MB_CHUNK, MB_R), lambda b, g, i, c=cidx, d=d: (b, d, g, c(i), 0)),
            pl.BlockSpec((1, 1, 1, MB_R, MB_CHUNK), lambda b, g, i, c=cidx, d=d: (b, d, g, 0, c(i))),
            pl.BlockSpec((1, 1, 1, MB_R), lambda b, g, i, d=d: (d, g, 0, 0)),
            pl.BlockSpec((1, 1, MB_R, 1), lambda b, g, i, d=d: (d, g, 0, 0)),
            pl.BlockSpec((1, 1, 1, MB_R), lambda b, g, i, d=d: (d, g, 0, 0)),
            pl.BlockSpec((1, 1, MB_R, 1), lambda b, g, i, d=d: (d, g, 0, 0)),
        ]
    out = jax.ShapeDtypeStruct((nb, T_ALL, MB_INNER), F32)
    return pl.pallas_call(
        _mb_ssd_kernel,
        grid=(nb, MB_G, N_CHUNK),
        in_specs=specs,
        out_specs=[pl.BlockSpec((1, MB_CHUNK, MB_R * HEAD), lambda b, g, i: (b, i, g)),
                   pl.BlockSpec((1, MB_CHUNK, MB_R * HEAD), lambda b, g, i: (b, _bwd_chunk(i), g))],
        out_shape=[out, out],
        scratch_shapes=[pltpu.VMEM((MB_R, MB_N, HEAD), F32), pltpu.VMEM((MB_R, MB_N, HEAD), F32)],
        compiler_params=_cparams(("arbitrary", "arbitrary", "arbitrary")),
        name="mb_ssd",
    )(*ins)


MB_GW = MB_INNER // MB_G


def _mb_post_kernel(yf_ref, yb_ref, x_ref, glo_ref, ghi_ref, d_ref, nw_ref, o_ref):
    y = yf_ref[0] + yb_ref[0] + x_ref[0] * d_ref[...]
    gate = jnp.concatenate([glo_ref[0], ghi_ref[0]], axis=1)
    y = y * (gate * jax.nn.sigmoid(gate))
    ms = jnp.mean(y * y, axis=-1, keepdims=True)
    o_ref[0] = (y * lax.rsqrt(ms + NORM_EPS) * nw_ref[...]).astype(BF16)


def mb_post(yf, yb, xbc, z, d_skip, norm_w):
    nb = yf.shape[0]
    g0 = MB_OFF // LANES
    grp = lambda b, i, g: (b, i, g)
    return pl.pallas_call(
        _mb_post_kernel,
        grid=(nb, N_ROW_BLK, MB_G),
        in_specs=[pl.BlockSpec((1, ROW_BLK, MB_GW), grp),
                  pl.BlockSpec((1, ROW_BLK, MB_GW), grp),
                  pl.BlockSpec((1, ROW_BLK, MB_GW), grp),
                  pl.BlockSpec((1, ROW_BLK, LANES), lambda b, i, g: (b, i, g0 + 2 * g)),
                  pl.BlockSpec((1, ROW_BLK, LANES), lambda b, i, g: (b, i, g0 + 2 * g + 1)),
                  pl.BlockSpec((1, MB_GW), lambda b, i, g: (0, g)),
                  pl.BlockSpec((1, MB_GW), lambda b, i, g: (0, g))],
        out_specs=pl.BlockSpec((1, ROW_BLK, MB_GW), grp),
        out_shape=jax.ShapeDtypeStruct((nb, T_ALL, MB_INNER), BF16),
        compiler_params=_cparams(("arbitrary", "arbitrary", "arbitrary")),
        name="mb_post",
    )(yf, yb, xbc, z, z, jnp.repeat(d_skip, HEAD).reshape(1, MB_INNER), norm_w.reshape(1, MB_INNER))


def mamba_mixer(z, conv_w, conv_b, dt_bias, a_log, d_skip, norm_w, cos, sin):
    xbc = mb_pre(z, conv_w, conv_b, cos, sin)
    yf, yb = mb_ssd(xbc, z[:, :, DT_OFF:DT_OFF + 2 * MB_H], dt_bias, a_log)
    return mb_post(yf, yb, xbc, z, d_skip, norm_w)


def kernel(x, c, ctx, c_ctx, ada_w, ada_b, norm1_w, norm2_w, w_in, w_out, rw_mu, rw_w0, rw_w2, rw_a0, rw_a2, rw_g2, rw_kk, rw_ka, rw_rk, rw_ln_w, rw_ln_b, na_rpb, mb_conv_w, mb_conv_b, mb_dt_bias, mb_a_log, mb_d, mb_norm_w, pe_wq, pe_keys, pe_u, pe_v, final_norm_w):
    nb = x.shape[0]
    xs = jnp.concatenate([ctx, x], axis=1)
    cond = jnp.zeros((SUBLANES, D), F32).at[:nb].set(c).at[nb].set(c_ctx)
    mods = ada_rows(cond, ada_w, ada_b)
    cos, sin = rope_tables()
    f = prev_tab = None
    for l in range(DEPTH):
        m = mods[l].reshape(SUBLANES, 6, D)
        tab = jnp.stack([jnp.broadcast_to(m[nb], (nb, 6, D)), m[:nb]], axis=1).reshape(2 * nb, 6, D)
        if f is None:
            h1 = norm_mod(xs, norm1_w[l].reshape(1, D), tab, 0)
        else:
            xs, h1 = resid_norm_mod(xs, f, prev_tab, norm1_w[l].reshape(1, D), tab)
        z = matmul(h1.reshape(nb * T_ALL, D), cast_bf16(w_in[l], ROW_BLK)).reshape(nb, T_ALL, IN_COLS)
        y_rw = rwkv_mixer(z, rw_mu[l], rw_w0[l], rw_w2[l], rw_a0[l], rw_a2[l], rw_g2[l], rw_kk[l],
                          rw_ka[l], rw_rk[l].reshape(RW_W), rw_ln_w[l], rw_ln_b[l])
        y_na = natten_mixer(z, na_bias_tables(na_rpb[l]))
        y_mb = mamba_mixer(z, mb_conv_w[l], mb_conv_b[l], mb_dt_bias[l], mb_a_log[l], mb_d[l],
                           mb_norm_w[l], cos, sin)
        xs = out_proj(xs, y_rw, y_na, y_mb, cast_bf16(w_out[l], ROW_BLK), tab)
        h2 = norm_mod(xs, norm2_w[l].reshape(1, D), tab, 3)
        f = peer_ffn(h2, cast_bf16(pe_wq[l], ROW_BLK), pe_keys[l],
                     cast_bf16(pe_u[l], 1024), cast_bf16(pe_v[l], 1024))
        prev_tab = tab
    return final_norm(xs, f, prev_tab, final_norm_w.reshape(1, D))
```

```python
import functools
import math

import numpy as np
import jax
import jax.numpy as jnp
from jax import lax
from jax.experimental import pallas as pl
from jax.experimental.pallas import tpu as pltpu

D = 2048
NB = 4
T_LAT = 2048
T_CTX = 256
T_ALL = T_CTX + T_LAT
DEPTH = 4
GRID_W = 64
GRID_ROWS = T_LAT // GRID_W
HEAD = 64
RW_W = 512
RW_H = 8
LORA = 64
LORA_G = 128
W_DECAY_SCALE = 0.606531
RW_LN_EPS = 64e-5
NA_W = 512
NA_H = 8
NA_KH = 8
NA_KW = 16
MB_INNER = 1024
MB_H = 16
MB_G = 4
MB_R = 4
MB_N = 128
MB_CONV = 5
MB_CHUNK = 128
ROPE_BASE = 10000.0
PEER_HEADS = 8
N_KEYS = 128
PEER_TOPK = 16
NORM_EPS = 1e-6
NEG_INF = -1e30
RW_COLS = 3 * RW_W + 4 * LORA + LORA_G
NA_COLS = 3 * NA_W
MB_CONV_CH = MB_INNER + 2 * MB_G * MB_N
MB_COLS = MB_INNER + MB_CONV_CH + 2 * MB_H
IN_COLS = RW_COLS + NA_COLS + MB_COLS
NA_OFF = RW_COLS
MB_OFF = RW_COLS + NA_COLS
XBC_OFF = MB_OFF + MB_INNER
DT_OFF = XBC_OFF + MB_CONV_CH

LANES = 128
SUBLANES = 8
VMEM_LIMIT = 56 * 1024 * 1024

ROW_BLK = 256
N_ROW_BLK = T_ALL // ROW_BLK
BF16 = jnp.bfloat16
F32 = jnp.float32
HI = lax.Precision.HIGHEST


def _cparams(sem):
    return pltpu.CompilerParams(dimension_semantics=sem, vmem_limit_bytes=VMEM_LIMIT)


def _dot(a, b, precision=None):
    return jnp.dot(a, b, preferred_element_type=F32, precision=precision)


def _dot_nt(a, b, precision=None):
    return lax.dot_general(a, b, (((1,), (1,)), ((), ())), preferred_element_type=F32,
                           precision=precision)


def _cast_kernel(x_ref, o_ref):
    o_ref[...] = x_ref[...].astype(BF16)


def cast_bf16(w, rows_blk):
    r, c = w.shape
    return pl.pallas_call(
        _cast_kernel,
        grid=(r // rows_blk,),
        in_specs=[pl.BlockSpec((rows_blk, c), lambda i: (i, 0))],
        out_specs=pl.BlockSpec((rows_blk, c), lambda i: (i, 0)),
        out_shape=jax.ShapeDtypeStruct((r, c), BF16),
        compiler_params=_cparams(("arbitrary",)),
        name="cast_bf16",
    )(w)


ADA_TN = 1024


def _ada_kernel(c_ref, w_ref, b_ref, o_ref):
    c = c_ref[...]
    s = c * jax.nn.sigmoid(c)
    o_ref[0] = _dot(s, w_ref[0], precision=HI) + b_ref[0]


def ada_rows(cond, ada_w, ada_b):
    nl = ada_w.shape[0]
    return pl.pallas_call(
        _ada_kernel,
        grid=(nl, 6 * D // ADA_TN),
        in_specs=[pl.BlockSpec((SUBLANES, D), lambda l, j: (0, 0)),
                  pl.BlockSpec((1, D, ADA_TN), lambda l, j: (l, 0, j)),
                  pl.BlockSpec((1, 1, ADA_TN), lambda l, j: (l, 0, j))],
        out_specs=pl.BlockSpec((1, SUBLANES, ADA_TN), lambda l, j: (l, 0, j)),
        out_shape=jax.ShapeDtypeStruct((nl, SUBLANES, 6 * D), F32),
        compiler_params=_cparams(("arbitrary", "arbitrary")),
        name="ada_rows",
    )(cond, ada_w, ada_b.reshape(nl, 1, 6 * D))


def _mod_index(b, i):
    return 2 * b + jnp.minimum(i, 1)


def _norm_mod_kernel(which, x_ref, nw_ref, tab_ref, o_ref):
    x = x_ref[0]
    ms = jnp.mean(x * x, axis=-1, keepdims=True)
    y = x * lax.rsqrt(ms + NORM_EPS) * nw_ref[...]
    shift = tab_ref[0, which:which + 1, :]
    scale = tab_ref[0, which + 1:which + 2, :]
    o_ref[0] = (y * (1.0 + scale) + shift).astype(BF16)


def norm_mod(x, nw, tab, which):
    nb = x.shape[0]
    return pl.pallas_call(
        functools.partial(_norm_mod_kernel, which),
        grid=(nb, N_ROW_BLK),
        in_specs=[pl.BlockSpec((1, ROW_BLK, D), lambda b, i: (b, i, 0)),
                  pl.BlockSpec((1, D), lambda b, i: (0, 0)),
                  pl.BlockSpec((1, 6, D), lambda b, i: (_mod_index(b, i), 0, 0))],
        out_specs=pl.BlockSpec((1, ROW_BLK, D), lambda b, i: (b, i, 0)),
        out_shape=jax.ShapeDtypeStruct(x.shape, BF16),
        compiler_params=_cparams(("arbitrary", "arbitrary")),
        name="norm_mod",
    )(x, nw, tab)


def _resid_norm_mod_kernel(x_ref, f_ref, ptab_ref, nw_ref, tab_ref, xo_ref, o_ref):
    x = x_ref[0] + ptab_ref[0, 5:6, :] * f_ref[0]
    xo_ref[0] = x
    ms = jnp.mean(x * x, axis=-1, keepdims=True)
    y = x * lax.rsqrt(ms + NORM_EPS) * nw_ref[...]
    o_ref[0] = (y * (1.0 + tab_ref[0, 1:2, :]) + tab_ref[0, 0:1, :]).astype(BF16)


def resid_norm_mod(x, f, prev_tab, nw, tab):
    nb = x.shape[0]
    row = lambda b, i: (b, i, 0)
    mod = lambda b, i: (_mod_index(b, i), 0, 0)
    return pl.pallas_call(
        _resid_norm_mod_kernel,
        grid=(nb, N_ROW_BLK),
        in_specs=[pl.BlockSpec((1, ROW_BLK, D), row),
                  pl.BlockSpec((1, ROW_BLK, D), row),
                  pl.BlockSpec((1, 6, D), mod),
                  pl.BlockSpec((1, D), lambda b, i: (0, 0)),
                  pl.BlockSpec((1, 6, D), mod)],
        out_specs=[pl.BlockSpec((1, ROW_BLK, D), row), pl.BlockSpec((1, ROW_BLK, D), row)],
        out_shape=[jax.ShapeDtypeStruct(x.shape, F32), jax.ShapeDtypeStruct(x.shape, BF16)],
        input_output_aliases={0: 0},
        compiler_params=_cparams(("arbitrary", "arbitrary")),
        name="resid_norm_mod",
    )(x, f, prev_tab, nw, tab)


def _final_norm_kernel(x_ref, f_ref, ptab_ref, nw_ref, o_ref):
    x = x_ref[0] + ptab_ref[0, 5:6, :] * f_ref[0]
    ms = jnp.mean(x * x, axis=-1, keepdims=True)
    o_ref[0] = x * lax.rsqrt(ms + NORM_EPS) * nw_ref[...]


def final_norm(x, f, prev_tab, nw):
    nb = x.shape[0]
    lat = lambda b, i: (b, i + 1, 0)
    return pl.pallas_call(
        _final_norm_kernel,
        grid=(nb, T_LAT // ROW_BLK),
        in_specs=[pl.BlockSpec((1, ROW_BLK, D), lat),
                  pl.BlockSpec((1, ROW_BLK, D), lat),
                  pl.BlockSpec((1, 6, D), lambda b, i: (2 * b + 1, 0, 0)),
                  pl.BlockSpec((1, D), lambda b, i: (0, 0))],
        out_specs=pl.BlockSpec((1, ROW_BLK, D), lambda b, i: (b, i, 0)),
        out_shape=jax.ShapeDtypeStruct((nb, T_LAT, D), F32),
        compiler_params=_cparams(("arbitrary", "arbitrary")),
        name="final_norm",
    )(x, f, prev_tab, nw)


MM_TM = 1024
MM_TN = 512


def _mm_kernel(a_ref, w_ref, o_ref):
    o_ref[...] = _dot(a_ref[...], w_ref[...])


def matmul(a, w):
    m, k = a.shape
    n = w.shape[1]
    tm = MM_TM if m % MM_TM == 0 else ROW_BLK
    return pl.pallas_call(
        _mm_kernel,
        grid=(m // tm, pl.cdiv(n, MM_TN)),
        in_specs=[pl.BlockSpec((tm, k), lambda i, j: (i, 0)),
                  pl.BlockSpec((k, MM_TN), lambda i, j: (0, j))],
        out_specs=pl.BlockSpec((tm, MM_TN), lambda i, j: (i, j)),
        out_shape=jax.ShapeDtypeStruct((m, n), F32),
        compiler_params=_cparams(("arbitrary", "arbitrary")),
        name="matmul",
    )(a, w)


def _out_proj_kernel(x_ref, rw_ref, na_ref, mb_ref, w_ref, tab_ref, o_ref):
    acc = _dot(rw_ref[0], w_ref[0:RW_W, :])
    acc += _dot(na_ref[0], w_ref[RW_W:RW_W + NA_W, :])
    acc += _dot(mb_ref[0], w_ref[RW_W + NA_W:, :])
    o_ref[0] = x_ref[0] + tab_ref[0, 2:3, :] * acc


def out_proj(x, y_rw, y_na, y_mb, w, tab):
    nb = x.shape[0]
    row = lambda b, i: (b, i, 0)
    return pl.pallas_call(
        _out_proj_kernel,
        grid=(nb, N_ROW_BLK),
        in_specs=[pl.BlockSpec((1, ROW_BLK, D), row),
                  pl.BlockSpec((1, ROW_BLK, RW_W), row),
                  pl.BlockSpec((1, ROW_BLK, NA_W), row),
                  pl.BlockSpec((1, ROW_BLK, MB_INNER), row),
                  pl.BlockSpec((D, D), lambda b, i: (0, 0)),
                  pl.BlockSpec((1, 6, D), lambda b, i: (_mod_index(b, i), 0, 0))],
        out_specs=pl.BlockSpec((1, ROW_BLK, D), row),
        out_shape=jax.ShapeDtypeStruct(x.shape, F32),
        input_output_aliases={0: 0},
        compiler_params=_cparams(("arbitrary", "arbitrary")),
        name="out_proj",
    )(x, y_rw, y_na, y_mb, w, tab)


TOPK_TT = 128


CAND_COUNTS = (16, 8, 5, 4, 3, 2, 2, 2)
CAND_ROWS = 16 + 8 * 7 + 8
N_HALVES = 2 * PEER_HEADS
N_SLOTS = PEER_HEADS * PEER_TOPK


def _first_max(s, rows, n):
    m = jnp.max(s, axis=0, keepdims=True)
    pos = jnp.min(jnp.where(s == m, rows, float(n)), axis=0, keepdims=True)
    return m, pos


def _peer_topk_kernel(q_ref, keys_ref, i1_ref, i2_ref, g_ref,
                      sc_ref, val_ref, idx_ref, cand_ref, c1_ref, c2_ref, best_ref, e1_ref, e2_ref):
    t = TOPK_TT
    for hp in range(N_HALVES):
        c0 = hp * N_KEYS
        qs = q_ref[:, c0:c0 + N_KEYS].astype(BF16)
        sc_ref[hp] = _dot_nt(keys_ref[hp // 2, hp % 2].astype(BF16), qs)

    rows = lax.broadcasted_iota(jnp.int32, (N_KEYS, t), 0).astype(F32)

    def stage1(r, carry):
        for hp in range(N_HALVES):
            s = sc_ref[hp]
            m, pos = _first_max(s, rows, N_KEYS)
            val_ref[hp, pl.ds(r, 1), :] = m
            idx_ref[hp, pl.ds(r, 1), :] = pos
            sc_ref[hp] = jnp.where(rows == pos, -jnp.inf, s)
        return carry

    lax.fori_loop(0, PEER_TOPK, stage1, 0)

    row8 = lax.broadcasted_iota(jnp.int32, (SUBLANES, t), 0)
    for h in range(PEER_HEADS):
        v1, v2 = val_ref[2 * h], val_ref[2 * h + 1]
        k1, k2 = idx_ref[2 * h], idx_ref[2 * h + 1]
        cand = [v1[0:1] + v2]
        c1 = [jnp.broadcast_to(k1[0:1], (PEER_TOPK, t))]
        c2 = [k2]
        for r1 in range(1, SUBLANES):
            cand.append(jnp.where(row8 < CAND_COUNTS[r1], v1[r1:r1 + 1] + v2[0:SUBLANES], -jnp.inf))
            c1.append(jnp.broadcast_to(k1[r1:r1 + 1], (SUBLANES, t)))
            c2.append(k2[0:SUBLANES])
        cand.append(v1[SUBLANES:] + v2[0:1])
        c1.append(k1[SUBLANES:])
        c2.append(jnp.broadcast_to(k2[0:1], (SUBLANES, t)))
        cand_ref[h] = jnp.concatenate(cand, axis=0)
        c1_ref[h] = jnp.concatenate(c1, axis=0)
        c2_ref[h] = jnp.concatenate(c2, axis=0)

    crow = lax.broadcasted_iota(jnp.int32, (CAND_ROWS, t), 0).astype(F32)

    def stage2(r, carry):
        for h in range(PEER_HEADS):
            cd = cand_ref[h]
            m, pos = _first_max(cd, crow, CAND_ROWS)
            sel = crow == pos
            slot = pl.ds(h * PEER_TOPK + r, 1)
            best_ref[slot, :] = m
            e1_ref[slot, :] = jnp.sum(jnp.where(sel, c1_ref[h], 0.0), axis=0, keepdims=True)
            e2_ref[slot, :] = jnp.sum(jnp.where(sel, c2_ref[h], 0.0), axis=0, keepdims=True)
            cand_ref[h] = jnp.where(sel, -jnp.inf, cd)
        return carry

    lax.fori_loop(0, PEER_TOPK, stage2, 0)

    for h in range(PEER_HEADS):
        sl = slice(h * PEER_TOPK, (h + 1) * PEER_TOPK)
        b = best_ref[sl, :]
        ex = jnp.exp(b - jnp.max(b, axis=0, keepdims=True))
        best_ref[sl, :] = ex / jnp.sum(ex, axis=0, keepdims=True)
    i1_ref[...] = e1_ref[...].T
    i2_ref[...] = e2_ref[...].T
    g_ref[...] = best_ref[...].T


def peer_topk(q, keys):
    n = q.shape[0]
    t = TOPK_TT
    out = jax.ShapeDtypeStruct((n, N_SLOTS), F32)
    spec = pl.BlockSpec((t, N_SLOTS), lambda i: (i, 0))
    return pl.pallas_call(
        _peer_topk_kernel,
        grid=(n // t,),
        in_specs=[pl.BlockSpec((t, D), lambda i: (i, 0)),
                  pl.BlockSpec(keys.shape, lambda i: (0, 0, 0, 0))],
        out_specs=[spec, spec, spec],
        out_shape=[out, out, out],
        scratch_shapes=[pltpu.VMEM((N_HALVES, N_KEYS, t), F32),
                        pltpu.VMEM((N_HALVES, PEER_TOPK, t), F32),
                        pltpu.VMEM((N_HALVES, PEER_TOPK, t), F32),
                        pltpu.VMEM((PEER_HEADS, CAND_ROWS, t), F32),
                        pltpu.VMEM((PEER_HEADS, CAND_ROWS, t), F32),
                        pltpu.VMEM((PEER_HEADS, CAND_ROWS, t), F32),
                        pltpu.VMEM((N_SLOTS, t), F32),
                        pltpu.VMEM((N_SLOTS, t), F32),
                        pltpu.VMEM((N_SLOTS, t), F32)],
        compiler_params=_cparams(("arbitrary",)),
        name="peer_topk",
    )(q, keys)


EXP_TM = 512
EXP_J = 8
EXP_BLK = EXP_J * N_KEYS
N_EXP_BLK = N_KEYS * N_KEYS // EXP_BLK
G_PITCH = N_KEYS // 2 + SUBLANES
SQRT_HALF = 0.7071067811865476


def _peer_expert_kernel(h_ref, i1_ref, i2_ref, g_ref, u_ref, v_ref, o_ref, gs_ref, w_ref):
    jj = pl.program_id(1)

    @pl.when(jj == 0)
    def _():
        o_ref[...] = jnp.zeros_like(o_ref)
        w_ref[...] = jnp.zeros_like(w_ref)
        rows = lax.broadcasted_iota(jnp.int32, (N_KEYS, N_KEYS), 0).astype(F32)

        def tok(t, carry):
            i1r = i1_ref[pl.ds(t, 1), :]
            i2r = i2_ref[pl.ds(t, 1), :]
            gr = g_ref[pl.ds(t, 1), :]
            at = jnp.where(rows == i1r, gr, 0.0).astype(BF16)
            bt = jnp.where(rows == i2r, 1.0, 0.0).astype(BF16)
            gm = _dot_nt(at, bt).astype(BF16)
            start = pl.multiple_of(t * G_PITCH, SUBLANES)
            gs_ref[pl.ds(start, N_KEYS // 2), :] = pltpu.bitcast(gm, jnp.uint32)
            return carry

        lax.fori_loop(0, EXP_TM, tok, 0, unroll=16)

    slot = jj % 2
    o_ref[...] += _dot(w_ref[1 - slot], v_ref[...])

    jb = jnp.minimum(jj, N_EXP_BLK - 1)
    s = _dot_nt(h_ref[...], u_ref[...])
    cols = []
    for c in range(EXP_J // 2):
        packed = gs_ref[pl.ds((EXP_J // 2) * jb + c, EXP_TM, stride=G_PITCH), :]
        cols.append(lax.bitcast_convert_type(packed << 16, F32))
        cols.append(lax.bitcast_convert_type(packed & jnp.uint32(0xFFFF0000), F32))
    g = jnp.concatenate(cols, axis=1)
    act = 0.5 * s * (1.0 + lax.erf(s * SQRT_HALF))
    w_ref[slot] = (g * act).astype(BF16)


def peer_expert(h, i1, i2, g, u, v):
    n = h.shape[0]
    row = lambda i, j: (i, 0)
    sel = pl.BlockSpec((EXP_TM, N_SLOTS), row)
    return pl.pallas_call(
        _peer_expert_kernel,
        grid=(n // EXP_TM, N_EXP_BLK + 1),
        in_specs=[pl.BlockSpec((EXP_TM, D), row), sel, sel, sel,
                  pl.BlockSpec((EXP_BLK, D), lambda i, j: (jnp.minimum(j, N_EXP_BLK - 1), 0)),
                  pl.BlockSpec((EXP_BLK, D), lambda i, j: (jnp.maximum(j - 1, 0), 0))],
        out_specs=pl.BlockSpec((EXP_TM, D), row),
        out_shape=jax.ShapeDtypeStruct((n, D), F32),
        scratch_shapes=[pltpu.VMEM((EXP_TM * G_PITCH, N_KEYS), jnp.uint32),
                        pltpu.VMEM((2, EXP_TM, EXP_BLK), BF16)],
        compiler_params=_cparams(("arbitrary", "arbitrary")),
        name="peer_expert",
    )(h, i1, i2, g, u, v)


def peer_ffn(h, wq, keys, u, v):
    nb = h.shape[0]
    hf = h.reshape(nb * T_ALL, D)
    i1, i2, g = peer_topk(matmul(hf, wq), keys)
    return peer_expert(hf, i1, i2, g, u, v).reshape(nb, T_ALL, D)


def _head_ones(width, group):
    r = np.arange(width) // group
    return jnp.asarray((r[:, None] == r[None, :]).astype(np.float32))


def _store_head_pairs(o_ref, x, y):
    for h in range(RW_H):
        ln = slice(h * HEAD, (h + 1) * HEAD)
        o_ref[pl.ds(h, ROW_BLK, stride=RW_H), :] = jnp.concatenate([x[:, ln], y[:, ln]], axis=1)


def _rw_pre_kernel(z_ref, hp_ref, hn_ref, mu_ref, w0_ref, w2_ref, a0_ref, a2_ref, g2_ref,
                   kk_ref, ka_ref, ones_ref,
                   r_ref, k_ref, v_ref, g_ref, rv_ref, awf_ref, awb_ref, bkf_ref, bkb_ref):
    z = z_ref[0]
    row = lax.broadcasted_iota(jnp.int32, (ROW_BLK, 1), 0)
    prev = jnp.where(row == 0, hp_ref[0, 0], pltpu.roll(z, 1, 0))
    nxt = jnp.where(row == ROW_BLK - 1, hn_ref[0, 0], pltpu.roll(z, ROW_BLK - 1, 0))
    zs = z + mu_ref[0:1, :] * (prev - z) + mu_ref[1:2, :] * (nxt - z)
    r = zs[:, 0:RW_W]
    k = zs[:, RW_W:2 * RW_W]
    v = zs[:, 2 * RW_W:3 * RW_W]
    o = 3 * RW_W
    lw = jnp.tanh(zs[:, o:o + 2 * LORA])
    la = zs[:, o + 2 * LORA:o + 4 * LORA]
    lg = jax.nn.sigmoid(zs[:, o + 4 * LORA:o + 4 * LORA + LORA_G])
    r_ref[0] = r
    k_ref[0] = k
    v_ref[0] = v
    g_ref[0] = _dot(lg, g2_ref[...], precision=HI)
    kkr = k * kk_ref[...]
    ss = _dot(kkr * kkr, ones_ref[...], precision=HI)
    kkn = kkr / jnp.maximum(jnp.sqrt(ss), 1e-12)
    _store_head_pairs(rv_ref, r, v)
    for d, (aw_ref, bk_ref) in enumerate(((awf_ref, bkf_ref), (awb_ref, bkb_ref))):
        dec = w0_ref[d:d + 1, :] + _dot(lw, w2_ref[d], precision=HI)
        a = jax.nn.sigmoid(a0_ref[d:d + 1, :] + _dot(la, a2_ref[d], precision=HI))
        _store_head_pairs(aw_ref, -kkn, jnp.exp(-W_DECAY_SCALE * jax.nn.sigmoid(dec)))
        _store_head_pairs(bk_ref, kkn * a, k * (1.0 + (a - 1.0) * ka_ref[...]))


def rw_pre(z, mu, w0, w2, a0, a2, g2, kk, ka):
    nb = z.shape[0]
    zr = z[:, :, :RW_COLS]
    zero = jnp.zeros((nb, 1, RW_COLS), F32)
    last = zr[:, ROW_BLK - 1::ROW_BLK]
    first = zr[:, ::ROW_BLK]
    halo_prev = jnp.concatenate([zero, zero, last[:, 1:N_ROW_BLK - 1]], axis=1)
    halo_next = jnp.concatenate([zero, first[:, 2:], zero], axis=1)
    halo_prev = halo_prev.reshape(nb, N_ROW_BLK, 1, RW_COLS)
    halo_next = halo_next.reshape(nb, N_ROW_BLK, 1, RW_COLS)
    zpad = jnp.zeros((LORA, RW_W), F32)
    w2p = jnp.stack([jnp.concatenate([w2[0], zpad]), jnp.concatenate([zpad, w2[1]])])
    a2p = jnp.stack([jnp.concatenate([a2[0], zpad]), jnp.concatenate([zpad, a2[1]])])
    row = lambda b, i: (b, i, 0)
    full = lambda shape: pl.BlockSpec(shape, lambda b, i: (0,) * len(shape))
    out = jax.ShapeDtypeStruct((nb, T_ALL, RW_W), F32)
    ospec = pl.BlockSpec((1, ROW_BLK, RW_W), row)
    pout = jax.ShapeDtypeStruct((nb, T_ALL * RW_H, LANES), F32)
    pspec = pl.BlockSpec((None, ROW_BLK * RW_H, LANES), row)
    return pl.pallas_call(
        _rw_pre_kernel,
        grid=(nb, N_ROW_BLK),
        in_specs=[pl.BlockSpec((1, ROW_BLK, RW_COLS), row),
                  pl.BlockSpec((1, 1, 1, RW_COLS), lambda b, i: (b, i, 0, 0)),
                  pl.BlockSpec((1, 1, 1, RW_COLS), lambda b, i: (b, i, 0, 0)),
                  full((2, RW_COLS)), full((2, RW_W)), full((2, 2 * LORA, RW_W)),
                  full((2, RW_W)), full((2, 2 * LORA, RW_W)), full((LORA_G, RW_W)),
                  full((1, RW_W)), full((1, RW_W)), full((RW_W, RW_W))],
        out_specs=[ospec] * 4 + [pspec] * 5,
        out_shape=[out] * 4 + [pout] * 5,
        compiler_params=_cparams(("arbitrary", "arbitrary")),
        name="rw_pre",
    )(z, halo_prev, halo_next, mu, w0, w2p, a0, a2p, g2, kk.reshape(1, RW_W),
      ka.reshape(1, RW_W), _head_ones(RW_W, HEAD))


SCAN_TB = 32
SCAN_I = HEAD // 2


SCAN_NBLK = T_ALL // SCAN_TB
SCAN_CTX_BLK = T_CTX // SCAN_TB


def _rw_scan_kernel(awf_ref, bkf_ref, rvf_ref, awb_ref, bkb_ref, rvb_ref, yf_ref, yb_ref,
                    s_ref, sa_ref, *tiles):
    @pl.when(pl.program_id(0) == 0)
    def _():
        s_ref[...] = jnp.zeros_like(s_ref)

    low_half = lax.broadcasted_iota(jnp.int32, (1, LANES), 1) < LANES // 2
    sets = (tiles[:6], tiles[6:])

    def prep(t, tile_set):
        a_s, w_s, b_s, k_s, r_s, v_s = tile_set
        rf = pl.ds(pl.multiple_of(t * RW_H, RW_H), RW_H)
        rb = pl.ds(pl.multiple_of((SCAN_TB - 1 - t) * RW_H, RW_H), RW_H)

        def transposed(f_ref, b_ref):
            rows = [f_ref[b, rf, :] for b in range(NB)] + [b_ref[b, rb, :] for b in range(NB)]
            return jnp.concatenate(rows + rows, axis=0).T

        aw = transposed(awf_ref, awb_ref)
        a_s[...] = aw[:HEAD]
        w_s[...] = aw[HEAD:]
        bk = transposed(bkf_ref, bkb_ref)
        b_s[...] = bk[:HEAD]
        k_s[...] = bk[HEAD:]
        rv = transposed(rvf_ref, rvb_ref)
        r_s[...] = rv[:HEAD]
        v_s[...] = jnp.where(low_half, rv[HEAD:HEAD + SCAN_I], rv[HEAD + SCAN_I:])

    def step(t, tile_set):
        a_s, w_s, b_s, k_s, r_s, v_s = tile_set
        a = a_s[...]
        for i in range(SCAN_I):
            sa_ref[pl.ds(i, 1), :] = jnp.sum(s_ref[i] * a, axis=0, keepdims=True)
        w = w_s[...]
        b = b_s[...]
        k = k_s[...]
        r = r_s[...]
        for i in range(SCAN_I):
            sn = s_ref[i] * w + sa_ref[pl.ds(i, 1), :] * b + v_s[pl.ds(i, 1), :] * k
            s_ref[i] = sn
            y = jnp.sum(sn * r, axis=0, keepdims=True)
            yf_ref[t, pl.ds(i, 1), :] = y
            yb_ref[SCAN_TB - 1 - t, pl.ds(i, 1), :] = y

    prep(0, sets[0])

    def two_steps(u, carry):
        t = 2 * u
        prep(t + 1, sets[1])
        step(t, sets[0])
        prep(jnp.minimum(t + 2, SCAN_TB - 1), sets[0])
        step(t + 1, sets[1])
        return carry

    lax.fori_loop(0, SCAN_TB // 2, two_steps, 0)


def _mirror_block(g):
    return jnp.where(g < SCAN_CTX_BLK, SCAN_CTX_BLK - 1 - g, SCAN_NBLK + SCAN_CTX_BLK - 1 - g)


def rw_scan(awf, bkf, awb, bkb, rv):
    assert awf.shape[0] == NB
    fspec = pl.BlockSpec((NB, SCAN_TB * RW_H, LANES), lambda g: (0, g, 0))
    bspec = pl.BlockSpec((NB, SCAN_TB * RW_H, LANES), lambda g: (0, _mirror_block(g), 0))
    yf = pl.BlockSpec((SCAN_TB, SCAN_I, LANES), lambda g: (g, 0, 0))
    yb = pl.BlockSpec((SCAN_TB, SCAN_I, LANES), lambda g: (_mirror_block(g), 0, 0))
    out = jax.ShapeDtypeStruct((T_ALL, SCAN_I, LANES), F32)
    tile_set = [pltpu.VMEM((HEAD, LANES), F32)] * 5 + [pltpu.VMEM((SCAN_I, LANES), F32)]
    return pl.pallas_call(
        _rw_scan_kernel,
        grid=(SCAN_NBLK,),
        in_specs=[fspec] * 3 + [bspec] * 3,
        out_specs=[yf, yb],
        out_shape=[out, out],
        scratch_shapes=[pltpu.VMEM((SCAN_I, HEAD, LANES), F32),
                        pltpu.VMEM((SCAN_I, LANES), F32)] + tile_set + tile_set,
        compiler_params=_cparams(("arbitrary",)),
        name="rw_scan",
    )(awf, bkf, rv, awb, bkb, rv)


def _scan_unlayout_i(y, d):
    s = y.reshape(T_ALL, SCAN_I, 2, 2, NB, RW_H)[:, :, :, d]
    return s.transpose(3, 0, 4, 2, 1).reshape(NB, T_ALL, RW_W)


def _rw_post_kernel(yf_ref, yb_ref, r_ref, k_ref, v_ref, g_ref, rk_ref, lw_ref, lb_ref, ones_ref,
                    o_ref):
    y = yf_ref[0] + yb_ref[0]
    ones = ones_ref[...]
    mean = _dot(y, ones, precision=HI) * (1.0 / HEAD)
    yc = y - mean
    var = _dot(yc * yc, ones, precision=HI) * (1.0 / HEAD)
    yn = yc * lax.rsqrt(var + RW_LN_EPS) * lw_ref[...] + lb_ref[...]
    bonus = _dot(r_ref[0] * k_ref[0] * rk_ref[...], ones, precision=HI) * v_ref[0]
    o_ref[0] = ((yn + bonus) * g_ref[0]).astype(BF16)


def rw_post(yf, yb, r, k, v, g, rk, ln_w, ln_b):
    nb = yf.shape[0]
    row = lambda b, i: (b, i, 0)
    spec = pl.BlockSpec((1, ROW_BLK, RW_W), row)
    vec = pl.BlockSpec((1, RW_W), lambda b, i: (0, 0))
    return pl.pallas_call(
        _rw_post_kernel,
        grid=(nb, N_ROW_BLK),
        in_specs=[spec] * 6 + [vec] * 3 + [pl.BlockSpec((RW_W, RW_W), lambda b, i: (0, 0))],
        out_specs=spec,
        out_shape=jax.ShapeDtypeStruct((nb, T_ALL, RW_W), BF16),
        compiler_params=_cparams(("arbitrary", "arbitrary")),
        name="rw_post",
    )(yf, yb, r, k, v, g, rk.reshape(1, RW_W), ln_w.reshape(1, RW_W), ln_b.reshape(1, RW_W),
      _head_ones(RW_W, HEAD))


def rwkv_mixer(z, mu, w0, w2, a0, a2, g2, kk, ka, rk, ln_w, ln_b):
    r, k, v, g, rv, awf, awb, bkf, bkb = rw_pre(z, mu, w0, w2, a0, a2, g2, kk, ka)
    y1, y2 = rw_scan(awf, bkf, awb, bkb, rv)
    return rw_post(_scan_unlayout_i(y1, 0), _scan_unlayout_i(y2, 1), r, k, v, g, rk, ln_w, ln_b)


NA_QROWS = ROW_BLK // GRID_W
NA_SLAB = NA_KH + NA_QROWS - 1
NA_SLAB_T = NA_SLAB * GRID_W
NA_SCALE = HEAD ** -0.5


def _na_bias_index():
    a = np.arange(NA_QROWS)[:, None]
    u = np.arange(NA_SLAB)[None, :]
    idx_r, valid = [], []
    for r0, u0 in ((0, 0), (NA_QROWS, 0), (GRID_ROWS - NA_QROWS, GRID_ROWS - NA_SLAB)):
        r = r0 + a
        kr = u0 + u
        kr0 = np.clip(r - NA_KH // 2, 0, GRID_ROWS - NA_KH)
        valid.append((kr >= kr0) & (kr < kr0 + NA_KH))
        idx_r.append(np.clip(kr - r + NA_KH - 1, 0, 2 * NA_KH - 2))
    idx_r = np.stack(idx_r)
    valid = np.stack(valid)[:, :, None, :, None]
    qc = np.arange(GRID_W)[:, None]
    kc = np.arange(GRID_W)[None, :]
    c0 = np.clip(qc - NA_KW // 2, 0, GRID_W - NA_KW)
    in_win = ((kc >= c0) & (kc < c0 + NA_KW))[None, None, :, None, :]
    idx_c = np.clip(kc - qc + NA_KW - 1, 0, 2 * NA_KW - 2)
    col_onehot = (idx_c[None] == np.arange(2 * NA_KW - 1)[:, None, None]).astype(np.float32)
    shape = (3, NA_QROWS, GRID_W, NA_SLAB, GRID_W)
    return idx_r, col_onehot, np.broadcast_to(valid & in_win, shape)


def na_bias_tables(rpb):
    idx_r, col_onehot, mask = _na_bias_index()
    rows = rpb[:, idx_r.reshape(-1), :].reshape(NA_H, 3, NA_QROWS, NA_SLAB, 2 * NA_KW - 1)
    b = jnp.einsum('hvauc,cqk->hvaquk', rows, jnp.asarray(col_onehot), precision=HI)
    b = jnp.where(mask[None], b, NEG_INF)
    return b.reshape(NA_H, 3, ROW_BLK, NA_SLAB_T)


def _na_kernel(q_ref, k_ref, v_ref, bias_ref, o_ref):
    qi = pl.program_id(2)

    @pl.when(qi == 0)
    def _():
        ys = []
        for hh in range(2):
            ln = slice(hh * HEAD, (hh + 1) * HEAD)
            q = (q_ref[0, :, ln] * NA_SCALE).astype(BF16)
            s = _dot_nt(q, k_ref[0, 0:T_CTX, ln].astype(BF16))
            p = jnp.exp(s - jnp.max(s, axis=-1, keepdims=True))
            y = _dot(p.astype(BF16), v_ref[0, 0:T_CTX, ln].astype(BF16))
            ys.append(y / jnp.sum(p, axis=-1, keepdims=True))
        o_ref[0] = jnp.concatenate(ys, axis=1).astype(BF16)

    @pl.when(qi > 0)
    def _():
        u0 = jnp.clip(NA_QROWS * (qi - 1) - NA_KH // 2, 0, GRID_ROWS - NA_SLAB)
        start = pl.multiple_of(T_CTX + GRID_W * u0, GRID_W)
        ys = []
        for hh in range(2):
            ln = slice(hh * HEAD, (hh + 1) * HEAD)
            q = (q_ref[0, :, ln] * NA_SCALE).astype(BF16)
            sc = _dot_nt(q, k_ref[0, 0:T_CTX, ln].astype(BF16))
            sw = _dot_nt(q, k_ref[0, pl.ds(start, NA_SLAB_T), ln].astype(BF16)) + bias_ref[hh, 0]
            m = jnp.maximum(jnp.max(sc, axis=-1, keepdims=True), jnp.max(sw, axis=-1, keepdims=True))
            pc = jnp.exp(sc - m)
            pw = jnp.exp(sw - m)
            y = (_dot(pw.astype(BF16), v_ref[0, pl.ds(start, NA_SLAB_T), ln].astype(BF16))
                 + _dot(pc.astype(BF16), v_ref[0, 0:T_CTX, ln].astype(BF16)))
            den = jnp.sum(pc, axis=-1, keepdims=True) + jnp.sum(pw, axis=-1, keepdims=True)
            ys.append(y / den)
        o_ref[0] = jnp.concatenate(ys, axis=1).astype(BF16)


def natten_mixer(z, bias):
    nb = z.shape[0]
    qb, kb, vb = (NA_OFF // LANES, (NA_OFF + NA_W) // LANES, (NA_OFF + 2 * NA_W) // LANES)
    n_blk = N_ROW_BLK - 1

    def bias_idx(b, hp, qi):
        var = jnp.where(qi <= 1, 0, jnp.where(qi == n_blk, 2, 1))
        return (hp, var, 0, 0)

    return pl.pallas_call(
        _na_kernel,
        grid=(nb, NA_H // 2, N_ROW_BLK),
        in_specs=[pl.BlockSpec((1, ROW_BLK, LANES), lambda b, hp, qi: (b, qi, qb + hp)),
                  pl.BlockSpec((1, T_ALL, LANES), lambda b, hp, qi: (b, 0, kb + hp)),
                  pl.BlockSpec((1, T_ALL, LANES), lambda b, hp, qi: (b, 0, vb + hp)),
                  pl.BlockSpec((2, 1, ROW_BLK, NA_SLAB_T), bias_idx)],
        out_specs=pl.BlockSpec((1, ROW_BLK, LANES), lambda b, hp, qi: (b, qi, hp)),
        out_shape=jax.ShapeDtypeStruct((nb, T_ALL, NA_W), BF16),
        compiler_params=_cparams(("arbitrary", "arbitrary", "arbitrary")),
        name="natten",
    )(z, z, z, bias)


ROPE_NF = MB_N // 4


def rope_tables():
    pos = np.arange(T_LAT)
    inv = ROPE_BASE ** (-np.arange(ROPE_NF, dtype=np.float32) / ROPE_NF)
    lane = np.arange(MB_N)
    p = np.where(lane[None, :] < MB_N // 2, (pos // GRID_W)[:, None], (pos % GRID_W)[:, None])
    ang = p.astype(np.float32) * inv[lane % ROPE_NF][None, :]
    sign = np.where((lane % (2 * ROPE_NF)) < ROPE_NF, -1.0, 1.0)[None, :]
    cos = np.concatenate([np.ones((T_CTX, MB_N), np.float32), np.cos(ang)])
    sin = np.concatenate([np.zeros((T_CTX, MB_N), np.float32), np.sin(ang) * sign])
    return jnp.asarray(cos, F32), jnp.asarray(sin, F32)


def _mb_pre_kernel(z_ref, w_ref, b_ref, cos_ref, sin_ref, o_ref):
    z = z_ref[0]
    pos = lax.broadcasted_iota(jnp.int32, (T_ALL, 1), 0)
    lo = jnp.where(pos < T_CTX, 0, T_CTX)
    hi = jnp.where(pos < T_CTX, T_CTX, T_ALL)
    half = MB_CONV // 2
    acc = z * w_ref[half:half + 1, :] + b_ref[...]
    for d in (-2, -1, 1, 2):
        zr = pltpu.roll(z, (-d) % T_ALL, 0)
        ok = (pos + d >= lo) & (pos + d < hi)
        acc = acc + jnp.where(ok, zr, 0.0) * w_ref[half + d:half + d + 1, :]
    y = acc * jax.nn.sigmoid(acc)
    is_bc = pl.program_id(1) >= MB_INNER // LANES

    @pl.when(jnp.logical_not(is_bc))
    def _():
        o_ref[0] = y

    @pl.when(is_bc)
    def _():
        lane = lax.broadcasted_iota(jnp.int32, (1, MB_N), 1)
        first = (lane % (2 * ROPE_NF)) < ROPE_NF
        partner = jnp.where(first, pltpu.roll(y, MB_N - ROPE_NF, 1), pltpu.roll(y, ROPE_NF, 1))
        o_ref[0] = y * cos_ref[...] + partner * sin_ref[...]


def mb_pre(z, conv_w, conv_b, cos, sin):
    nb = z.shape[0]
    c0 = XBC_OFF // LANES
    return pl.pallas_call(
        _mb_pre_kernel,
        grid=(nb, MB_CONV_CH // LANES),
        in_specs=[pl.BlockSpec((1, T_ALL, LANES), lambda b, j: (b, 0, c0 + j)),
                  pl.BlockSpec((MB_CONV, LANES), lambda b, j: (0, j)),
                  pl.BlockSpec((1, LANES), lambda b, j: (0, j)),
                  pl.BlockSpec((T_ALL, MB_N), lambda b, j: (0, 0)),
                  pl.BlockSpec((T_ALL, MB_N), lambda b, j: (0, 0))],
        out_specs=pl.BlockSpec((1, T_ALL, LANES), lambda b, j: (b, 0, j)),
        out_shape=jax.ShapeDtypeStruct((nb, T_ALL, MB_CONV_CH), F32),
        compiler_params=_cparams(("arbitrary", "arbitrary")),
        name="mb_pre",
    )(z, conv_w, conv_b.reshape(1, MB_CONV_CH), cos, sin)


N_CHUNK = T_ALL // MB_CHUNK
N_CTX_CHUNK = T_CTX // MB_CHUNK
XB_BLK = MB_INNER // LANES
XC_BLK = XB_BLK + MB_G


def _ssd_direction(reverse, x_ref, b_ref, c_ref, dtw_ref, dtr_ref, bw_ref, br_ref, aw_ref, ar_ref,
                   st_ref, y_ref):
    L = MB_CHUNK
    dt_head = jax.nn.softplus(dtw_ref[0, 0] + bw_ref[0])
    da_head = dt_head * (-jnp.exp(aw_ref[0]))
    dtr = jax.nn.softplus(dtr_ref[0, 0, 0] + br_ref[0, 0])
    dar = dtr * (-jnp.exp(ar_ref[0, 0]))
    ri = lax.broadcasted_iota(jnp.int32, (L, L), 0)
    ci = lax.broadcasted_iota(jnp.int32, (L, L), 1)
    mask = (ri <= ci) if reverse else (ri >= ci)
    tri_r = (ri >= ci).astype(F32) if reverse else (ri <= ci).astype(F32)
    tot_row = 0 if reverse else L - 1
    cum_head = da_head
    trow = lax.broadcasted_iota(jnp.int32, (L, 1), 0)
    step = 1
    while step < L:
        if reverse:
            shifted = jnp.where(trow < L - step, pltpu.roll(cum_head, L - step, 0), 0.0)
        else:
            shifted = jnp.where(trow >= step, pltpu.roll(cum_head, step, 0), 0.0)
        cum_head = cum_head + shifted
        step *= 2
    cum_r = _dot(dar, tri_r, precision=HI)
    tot = cum_head[tot_row:tot_row + 1, :]
    bm = b_ref[0]
    cm = c_ref[0].astype(BF16)
    cb = _dot_nt(cm, bm.astype(BF16))
    bt = bm.T.astype(BF16)
    xc = x_ref[0] * dt_head
    xdec = (xc * jnp.exp(tot - cum_head)).astype(BF16)
    ecum = jnp.exp(cum_head)
    etot = jnp.exp(tot)
    first = lax.broadcasted_iota(jnp.int32, (1, LANES), 1) < HEAD
    ys = []
    for p in range(MB_R // 2):
        sl = slice(p * LANES, (p + 1) * LANES)
        xp = xc[:, sl].astype(BF16)
        per_head = []
        for r in (2 * p, 2 * p + 1):
            diff = cum_head[:, r * HEAD:r * HEAD + 1] - cum_r[r:r + 1, :]
            lmat = jnp.exp(jnp.where(mask, diff, -jnp.inf))
            per_head.append(_dot((cb * lmat).astype(BF16), xp))
        y_diag = jnp.where(first, per_head[0], per_head[1])
        st = st_ref[p]
        y_off = _dot(cm, st.astype(BF16)) * ecum[:, sl]
        st_ref[p] = etot[:, sl] * st + _dot(bt, xdec[:, sl])
        ys.append(y_diag + y_off)
    y_ref[0] = jnp.concatenate(ys, axis=1)


def _mb_ssd_kernel(*refs):
    fwd, bwd = refs[0:9], refs[9:18]
    yf_ref, yb_ref, stf_ref, stb_ref = refs[18:22]

    @pl.when(pl.program_id(2) == 0)
    def _():
        stf_ref[...] = jnp.zeros_like(stf_ref)
        stb_ref[...] = jnp.zeros_like(stb_ref)

    _ssd_direction(False, *fwd, stf_ref, yf_ref)
    _ssd_direction(True, *bwd, stb_ref, yb_ref)


def _bwd_chunk(i):
    return jnp.where(i < N_CTX_CHUNK, N_CTX_CHUNK - 1 - i, N_CHUNK + N_CTX_CHUNK - 1 - i)


def mb_ssd(xbc, dt_raw, dt_bias, a_log):
    nb = xbc.shape[0]
    dt4 = dt_raw.reshape(nb, T_ALL, 2, MB_H).transpose(0, 2, 1, 3)
    dtw = jnp.repeat(dt4, HEAD, axis=-1)
    dtr = dt4.reshape(nb, 2, T_ALL, MB_G, MB_R).transpose(0, 1, 3, 4, 2)
    bw = jnp.repeat(dt_bias, HEAD, axis=-1).reshape(2, 1, MB_INNER)
    aw = jnp.repeat(a_log, HEAD, axis=-1).reshape(2, 1, MB_INNER)
    b4 = dt_bias.reshape(2, MB_G, MB_R, 1)
    a4 = a_log.reshape(2, MB_G, MB_R, 1)
    ins, specs = [], []
    for d, cidx in ((0, lambda i: i), (1, _bwd_chunk)):
        ins += [xbc, xbc, xbc, dtw, dtr, bw, b4, aw, a4]
        specs += [
            pl.BlockSpec((1, MB_CHUNK, MB_R * HEAD), lambda b, g, i, c=cidx: (b, c(i), g)),
            pl.BlockSpec((1, MB_CHUNK, MB_N), lambda b, g, i, c=cidx: (b, c(i), XB_BLK + g)),
            pl.BlockSpec((1, MB_CHUNK, MB_N), lambda b, g, i, c=cidx: (b, c(i), XC_BLK + g)),
            pl.BlockSpec((1, 1, MB_CHUNK, MB_R * HEAD), lambda b, g, i, c=cidx, d=d: (b, d, c(i), g)),
            pl.BlockSpec((1, 1, 1, MB_R, MB_CHUNK), lambda b, g, i, c=cidx, d=d: (b, d, g, 0, c(i))),
            pl.BlockSpec((1, 1, MB_R * HEAD), lambda b, g, i, d=d: (d, 0, g)),
            pl.BlockSpec((1, 1, MB_R, 1), lambda b, g, i, d=d: (d, g, 0, 0)),
            pl.BlockSpec((1, 1, MB_R * HEAD), lambda b, g, i, d=d: (d, 0, g)),
            pl.BlockSpec((1, 1, MB_R, 1), lambda b, g, i, d=d: (d, g, 0, 0)),
        ]
    out = jax.ShapeDtypeStruct((nb, T_ALL, MB_INNER), F32)
    return pl.pallas_call(
        _mb_ssd_kernel,
        grid=(nb, MB_G, N_CHUNK),
        in_specs=specs,
        out_specs=[pl.BlockSpec((1, MB_CHUNK, MB_R * HEAD), lambda b, g, i: (b, i, g)),
                   pl.BlockSpec((1, MB_CHUNK, MB_R * HEAD), lambda b, g, i: (b, _bwd_chunk(i), g))],
        out_shape=[out, out],
        scratch_shapes=[pltpu.VMEM((MB_R // 2, MB_N, 2 * HEAD), F32),
                        pltpu.VMEM((MB_R // 2, MB_N, 2 * HEAD), F32)],
        compiler_params=_cparams(("arbitrary", "arbitrary", "arbitrary")),
        name="mb_ssd",
    )(*ins)


MB_GW = MB_INNER // MB_G


def _mb_post_kernel(yf_ref, yb_ref, x_ref, glo_ref, ghi_ref, d_ref, nw_ref, o_ref):
    y = yf_ref[0] + yb_ref[0] + x_ref[0] * d_ref[...]
    gate = jnp.concatenate([glo_ref[0], ghi_ref[0]], axis=1)
    y = y * (gate * jax.nn.sigmoid(gate))
    ms = jnp.mean(y * y, axis=-1, keepdims=True)
    o_ref[0] = (y * lax.rsqrt(ms + NORM_EPS) * nw_ref[...]).astype(BF16)


def mb_post(yf, yb, xbc, z, d_skip, norm_w):
    nb = yf.shape[0]
    g0 = MB_OFF // LANES
    grp = lambda b, i, g: (b, i, g)
    return pl.pallas_call(
        _mb_post_kernel,
        grid=(nb, N_ROW_BLK, MB_G),
        in_specs=[pl.BlockSpec((1, ROW_BLK, MB_GW), grp),
                  pl.BlockSpec((1, ROW_BLK, MB_GW), grp),
                  pl.BlockSpec((1, ROW_BLK, MB_GW), grp),
                  pl.BlockSpec((1, ROW_BLK, LANES), lambda b, i, g: (b, i, g0 + 2 * g)),
                  pl.BlockSpec((1, ROW_BLK, LANES), lambda b, i, g: (b, i, g0 + 2 * g + 1)),
                  pl.BlockSpec((1, MB_GW), lambda b, i, g: (0, g)),
                  pl.BlockSpec((1, MB_GW), lambda b, i, g: (0, g))],
        out_specs=pl.BlockSpec((1, ROW_BLK, MB_GW), grp),
        out_shape=jax.ShapeDtypeStruct((nb, T_ALL, MB_INNER), BF16),
        compiler_params=_cparams(("arbitrary", "arbitrary", "arbitrary")),
        name="mb_post",
    )(yf, yb, xbc, z, z, jnp.repeat(d_skip, HEAD).reshape(1, MB_INNER), norm_w.reshape(1, MB_INNER))


def mamba_mixer(z, conv_w, conv_b, dt_bias, a_log, d_skip, norm_w, cos, sin):
    xbc = mb_pre(z, conv_w, conv_b, cos, sin)
    yf, yb = mb_ssd(xbc, z[:, :, DT_OFF:DT_OFF + 2 * MB_H], dt_bias, a_log)
    return mb_post(yf, yb, xbc, z, d_skip, norm_w)


def kernel(x, c, ctx, c_ctx, ada_w, ada_b, norm1_w, norm2_w, w_in, w_out, rw_mu, rw_w0, rw_w2, rw_a0, rw_a2, rw_g2, rw_kk, rw_ka, rw_rk, rw_ln_w, rw_ln_b, na_rpb, mb_conv_w, mb_conv_b, mb_dt_bias, mb_a_log, mb_d, mb_norm_w, pe_wq, pe_keys, pe_u, pe_v, final_norm_w):
    nb = x.shape[0]
    xs = jnp.concatenate([ctx, x], axis=1)
    cond = jnp.zeros((SUBLANES, D), F32).at[:nb].set(c).at[nb].set(c_ctx)
    mods = ada_rows(cond, ada_w, ada_b)
    cos, sin = rope_tables()
    f = prev_tab = None
    for l in range(DEPTH):
        m = mods[l].reshape(SUBLANES, 6, D)
        tab = jnp.stack([jnp.broadcast_to(m[nb], (nb, 6, D)), m[:nb]], axis=1).reshape(2 * nb, 6, D)
        if f is None:
            h1 = norm_mod(xs, norm1_w[l].reshape(1, D), tab, 0)
        else:
            xs, h1 = resid_norm_mod(xs, f, prev_tab, norm1_w[l].reshape(1, D), tab)
        z = matmul(h1.reshape(nb * T_ALL, D), cast_bf16(w_in[l], ROW_BLK)).reshape(nb, T_ALL, IN_COLS)
        y_rw = rwkv_mixer(z, rw_mu[l], rw_w0[l], rw_w2[l], rw_a0[l], rw_a2[l], rw_g2[l], rw_kk[l],
                          rw_ka[l], rw_rk[l].reshape(RW_W), rw_ln_w[l], rw_ln_b[l])
        y_na = natten_mixer(z, na_bias_tables(na_rpb[l]))
        y_mb = mamba_mixer(z, mb_conv_w[l], mb_conv_b[l], mb_dt_bias[l], mb_a_log[l], mb_d[l],
                           mb_norm_w[l], cos, sin)
        xs = out_proj(xs, y_rw, y_na, y_mb, cast_bf16(w_out[l], ROW_BLK), tab)
        h2 = norm_mod(xs, norm2_w[l].reshape(1, D), tab, 3)
        f = peer_ffn(h2, cast_bf16(pe_wq[l], ROW_BLK), pe_keys[l],
                     cast_bf16(pe_u[l], 1024), cast_bf16(pe_v[l], 1024))
        prev_tab = tab
    return final_norm(xs, f, prev_tab, final_norm_w.reshape(1, D))
```

```python
import functools
import math

import numpy as np
import jax
import jax.numpy as jnp
from jax import lax
from jax.experimental import pallas as pl
from jax.experimental.pallas import tpu as pltpu

D = 2048
NB = 4
T_LAT = 2048
T_CTX = 256
T_ALL = T_CTX + T_LAT
DEPTH = 4
GRID_W = 64
GRID_ROWS = T_LAT // GRID_W
HEAD = 64
RW_W = 512
RW_H = 8
LORA = 64
LORA_G = 128
W_DECAY_SCALE = 0.606531
RW_LN_EPS = 64e-5
NA_W = 512
NA_H = 8
NA_KH = 8
NA_KW = 16
MB_INNER = 1024
MB_H = 16
MB_G = 4
MB_R = 4
MB_N = 128
MB_CONV = 5
MB_CHUNK = 128
ROPE_BASE = 10000.0
PEER_HEADS = 8
N_KEYS = 128
PEER_TOPK = 16
NORM_EPS = 1e-6
NEG_INF = -1e30
RW_COLS = 3 * RW_W + 4 * LORA + LORA_G
NA_COLS = 3 * NA_W
MB_CONV_CH = MB_INNER + 2 * MB_G * MB_N
MB_COLS = MB_INNER + MB_CONV_CH + 2 * MB_H
IN_COLS = RW_COLS + NA_COLS + MB_COLS
NA_OFF = RW_COLS
MB_OFF = RW_COLS + NA_COLS
XBC_OFF = MB_OFF + MB_INNER
DT_OFF = XBC_OFF + MB_CONV_CH

LANES = 128
SUBLANES = 8
VMEM_LIMIT = 56 * 1024 * 1024

ROW_BLK = 256
N_ROW_BLK = T_ALL // ROW_BLK
BF16 = jnp.bfloat16
F32 = jnp.float32
HI = lax.Precision.HIGHEST


def _cparams(sem):
    return pltpu.CompilerParams(dimension_semantics=sem, vmem_limit_bytes=VMEM_LIMIT)


def _dot(a, b, precision=None):
    return jnp.dot(a, b, preferred_element_type=F32, precision=precision)


def _split_bf16(x):
    hi = x.astype(BF16)
    return hi, (x - hi.astype(F32)).astype(BF16)


def _dot_split(a, b):
    ah, al = _split_bf16(a)
    bh, bl = _split_bf16(b)
    return _dot(ah, bh) + _dot(al, bh) + _dot(ah, bl)


def _dot_exact_rhs(a, b_bf16):
    ah, al = _split_bf16(a)
    return _dot(ah, b_bf16) + _dot(al, b_bf16)


def _dot_nt(a, b, precision=None):
    return lax.dot_general(a, b, (((1,), (1,)), ((), ())), preferred_element_type=F32,
                           precision=precision)


def _cast_kernel(x_ref, o_ref):
    o_ref[...] = x_ref[...].astype(BF16)


def cast_bf16(w, layer, rows_blk):
    _, r, c = w.shape
    return pl.pallas_call(
        _cast_kernel,
        grid=(r // rows_blk,),
        in_specs=[pl.BlockSpec((None, rows_blk, c), lambda i: (layer, i, 0))],
        out_specs=pl.BlockSpec((rows_blk, c), lambda i: (i, 0)),
        out_shape=jax.ShapeDtypeStruct((r, c), BF16),
        compiler_params=_cparams(("arbitrary",)),
        name="cast_bf16",
    )(w)


ADA_TN = 1024


def _ada_kernel(c_ref, w_ref, b_ref, o_ref):
    c = c_ref[...]
    s = c * jax.nn.sigmoid(c)
    o_ref[0] = _dot(s, w_ref[0], precision=HI) + b_ref[0]


def ada_rows(cond, ada_w, ada_b):
    nl = ada_w.shape[0]
    return pl.pallas_call(
        _ada_kernel,
        grid=(nl, 6 * D // ADA_TN),
        in_specs=[pl.BlockSpec((SUBLANES, D), lambda l, j: (0, 0)),
                  pl.BlockSpec((1, D, ADA_TN), lambda l, j: (l, 0, j)),
                  pl.BlockSpec((1, 1, ADA_TN), lambda l, j: (l, 0, j))],
        out_specs=pl.BlockSpec((1, SUBLANES, ADA_TN), lambda l, j: (l, 0, j)),
        out_shape=jax.ShapeDtypeStruct((nl, SUBLANES, 6 * D), F32),
        compiler_params=_cparams(("arbitrary", "arbitrary")),
        name="ada_rows",
    )(cond, ada_w, ada_b.reshape(nl, 1, 6 * D))


def _mod_index(b, i):
    return 2 * b + jnp.minimum(i, 1)


def _norm_mod_kernel(which, x_ref, nw_ref, tab_ref, o_ref):
    x = x_ref[0]
    ms = jnp.mean(x * x, axis=-1, keepdims=True)
    y = x * lax.rsqrt(ms + NORM_EPS) * nw_ref[...]
    shift = tab_ref[0, which:which + 1, :]
    scale = tab_ref[0, which + 1:which + 2, :]
    o_ref[0] = (y * (1.0 + scale) + shift).astype(BF16)


def norm_mod(x, nw, tab, which):
    nb = x.shape[0]
    return pl.pallas_call(
        functools.partial(_norm_mod_kernel, which),
        grid=(nb, N_ROW_BLK),
        in_specs=[pl.BlockSpec((1, ROW_BLK, D), lambda b, i: (b, i, 0)),
                  pl.BlockSpec((1, D), lambda b, i: (0, 0)),
                  pl.BlockSpec((1, 6, D), lambda b, i: (_mod_index(b, i), 0, 0))],
        out_specs=pl.BlockSpec((1, ROW_BLK, D), lambda b, i: (b, i, 0)),
        out_shape=jax.ShapeDtypeStruct(x.shape, BF16),
        compiler_params=_cparams(("arbitrary", "arbitrary")),
        name="norm_mod",
    )(x, nw, tab)


def _resid_norm_mod_kernel(x_ref, f_ref, ptab_ref, nw_ref, tab_ref, xo_ref, o_ref):
    x = x_ref[0] + ptab_ref[0, 5:6, :] * f_ref[0]
    xo_ref[0] = x
    ms = jnp.mean(x * x, axis=-1, keepdims=True)
    y = x * lax.rsqrt(ms + NORM_EPS) * nw_ref[...]
    o_ref[0] = (y * (1.0 + tab_ref[0, 1:2, :]) + tab_ref[0, 0:1, :]).astype(BF16)


def resid_norm_mod(x, f, prev_tab, nw, tab):
    nb = x.shape[0]
    row = lambda b, i: (b, i, 0)
    mod = lambda b, i: (_mod_index(b, i), 0, 0)
    return pl.pallas_call(
        _resid_norm_mod_kernel,
        grid=(nb, N_ROW_BLK),
        in_specs=[pl.BlockSpec((1, ROW_BLK, D), row),
                  pl.BlockSpec((1, ROW_BLK, D), row),
                  pl.BlockSpec((1, 6, D), mod),
                  pl.BlockSpec((1, D), lambda b, i: (0, 0)),
                  pl.BlockSpec((1, 6, D), mod)],
        out_specs=[pl.BlockSpec((1, ROW_BLK, D), row), pl.BlockSpec((1, ROW_BLK, D), row)],
        out_shape=[jax.ShapeDtypeStruct(x.shape, F32), jax.ShapeDtypeStruct(x.shape, BF16)],
        input_output_aliases={0: 0},
        compiler_params=_cparams(("arbitrary", "arbitrary")),
        name="resid_norm_mod",
    )(x, f, prev_tab, nw, tab)


def _final_norm_kernel(x_ref, f_ref, ptab_ref, nw_ref, o_ref):
    x = x_ref[0] + ptab_ref[0, 5:6, :] * f_ref[0]
    ms = jnp.mean(x * x, axis=-1, keepdims=True)
    o_ref[0] = x * lax.rsqrt(ms + NORM_EPS) * nw_ref[...]


def final_norm(x, f, prev_tab, nw):
    nb = x.shape[0]
    lat = lambda b, i: (b, i + 1, 0)
    return pl.pallas_call(
        _final_norm_kernel,
        grid=(nb, T_LAT // ROW_BLK),
        in_specs=[pl.BlockSpec((1, ROW_BLK, D), lat),
                  pl.BlockSpec((1, ROW_BLK, D), lat),
                  pl.BlockSpec((1, 6, D), lambda b, i: (2 * b + 1, 0, 0)),
                  pl.BlockSpec((1, D), lambda b, i: (0, 0))],
        out_specs=pl.BlockSpec((1, ROW_BLK, D), lambda b, i: (b, i, 0)),
        out_shape=jax.ShapeDtypeStruct((nb, T_LAT, D), F32),
        compiler_params=_cparams(("arbitrary", "arbitrary")),
        name="final_norm",
    )(x, f, prev_tab, nw)


MM_TM = 1024
MM_TN = 512


def _mm_kernel(a_ref, w_ref, o_ref):
    o_ref[...] = _dot(a_ref[...], w_ref[...])


def matmul(a, w):
    m, k = a.shape
    n = w.shape[1]
    tm = MM_TM if m % MM_TM == 0 else ROW_BLK
    return pl.pallas_call(
        _mm_kernel,
        grid=(m // tm, pl.cdiv(n, MM_TN)),
        in_specs=[pl.BlockSpec((tm, k), lambda i, j: (i, 0)),
                  pl.BlockSpec((k, MM_TN), lambda i, j: (0, j))],
        out_specs=pl.BlockSpec((tm, MM_TN), lambda i, j: (i, j)),
        out_shape=jax.ShapeDtypeStruct((m, n), F32),
        compiler_params=_cparams(("arbitrary", "arbitrary")),
        name="matmul",
    )(a, w)


def _out_proj_kernel(x_ref, rw_ref, na_ref, mb_ref, w_ref, tab_ref, o_ref):
    acc = _dot(rw_ref[0], w_ref[0:RW_W, :])
    acc += _dot(na_ref[0], w_ref[RW_W:RW_W + NA_W, :])
    acc += _dot(mb_ref[0], w_ref[RW_W + NA_W:, :])
    o_ref[0] = x_ref[0] + tab_ref[0, 2:3, :] * acc


def out_proj(x, y_rw, y_na, y_mb, w, tab):
    nb = x.shape[0]
    row = lambda b, i: (b, i, 0)
    return pl.pallas_call(
        _out_proj_kernel,
        grid=(nb, N_ROW_BLK),
        in_specs=[pl.BlockSpec((1, ROW_BLK, D), row),
                  pl.BlockSpec((1, ROW_BLK, RW_W), row),
                  pl.BlockSpec((1, ROW_BLK, NA_W), row),
                  pl.BlockSpec((1, ROW_BLK, MB_INNER), row),
                  pl.BlockSpec((D, D), lambda b, i: (0, 0)),
                  pl.BlockSpec((1, 6, D), lambda b, i: (_mod_index(b, i), 0, 0))],
        out_specs=pl.BlockSpec((1, ROW_BLK, D), row),
        out_shape=jax.ShapeDtypeStruct(x.shape, F32),
        input_output_aliases={0: 0},
        compiler_params=_cparams(("arbitrary", "arbitrary")),
        name="out_proj",
    )(x, y_rw, y_na, y_mb, w, tab)


TOPK_TT = 128


CAND_COUNTS = (16, 8, 5, 4, 3, 2, 2, 2)
CAND_ROWS = 16 + 8 * 7 + 8
N_HALVES = 2 * PEER_HEADS
N_SLOTS = PEER_HEADS * PEER_TOPK


def _first_max(s, rows, n):
    m = jnp.max(s, axis=0, keepdims=True)
    pos = jnp.min(jnp.where(s == m, rows, float(n)), axis=0, keepdims=True)
    return m, pos


def _peer_topk_kernel(q_ref, keys_ref, i1_ref, i2_ref, g_ref,
                      sc_ref, val_ref, idx_ref, cand_ref, c1_ref, c2_ref, best_ref, e1_ref, e2_ref):
    t = TOPK_TT
    for hp in range(N_HALVES):
        c0 = hp * N_KEYS
        qs = q_ref[:, c0:c0 + N_KEYS].astype(BF16)
        sc_ref[hp] = _dot_nt(keys_ref[hp // 2, hp % 2].astype(BF16), qs)

    rows = lax.broadcasted_iota(jnp.int32, (N_KEYS, t), 0).astype(F32)

    def stage1(r, carry):
        for hp in range(N_HALVES):
            s = sc_ref[hp]
            m, pos = _first_max(s, rows, N_KEYS)
            val_ref[hp, pl.ds(r, 1), :] = m
            idx_ref[hp, pl.ds(r, 1), :] = pos
            sc_ref[hp] = jnp.where(rows == pos, -jnp.inf, s)
        return carry

    lax.fori_loop(0, PEER_TOPK, stage1, 0)

    row8 = lax.broadcasted_iota(jnp.int32, (SUBLANES, t), 0)
    for h in range(PEER_HEADS):
        v1, v2 = val_ref[2 * h], val_ref[2 * h + 1]
        k1, k2 = idx_ref[2 * h], idx_ref[2 * h + 1]
        cand = [v1[0:1] + v2]
        c1 = [jnp.broadcast_to(k1[0:1], (PEER_TOPK, t))]
        c2 = [k2]
        for r1 in range(1, SUBLANES):
            cand.append(jnp.where(row8 < CAND_COUNTS[r1], v1[r1:r1 + 1] + v2[0:SUBLANES], -jnp.inf))
            c1.append(jnp.broadcast_to(k1[r1:r1 + 1], (SUBLANES, t)))
            c2.append(k2[0:SUBLANES])
        cand.append(v1[SUBLANES:] + v2[0:1])
        c1.append(k1[SUBLANES:])
        c2.append(jnp.broadcast_to(k2[0:1], (SUBLANES, t)))
        cand_ref[h] = jnp.concatenate(cand, axis=0)
        c1_ref[h] = jnp.concatenate(c1, axis=0)
        c2_ref[h] = jnp.concatenate(c2, axis=0)

    crow = lax.broadcasted_iota(jnp.int32, (CAND_ROWS, t), 0).astype(F32)

    def stage2(r, carry):
        for h in range(PEER_HEADS):
            cd = cand_ref[h]
            m, pos = _first_max(cd, crow, CAND_ROWS)
            sel = crow == pos
            slot = pl.ds(h * PEER_TOPK + r, 1)
            best_ref[slot, :] = m
            e1_ref[slot, :] = jnp.sum(jnp.where(sel, c1_ref[h], 0.0), axis=0, keepdims=True)
            e2_ref[slot, :] = jnp.sum(jnp.where(sel, c2_ref[h], 0.0), axis=0, keepdims=True)
            cand_ref[h] = jnp.where(sel, -jnp.inf, cd)
        return carry

    lax.fori_loop(0, PEER_TOPK, stage2, 0)

    for h in range(PEER_HEADS):
        sl = slice(h * PEER_TOPK, (h + 1) * PEER_TOPK)
        b = best_ref[sl, :]
        ex = jnp.exp(b - jnp.max(b, axis=0, keepdims=True))
        best_ref[sl, :] = ex / jnp.sum(ex, axis=0, keepdims=True)
    i1_ref[...] = e1_ref[...].T
    i2_ref[...] = e2_ref[...].T
    g_ref[...] = best_ref[...].T


def peer_topk(q, keys):
    n = q.shape[0]
    t = TOPK_TT
    out = jax.ShapeDtypeStruct((n, N_SLOTS), F32)
    spec = pl.BlockSpec((t, N_SLOTS), lambda i: (i, 0))
    return pl.pallas_call(
        _peer_topk_kernel,
        grid=(n // t,),
        in_specs=[pl.BlockSpec((t, D), lambda i: (i, 0)),
                  pl.BlockSpec(keys.shape, lambda i: (0, 0, 0, 0))],
        out_specs=[spec, spec, spec],
        out_shape=[out, out, out],
        scratch_shapes=[pltpu.VMEM((N_HALVES, N_KEYS, t), F32),
                        pltpu.VMEM((N_HALVES, PEER_TOPK, t), F32),
                        pltpu.VMEM((N_HALVES, PEER_TOPK, t), F32),
                        pltpu.VMEM((PEER_HEADS, CAND_ROWS, t), F32),
                        pltpu.VMEM((PEER_HEADS, CAND_ROWS, t), F32),
                        pltpu.VMEM((PEER_HEADS, CAND_ROWS, t), F32),
                        pltpu.VMEM((N_SLOTS, t), F32),
                        pltpu.VMEM((N_SLOTS, t), F32),
                        pltpu.VMEM((N_SLOTS, t), F32)],
        compiler_params=_cparams(("arbitrary",)),
        name="peer_topk",
    )(q, keys)


EXP_TM = 512
EXP_J = 8
EXP_BLK = EXP_J * N_KEYS
N_EXP_BLK = N_KEYS * N_KEYS // EXP_BLK
G_PITCH = N_KEYS // 2 + SUBLANES
SQRT_HALF = 0.7071067811865476


def _peer_expert_kernel(h_ref, i1_ref, i2_ref, g_ref, u_ref, v_ref, o_ref, gs_ref, w_ref):
    jj = pl.program_id(1)

    @pl.when(jj == 0)
    def _():
        o_ref[...] = jnp.zeros_like(o_ref)
        w_ref[...] = jnp.zeros_like(w_ref)
        rows = lax.broadcasted_iota(jnp.int32, (N_KEYS, N_KEYS), 0).astype(F32)

        def tok(t, carry):
            i1r = i1_ref[pl.ds(t, 1), :]
            i2r = i2_ref[pl.ds(t, 1), :]
            gr = g_ref[pl.ds(t, 1), :]
            at = jnp.where(rows == i1r, gr, 0.0).astype(BF16)
            bt = jnp.where(rows == i2r, 1.0, 0.0).astype(BF16)
            gm = _dot_nt(at, bt).astype(BF16)
            start = pl.multiple_of(t * G_PITCH, SUBLANES)
            gs_ref[pl.ds(start, N_KEYS // 2), :] = pltpu.bitcast(gm, jnp.uint32)
            return carry

        lax.fori_loop(0, EXP_TM, tok, 0, unroll=16)

    slot = jj % 2
    o_ref[...] += _dot(w_ref[1 - slot], v_ref[...])

    jb = jnp.minimum(jj, N_EXP_BLK - 1)
    s = _dot_nt(h_ref[...], u_ref[...])
    cols = []
    for c in range(EXP_J // 2):
        packed = gs_ref[pl.ds((EXP_J // 2) * jb + c, EXP_TM, stride=G_PITCH), :]
        cols.append(lax.bitcast_convert_type(packed << 16, F32))
        cols.append(lax.bitcast_convert_type(packed & jnp.uint32(0xFFFF0000), F32))
    g = jnp.concatenate(cols, axis=1)
    act = 0.5 * s * (1.0 + lax.erf(s * SQRT_HALF))
    w_ref[slot] = (g * act).astype(BF16)


def peer_expert(h, i1, i2, g, u, v):
    n = h.shape[0]
    row = lambda i, j: (i, 0)
    sel = pl.BlockSpec((EXP_TM, N_SLOTS), row)
    return pl.pallas_call(
        _peer_expert_kernel,
        grid=(n // EXP_TM, N_EXP_BLK + 1),
        in_specs=[pl.BlockSpec((EXP_TM, D), row), sel, sel, sel,
                  pl.BlockSpec((EXP_BLK, D), lambda i, j: (jnp.minimum(j, N_EXP_BLK - 1), 0)),
                  pl.BlockSpec((EXP_BLK, D), lambda i, j: (jnp.maximum(j - 1, 0), 0))],
        out_specs=pl.BlockSpec((EXP_TM, D), row),
        out_shape=jax.ShapeDtypeStruct((n, D), F32),
        scratch_shapes=[pltpu.VMEM((EXP_TM * G_PITCH, N_KEYS), jnp.uint32),
                        pltpu.VMEM((2, EXP_TM, EXP_BLK), BF16)],
        compiler_params=_cparams(("arbitrary", "arbitrary")),
        name="peer_expert",
    )(h, i1, i2, g, u, v)


def peer_ffn(h, wq, keys, u, v):
    nb = h.shape[0]
    hf = h.reshape(nb * T_ALL, D)
    i1, i2, g = peer_topk(matmul(hf, wq), keys)
    return peer_expert(hf, i1, i2, g, u, v).reshape(nb, T_ALL, D)


def _head_ones(width, group):
    r = np.arange(width) // group
    return jnp.asarray((r[:, None] == r[None, :]).astype(np.float32)).astype(BF16)


def _store_head_pairs(o_ref, x, y):
    for h in range(RW_H):
        ln = slice(h * HEAD, (h + 1) * HEAD)
        o_ref[pl.ds(h, ROW_BLK, stride=RW_H), :] = jnp.concatenate([x[:, ln], y[:, ln]], axis=1)


def _rw_pre_kernel(z_ref, hp_ref, hn_ref, mu_ref, w0_ref, w2_ref, a0_ref, a2_ref, g2_ref,
                   kk_ref, ka_ref, ones_ref,
                   r_ref, k_ref, v_ref, g_ref, rv_ref, awf_ref, awb_ref, bkf_ref, bkb_ref):
    z = z_ref[0]
    row = lax.broadcasted_iota(jnp.int32, (ROW_BLK, 1), 0)
    prev = jnp.where(row == 0, hp_ref[0, 0], pltpu.roll(z, 1, 0))
    nxt = jnp.where(row == ROW_BLK - 1, hn_ref[0, 0], pltpu.roll(z, ROW_BLK - 1, 0))
    zs = z + mu_ref[0:1, :] * (prev - z) + mu_ref[1:2, :] * (nxt - z)
    r = zs[:, 0:RW_W]
    k = zs[:, RW_W:2 * RW_W]
    v = zs[:, 2 * RW_W:3 * RW_W]
    o = 3 * RW_W
    lw = jnp.tanh(zs[:, o:o + 2 * LORA])
    la = zs[:, o + 2 * LORA:o + 4 * LORA]
    lg = jax.nn.sigmoid(zs[:, o + 4 * LORA:o + 4 * LORA + LORA_G])
    r_ref[0] = r
    k_ref[0] = k
    v_ref[0] = v
    g_ref[0] = _dot_split(lg, g2_ref[...])
    kkr = k * kk_ref[...]
    ss = _dot_exact_rhs(kkr * kkr, ones_ref[...])
    kkn = kkr / jnp.maximum(jnp.sqrt(ss), 1e-12)
    _store_head_pairs(rv_ref, r, v)
    for d, (aw_ref, bk_ref) in enumerate(((awf_ref, bkf_ref), (awb_ref, bkb_ref))):
        dec = w0_ref[d:d + 1, :] + _dot_split(lw, w2_ref[d])
        a = jax.nn.sigmoid(a0_ref[d:d + 1, :] + _dot_split(la, a2_ref[d]))
        _store_head_pairs(aw_ref, -kkn, jnp.exp(-W_DECAY_SCALE * jax.nn.sigmoid(dec)))
        _store_head_pairs(bk_ref, kkn * a, k * (1.0 + (a - 1.0) * ka_ref[...]))


def rw_pre(z, mu, w0, w2, a0, a2, g2, kk, ka):
    nb = z.shape[0]
    zero = jnp.zeros((nb, 1, RW_COLS), F32)
    last = z[:, ROW_BLK - 1::ROW_BLK, :RW_COLS]
    first = z[:, ::ROW_BLK, :RW_COLS]
    halo_prev = jnp.concatenate([zero, zero, last[:, 1:N_ROW_BLK - 1]], axis=1)
    halo_next = jnp.concatenate([zero, first[:, 2:], zero], axis=1)
    halo_prev = halo_prev.reshape(nb, N_ROW_BLK, 1, RW_COLS)
    halo_next = halo_next.reshape(nb, N_ROW_BLK, 1, RW_COLS)
    zpad = jnp.zeros((LORA, RW_W), F32)
    w2p = jnp.stack([jnp.concatenate([w2[0], zpad]), jnp.concatenate([zpad, w2[1]])])
    a2p = jnp.stack([jnp.concatenate([a2[0], zpad]), jnp.concatenate([zpad, a2[1]])])
    row = lambda b, i: (b, i, 0)
    full = lambda shape: pl.BlockSpec(shape, lambda b, i: (0,) * len(shape))
    out = jax.ShapeDtypeStruct((nb, T_ALL, RW_W), F32)
    ospec = pl.BlockSpec((1, ROW_BLK, RW_W), row)
    pout = jax.ShapeDtypeStruct((nb, T_ALL * RW_H, LANES), F32)
    pspec = pl.BlockSpec((None, ROW_BLK * RW_H, LANES), row)
    return pl.pallas_call(
        _rw_pre_kernel,
        grid=(nb, N_ROW_BLK),
        in_specs=[pl.BlockSpec((1, ROW_BLK, RW_COLS), row),
                  pl.BlockSpec((1, 1, 1, RW_COLS), lambda b, i: (b, i, 0, 0)),
                  pl.BlockSpec((1, 1, 1, RW_COLS), lambda b, i: (b, i, 0, 0)),
                  full((2, RW_COLS)), full((2, RW_W)), full((2, 2 * LORA, RW_W)),
                  full((2, RW_W)), full((2, 2 * LORA, RW_W)), full((LORA_G, RW_W)),
                  full((1, RW_W)), full((1, RW_W)), full((RW_W, RW_W))],
        out_specs=[ospec] * 4 + [pspec] * 5,
        out_shape=[out] * 4 + [pout] * 5,
        compiler_params=_cparams(("arbitrary", "arbitrary")),
        name="rw_pre",
    )(z, halo_prev, halo_next, mu, w0, w2p, a0, a2p, g2, kk.reshape(1, RW_W),
      ka.reshape(1, RW_W), _head_ones(RW_W, HEAD))


SCAN_TB = 32
SCAN_I = HEAD // 2


SCAN_NBLK = T_ALL // SCAN_TB
SCAN_CTX_BLK = T_CTX // SCAN_TB


def _rw_scan_kernel(awf_ref, bkf_ref, rvf_ref, awb_ref, bkb_ref, rvb_ref, yf_ref, yb_ref,
                    s_ref, sa_ref, gam_ref, *tiles):
    @pl.when(pl.program_id(0) == 0)
    def _():
        s_ref[...] = jnp.zeros_like(s_ref)

    low_half = lax.broadcasted_iota(jnp.int32, (1, LANES), 1) < LANES // 2
    sub = lax.broadcasted_iota(jnp.int32, (SUBLANES, LANES), 0)
    sets = (tiles[:5], tiles[5:])

    def prep(t, tile_set):
        a_s, b_s, k_s, r_s, v_s = tile_set
        rf = pl.ds(pl.multiple_of(t * RW_H, RW_H), RW_H)
        rb = pl.ds(pl.multiple_of((SCAN_TB - 1 - t) * RW_H, RW_H), RW_H)

        def transposed(f_ref, b_ref):
            rows = [f_ref[b, rf, :] for b in range(NB)] + [b_ref[b, rb, :] for b in range(NB)]
            return jnp.concatenate(rows + rows, axis=0).T

        aw = transposed(awf_ref, awb_ref)
        gam_prev = gam_ref[...]
        gam = gam_prev * aw[HEAD:]
        gam_ref[...] = gam
        inv = 1.0 / gam
        a_s[...] = aw[:HEAD] * gam_prev
        bk = transposed(bkf_ref, bkb_ref)
        b_s[...] = bk[:HEAD] * inv
        k_s[...] = bk[HEAD:] * inv
        rv = transposed(rvf_ref, rvb_ref)
        r_s[...] = rv[:HEAD] * gam
        v_s[...] = jnp.where(low_half, rv[HEAD:HEAD + SCAN_I], rv[HEAD + SCAN_I:])

    def fold(x):
        return jnp.sum(x.reshape(SUBLANES, SUBLANES, LANES), axis=0)

    def reduce8(p):
        z = [jnp.where(sub < 4, p[k], p[k + 4]) + pltpu.roll(jnp.where(sub < 4, p[k + 4], p[k]), 4, 0)
             for k in range(4)]
        even2 = (sub & 2) == 0
        v = [jnp.where(even2, z[k] + pltpu.roll(z[k], 6, 0), z[k + 2] + pltpu.roll(z[k + 2], 2, 0))
             for k in range(2)]
        return jnp.where((sub & 1) == 0, v[0] + pltpu.roll(v[0], 7, 0), v[1] + pltpu.roll(v[1], 1, 0))

    def step(t, tile_set):
        a_s, b_s, k_s, r_s, v_s = tile_set
        a = a_s[...]
        for c in range(SCAN_I // SUBLANES):
            rows = range(c * SUBLANES, (c + 1) * SUBLANES)
            sa_ref[c * SUBLANES:(c + 1) * SUBLANES, :] = reduce8([fold(s_ref[i] * a) for i in rows])
        b = b_s[...]
        k = k_s[...]
        r = r_s[...]
        for c in range(SCAN_I // SUBLANES):
            parts = []
            for i in range(c * SUBLANES, (c + 1) * SUBLANES):
                sn = s_ref[i] + sa_ref[pl.ds(i, 1), :] * b + v_s[pl.ds(i, 1), :] * k
                s_ref[i] = sn
                parts.append(fold(sn * r))
            y = reduce8(parts)
            yf_ref[t, c * SUBLANES:(c + 1) * SUBLANES, :] = y
            yb_ref[SCAN_TB - 1 - t, c * SUBLANES:(c + 1) * SUBLANES, :] = y

    gam_ref[...] = jnp.ones_like(gam_ref)
    prep(0, sets[0])

    def two_steps(u, carry):
        t = 2 * u
        prep(t + 1, sets[1])
        step(t, sets[0])
        prep(t + 2, sets[0])
        step(t + 1, sets[1])
        return carry

    lax.fori_loop(0, SCAN_TB // 2 - 1, two_steps, 0)
    prep(SCAN_TB - 1, sets[1])
    step(SCAN_TB - 2, sets[0])
    step(SCAN_TB - 1, sets[1])

    gam_end = gam_ref[...]
    for i in range(SCAN_I):
        s_ref[i] = s_ref[i] * gam_end


def _mirror_block(g):
    return jnp.where(g < SCAN_CTX_BLK, SCAN_CTX_BLK - 1 - g, SCAN_NBLK + SCAN_CTX_BLK - 1 - g)


def rw_scan(awf, bkf, awb, bkb, rv):
    assert awf.shape[0] == NB
    fspec = pl.BlockSpec((NB, SCAN_TB * RW_H, LANES), lambda g: (0, g, 0))
    bspec = pl.BlockSpec((NB, SCAN_TB * RW_H, LANES), lambda g: (0, _mirror_block(g), 0))
    yf = pl.BlockSpec((SCAN_TB, SCAN_I, LANES), lambda g: (g, 0, 0))
    yb = pl.BlockSpec((SCAN_TB, SCAN_I, LANES), lambda g: (_mirror_block(g), 0, 0))
    out = jax.ShapeDtypeStruct((T_ALL, SCAN_I, LANES), F32)
    tile_set = [pltpu.VMEM((HEAD, LANES), F32)] * 4 + [pltpu.VMEM((SCAN_I, LANES), F32)]
    return pl.pallas_call(
        _rw_scan_kernel,
        grid=(SCAN_NBLK,),
        in_specs=[fspec] * 3 + [bspec] * 3,
        out_specs=[yf, yb],
        out_shape=[out, out],
        scratch_shapes=[pltpu.VMEM((SCAN_I, HEAD, LANES), F32),
                        pltpu.VMEM((SCAN_I, LANES), F32),
                        pltpu.VMEM((HEAD, LANES), F32)] + tile_set + tile_set,
        compiler_params=_cparams(("arbitrary",)),
        name="rw_scan",
    )(awf, bkf, rv, awb, bkb, rv)


def _scan_unlayout_i(y, d):
    s = y.reshape(T_ALL, SCAN_I, 2, 2, NB, RW_H)[:, :, :, d]
    return s.transpose(3, 0, 4, 2, 1).reshape(NB, T_ALL, RW_W)


def _rw_post_kernel(yf_ref, yb_ref, r_ref, k_ref, v_ref, g_ref, rk_ref, lw_ref, lb_ref, ones_ref,
                    o_ref):
    y = yf_ref[0] + yb_ref[0]
    ones = ones_ref[...]
    mean = _dot_exact_rhs(y, ones) * (1.0 / HEAD)
    yc = y - mean
    var = _dot_exact_rhs(yc * yc, ones) * (1.0 / HEAD)
    yn = yc * lax.rsqrt(var + RW_LN_EPS) * lw_ref[...] + lb_ref[...]
    bonus = _dot_exact_rhs(r_ref[0] * k_ref[0] * rk_ref[...], ones) * v_ref[0]
    o_ref[0] = ((yn + bonus) * g_ref[0]).astype(BF16)


def rw_post(yf, yb, r, k, v, g, rk, ln_w, ln_b):
    nb = yf.shape[0]
    row = lambda b, i: (b, i, 0)
    spec = pl.BlockSpec((1, ROW_BLK, RW_W), row)
    vec = pl.BlockSpec((1, RW_W), lambda b, i: (0, 0))
    return pl.pallas_call(
        _rw_post_kernel,
        grid=(nb, N_ROW_BLK),
        in_specs=[spec] * 6 + [vec] * 3 + [pl.BlockSpec((RW_W, RW_W), lambda b, i: (0, 0))],
        out_specs=spec,
        out_shape=jax.ShapeDtypeStruct((nb, T_ALL, RW_W), BF16),
        compiler_params=_cparams(("arbitrary", "arbitrary")),
        name="rw_post",
    )(yf, yb, r, k, v, g, rk.reshape(1, RW_W), ln_w.reshape(1, RW_W), ln_b.reshape(1, RW_W),
      _head_ones(RW_W, HEAD))


def rwkv_mixer(z, mu, w0, w2, a0, a2, g2, kk, ka, rk, ln_w, ln_b):
    r, k, v, g, rv, awf, awb, bkf, bkb = rw_pre(z, mu, w0, w2, a0, a2, g2, kk, ka)
    y1, y2 = rw_scan(awf, bkf, awb, bkb, rv)
    return rw_post(_scan_unlayout_i(y1, 0), _scan_unlayout_i(y2, 1), r, k, v, g, rk, ln_w, ln_b)


NA_QROWS = ROW_BLK // GRID_W
NA_SLAB = NA_KH + NA_QROWS - 1
NA_SLAB_T = NA_SLAB * GRID_W
NA_SCALE = HEAD ** -0.5


def _na_bias_index():
    a = np.arange(NA_QROWS)[:, None]
    u = np.arange(NA_SLAB)[None, :]
    idx_r, valid = [], []
    for r0, u0 in ((0, 0), (NA_QROWS, 0), (GRID_ROWS - NA_QROWS, GRID_ROWS - NA_SLAB)):
        r = r0 + a
        kr = u0 + u
        kr0 = np.clip(r - NA_KH // 2, 0, GRID_ROWS - NA_KH)
        valid.append((kr >= kr0) & (kr < kr0 + NA_KH))
        idx_r.append(np.clip(kr - r + NA_KH - 1, 0, 2 * NA_KH - 2))
    idx_r = np.stack(idx_r)
    valid = np.stack(valid)[:, :, None, :, None]
    qc = np.arange(GRID_W)[:, None]
    kc = np.arange(GRID_W)[None, :]
    c0 = np.clip(qc - NA_KW // 2, 0, GRID_W - NA_KW)
    in_win = ((kc >= c0) & (kc < c0 + NA_KW))[None, None, :, None, :]
    idx_c = np.clip(kc - qc + NA_KW - 1, 0, 2 * NA_KW - 2)
    col_onehot = (idx_c[None] == np.arange(2 * NA_KW - 1)[:, None, None]).astype(np.float32)
    shape = (3, NA_QROWS, GRID_W, NA_SLAB, GRID_W)
    return idx_r, col_onehot, np.broadcast_to(valid & in_win, shape)


def na_bias_tables(rpb):
    idx_r, col_onehot, mask = _na_bias_index()
    rows = rpb[:, idx_r.reshape(-1), :].reshape(NA_H, 3, NA_QROWS, NA_SLAB, 2 * NA_KW - 1)
    b = jnp.einsum('hvauc,cqk->hvaquk', rows, jnp.asarray(col_onehot), precision=HI)
    b = jnp.where(mask[None], b, NEG_INF)
    return b.reshape(NA_H, 3, ROW_BLK, NA_SLAB_T)


def _na_kernel(q_ref, k_ref, v_ref, bias_ref, o_ref):
    qi = pl.program_id(2)

    @pl.when(qi == 0)
    def _():
        ys = []
        for hh in range(2):
            ln = slice(hh * HEAD, (hh + 1) * HEAD)
            q = (q_ref[0, :, ln] * NA_SCALE).astype(BF16)
            s = _dot_nt(q, k_ref[0, 0:T_CTX, ln].astype(BF16))
            p = jnp.exp(s - jnp.max(s, axis=-1, keepdims=True))
            y = _dot(p.astype(BF16), v_ref[0, 0:T_CTX, ln].astype(BF16))
            ys.append(y / jnp.sum(p, axis=-1, keepdims=True))
        o_ref[0] = jnp.concatenate(ys, axis=1).astype(BF16)

    @pl.when(qi > 0)
    def _():
        u0 = jnp.clip(NA_QROWS * (qi - 1) - NA_KH // 2, 0, GRID_ROWS - NA_SLAB)
        start = pl.multiple_of(T_CTX + GRID_W * u0, GRID_W)
        ys = []
        for hh in range(2):
            ln = slice(hh * HEAD, (hh + 1) * HEAD)
            q = (q_ref[0, :, ln] * NA_SCALE).astype(BF16)
            sc = _dot_nt(q, k_ref[0, 0:T_CTX, ln].astype(BF16))
            sw = _dot_nt(q, k_ref[0, pl.ds(start, NA_SLAB_T), ln].astype(BF16)) + bias_ref[hh, 0]
            m = jnp.maximum(jnp.max(sc, axis=-1, keepdims=True), jnp.max(sw, axis=-1, keepdims=True))
            pc = jnp.exp(sc - m)
            pw = jnp.exp(sw - m)
            y = (_dot(pw.astype(BF16), v_ref[0, pl.ds(start, NA_SLAB_T), ln].astype(BF16))
                 + _dot(pc.astype(BF16), v_ref[0, 0:T_CTX, ln].astype(BF16)))
            den = jnp.sum(pc, axis=-1, keepdims=True) + jnp.sum(pw, axis=-1, keepdims=True)
            ys.append(y / den)
        o_ref[0] = jnp.concatenate(ys, axis=1).astype(BF16)


def natten_mixer(z, bias):
    nb = z.shape[0]
    qb, kb, vb = (NA_OFF // LANES, (NA_OFF + NA_W) // LANES, (NA_OFF + 2 * NA_W) // LANES)
    n_blk = N_ROW_BLK - 1

    def bias_idx(b, hp, qi):
        var = jnp.where(qi <= 1, 0, jnp.where(qi == n_blk, 2, 1))
        return (hp, var, 0, 0)

    return pl.pallas_call(
        _na_kernel,
        grid=(nb, NA_H // 2, N_ROW_BLK),
        in_specs=[pl.BlockSpec((1, ROW_BLK, LANES), lambda b, hp, qi: (b, qi, qb + hp)),
                  pl.BlockSpec((1, T_ALL, LANES), lambda b, hp, qi: (b, 0, kb + hp)),
                  pl.BlockSpec((1, T_ALL, LANES), lambda b, hp, qi: (b, 0, vb + hp)),
                  pl.BlockSpec((2, 1, ROW_BLK, NA_SLAB_T), bias_idx)],
        out_specs=pl.BlockSpec((1, ROW_BLK, LANES), lambda b, hp, qi: (b, qi, hp)),
        out_shape=jax.ShapeDtypeStruct((nb, T_ALL, NA_W), BF16),
        compiler_params=_cparams(("arbitrary", "arbitrary", "arbitrary")),
        name="natten",
    )(z, z, z, bias)


ROPE_NF = MB_N // 4


def rope_tables():
    pos = np.arange(T_LAT)
    inv = ROPE_BASE ** (-np.arange(ROPE_NF, dtype=np.float32) / ROPE_NF)
    lane = np.arange(MB_N)
    p = np.where(lane[None, :] < MB_N // 2, (pos // GRID_W)[:, None], (pos % GRID_W)[:, None])
    ang = p.astype(np.float32) * inv[lane % ROPE_NF][None, :]
    sign = np.where((lane % (2 * ROPE_NF)) < ROPE_NF, -1.0, 1.0)[None, :]
    cos = np.concatenate([np.ones((T_CTX, MB_N), np.float32), np.cos(ang)])
    sin = np.concatenate([np.zeros((T_CTX, MB_N), np.float32), np.sin(ang) * sign])
    return jnp.asarray(cos, F32), jnp.asarray(sin, F32)


def _mb_pre_kernel(z_ref, w_ref, b_ref, cos_ref, sin_ref, o_ref):
    z = z_ref[0]
    pos = lax.broadcasted_iota(jnp.int32, (T_ALL, 1), 0)
    lo = jnp.where(pos < T_CTX, 0, T_CTX)
    hi = jnp.where(pos < T_CTX, T_CTX, T_ALL)
    half = MB_CONV // 2
    acc = z * w_ref[half:half + 1, :] + b_ref[...]
    for d in (-2, -1, 1, 2):
        zr = pltpu.roll(z, (-d) % T_ALL, 0)
        ok = (pos + d >= lo) & (pos + d < hi)
        acc = acc + jnp.where(ok, zr, 0.0) * w_ref[half + d:half + d + 1, :]
    y = acc * jax.nn.sigmoid(acc)
    is_bc = pl.program_id(1) >= MB_INNER // LANES

    @pl.when(jnp.logical_not(is_bc))
    def _():
        o_ref[0] = y

    @pl.when(is_bc)
    def _():
        lane = lax.broadcasted_iota(jnp.int32, (1, MB_N), 1)
        first = (lane % (2 * ROPE_NF)) < ROPE_NF
        partner = jnp.where(first, pltpu.roll(y, MB_N - ROPE_NF, 1), pltpu.roll(y, ROPE_NF, 1))
        o_ref[0] = y * cos_ref[...] + partner * sin_ref[...]


def mb_pre(z, conv_w, conv_b, cos, sin):
    nb = z.shape[0]
    c0 = XBC_OFF // LANES
    return pl.pallas_call(
        _mb_pre_kernel,
        grid=(nb, MB_CONV_CH // LANES),
        in_specs=[pl.BlockSpec((1, T_ALL, LANES), lambda b, j: (b, 0, c0 + j)),
                  pl.BlockSpec((MB_CONV, LANES), lambda b, j: (0, j)),
                  pl.BlockSpec((1, LANES), lambda b, j: (0, j)),
                  pl.BlockSpec((T_ALL, MB_N), lambda b, j: (0, 0)),
                  pl.BlockSpec((T_ALL, MB_N), lambda b, j: (0, 0))],
        out_specs=pl.BlockSpec((1, T_ALL, LANES), lambda b, j: (b, 0, j)),
        out_shape=jax.ShapeDtypeStruct((nb, T_ALL, MB_CONV_CH), F32),
        compiler_params=_cparams(("arbitrary", "arbitrary")),
        name="mb_pre",
    )(z, conv_w, conv_b.reshape(1, MB_CONV_CH), cos, sin)


N_CHUNK = T_ALL // MB_CHUNK
N_CTX_CHUNK = T_CTX // MB_CHUNK
XB_BLK = MB_INNER // LANES
XC_BLK = XB_BLK + MB_G


def _ssd_direction(reverse, x_ref, b_ref, c_ref, dtw_ref, dtr_ref, bw_ref, br_ref, aw_ref, ar_ref,
                   st_ref, y_ref):
    L = MB_CHUNK
    dt_head = jax.nn.softplus(dtw_ref[0, 0] + bw_ref[0])
    da_head = dt_head * (-jnp.exp(aw_ref[0]))
    dtr = jax.nn.softplus(dtr_ref[0, 0, 0] + br_ref[0, 0])
    dar = dtr * (-jnp.exp(ar_ref[0, 0]))
    ri = lax.broadcasted_iota(jnp.int32, (L, L), 0)
    ci = lax.broadcasted_iota(jnp.int32, (L, L), 1)
    mask = (ri <= ci) if reverse else (ri >= ci)
    tri_r = (ri >= ci).astype(F32) if reverse else (ri <= ci).astype(F32)
    tot_row = 0 if reverse else L - 1
    cum_head = da_head
    trow = lax.broadcasted_iota(jnp.int32, (L, 1), 0)
    step = 1
    while step < L:
        if reverse:
            shifted = jnp.where(trow < L - step, pltpu.roll(cum_head, L - step, 0), 0.0)
        else:
            shifted = jnp.where(trow >= step, pltpu.roll(cum_head, step, 0), 0.0)
        cum_head = cum_head + shifted
        step *= 2
    cum_r = _dot(dar, tri_r, precision=HI)
    tot = cum_head[tot_row:tot_row + 1, :]
    bm = b_ref[0]
    cm = c_ref[0].astype(BF16)
    cb = _dot_nt(cm, bm.astype(BF16))
    bt = bm.T.astype(BF16)
    xc = x_ref[0] * dt_head
    xdec = (xc * jnp.exp(tot - cum_head)).astype(BF16)
    ecum = jnp.exp(cum_head)
    etot = jnp.exp(tot)
    first = lax.broadcasted_iota(jnp.int32, (1, LANES), 1) < HEAD
    ys = []
    for p in range(MB_R // 2):
        sl = slice(p * LANES, (p + 1) * LANES)
        xp = xc[:, sl].astype(BF16)
        per_head = []
        for r in (2 * p, 2 * p + 1):
            diff = cum_head[:, r * HEAD:r * HEAD + 1] - cum_r[r:r + 1, :]
            lmat = jnp.exp(jnp.where(mask, diff, -jnp.inf))
            per_head.append(_dot((cb * lmat).astype(BF16), xp))
        y_diag = jnp.where(first, per_head[0], per_head[1])
        st = st_ref[p]
        y_off = _dot(cm, st.astype(BF16)) * ecum[:, sl]
        st_ref[p] = etot[:, sl] * st + _dot(bt, xdec[:, sl])
        ys.append(y_diag + y_off)
    y_ref[0] = jnp.concatenate(ys, axis=1)


def _mb_ssd_kernel(*refs):
    fwd, bwd = refs[0:9], refs[9:18]
    yf_ref, yb_ref, stf_ref, stb_ref = refs[18:22]

    @pl.when(pl.program_id(2) == 0)
    def _():
        stf_ref[...] = jnp.zeros_like(stf_ref)
        stb_ref[...] = jnp.zeros_like(stb_ref)

    _ssd_direction(False, *fwd, stf_ref, yf_ref)
    _ssd_direction(True, *bwd, stb_ref, yb_ref)


def _bwd_chunk(i):
    return jnp.where(i < N_CTX_CHUNK, N_CTX_CHUNK - 1 - i, N_CHUNK + N_CTX_CHUNK - 1 - i)


def mb_ssd(xbc, dt_raw, dt_bias, a_log):
    nb = xbc.shape[0]
    dt4 = dt_raw.reshape(nb, T_ALL, 2, MB_H).transpose(0, 2, 1, 3)
    dtw = jnp.repeat(dt4, HEAD, axis=-1)
    dtr = dt4.reshape(nb, 2, T_ALL, MB_G, MB_R).transpose(0, 1, 3, 4, 2)
    bw = jnp.repeat(dt_bias, HEAD, axis=-1).reshape(2, 1, MB_INNER)
    aw = jnp.repeat(a_log, HEAD, axis=-1).reshape(2, 1, MB_INNER)
    b4 = dt_bias.reshape(2, MB_G, MB_R, 1)
    a4 = a_log.reshape(2, MB_G, MB_R, 1)
    ins, specs = [], []
    for d, cidx in ((0, lambda i: i), (1, _bwd_chunk)):
        ins += [xbc, xbc, xbc, dtw, dtr, bw, b4, aw, a4]
        specs += [
            pl.BlockSpec((1, MB_CHUNK, MB_R * HEAD), lambda b, g, i, c=cidx: (b, c(i), g)),
            pl.BlockSpec((1, MB_CHUNK, MB_N), lambda b, g, i, c=cidx: (b, c(i), XB_BLK + g)),
            pl.BlockSpec((1, MB_CHUNK, MB_N), lambda b, g, i, c=cidx: (b, c(i), XC_BLK + g)),
            pl.BlockSpec((1, 1, MB_CHUNK, MB_R * HEAD), lambda b, g, i, c=cidx, d=d: (b, d, c(i), g)),
            pl.BlockSpec((1, 1, 1, MB_R, MB_CHUNK), lambda b, g, i, c=cidx, d=d: (b, d, g, 0, c(i))),
            pl.BlockSpec((1, 1, MB_R * HEAD), lambda b, g, i, d=d: (d, 0, g)),
            pl.BlockSpec((1, 1, MB_R, 1), lambda b, g, i, d=d: (d, g, 0, 0)),
            pl.BlockSpec((1, 1, MB_R * HEAD), lambda b, g, i, d=d: (d, 0, g)),
            pl.BlockSpec((1, 1, MB_R, 1), lambda b, g, i, d=d: (d, g, 0, 0)),
        ]
    out = jax.ShapeDtypeStruct((nb, T_ALL, MB_INNER), F32)
    return pl.pallas_call(
        _mb_ssd_kernel,
        grid=(nb, MB_G, N_CHUNK),
        in_specs=specs,
        out_specs=[pl.BlockSpec((1, MB_CHUNK, MB_R * HEAD), lambda b, g, i: (b, i, g)),
                   pl.BlockSpec((1, MB_CHUNK, MB_R * HEAD), lambda b, g, i: (b, _bwd_chunk(i), g))],
        out_shape=[out, out],
        scratch_shapes=[pltpu.VMEM((MB_R // 2, MB_N, 2 * HEAD), F32),
                        pltpu.VMEM((MB_R // 2, MB_N, 2 * HEAD), F32)],
        compiler_params=_cparams(("arbitrary", "arbitrary", "arbitrary")),
        name="mb_ssd",
    )(*ins)


MB_GW = MB_INNER // MB_G


def _mb_post_kernel(yf_ref, yb_ref, x_ref, glo_ref, ghi_ref, d_ref, nw_ref, o_ref):
    y = yf_ref[0] + yb_ref[0] + x_ref[0] * d_ref[...]
    gate = jnp.concatenate([glo_ref[0], ghi_ref[0]], axis=1)
    y = y * (gate * jax.nn.sigmoid(gate))
    ms = jnp.mean(y * y, axis=-1, keepdims=True)
    o_ref[0] = (y * lax.rsqrt(ms + NORM_EPS) * nw_ref[...]).astype(BF16)


def mb_post(yf, yb, xbc, z, d_skip, norm_w):
    nb = yf.shape[0]
    g0 = MB_OFF // LANES
    grp = lambda b, i, g: (b, i, g)
    return pl.pallas_call(
        _mb_post_kernel,
        grid=(nb, N_ROW_BLK, MB_G),
        in_specs=[pl.BlockSpec((1, ROW_BLK, MB_GW), grp),
                  pl.BlockSpec((1, ROW_BLK, MB_GW), grp),
                  pl.BlockSpec((1, ROW_BLK, MB_GW), grp),
                  pl.BlockSpec((1, ROW_BLK, LANES), lambda b, i, g: (b, i, g0 + 2 * g)),
                  pl.BlockSpec((1, ROW_BLK, LANES), lambda b, i, g: (b, i, g0 + 2 * g + 1)),
                  pl.BlockSpec((1, MB_GW), lambda b, i, g: (0, g)),
                  pl.BlockSpec((1, MB_GW), lambda b, i, g: (0, g))],
        out_specs=pl.BlockSpec((1, ROW_BLK, MB_GW), grp),
        out_shape=jax.ShapeDtypeStruct((nb, T_ALL, MB_INNER), BF16),
        compiler_params=_cparams(("arbitrary", "arbitrary", "arbitrary")),
        name="mb_post",
    )(yf, yb, xbc, z, z, jnp.repeat(d_skip, HEAD).reshape(1, MB_INNER), norm_w.reshape(1, MB_INNER))


def mamba_mixer(z, conv_w, conv_b, dt_bias, a_log, d_skip, norm_w, cos, sin):
    xbc = mb_pre(z, conv_w, conv_b, cos, sin)
    yf, yb = mb_ssd(xbc, z[:, :, DT_OFF:DT_OFF + 2 * MB_H], dt_bias, a_log)
    return mb_post(yf, yb, xbc, z, d_skip, norm_w)


def kernel(x, c, ctx, c_ctx, ada_w, ada_b, norm1_w, norm2_w, w_in, w_out, rw_mu, rw_w0, rw_w2, rw_a0, rw_a2, rw_g2, rw_kk, rw_ka, rw_rk, rw_ln_w, rw_ln_b, na_rpb, mb_conv_w, mb_conv_b, mb_dt_bias, mb_a_log, mb_d, mb_norm_w, pe_wq, pe_keys, pe_u, pe_v, final_norm_w):
    nb = x.shape[0]
    xs = jnp.concatenate([ctx, x], axis=1)
    cond = jnp.zeros((SUBLANES, D), F32).at[:nb].set(c).at[nb].set(c_ctx)
    mods = ada_rows(cond, ada_w, ada_b)
    cos, sin = rope_tables()
    f = prev_tab = None
    for l in range(DEPTH):
        m = mods[l].reshape(SUBLANES, 6, D)
        tab = jnp.stack([jnp.broadcast_to(m[nb], (nb, 6, D)), m[:nb]], axis=1).reshape(2 * nb, 6, D)
        if f is None:
            h1 = norm_mod(xs, norm1_w[l].reshape(1, D), tab, 0)
        else:
            xs, h1 = resid_norm_mod(xs, f, prev_tab, norm1_w[l].reshape(1, D), tab)
        z = matmul(h1.reshape(nb * T_ALL, D), cast_bf16(w_in, l, ROW_BLK)).reshape(nb, T_ALL, IN_COLS)
        y_rw = rwkv_mixer(z, rw_mu[l], rw_w0[l], rw_w2[l], rw_a0[l], rw_a2[l], rw_g2[l], rw_kk[l],
                          rw_ka[l], rw_rk[l].reshape(RW_W), rw_ln_w[l], rw_ln_b[l])
        y_na = natten_mixer(z, na_bias_tables(na_rpb[l]))
        y_mb = mamba_mixer(z, mb_conv_w[l], mb_conv_b[l], mb_dt_bias[l], mb_a_log[l], mb_d[l],
                           mb_norm_w[l], cos, sin)
        xs = out_proj(xs, y_rw, y_na, y_mb, cast_bf16(w_out, l, ROW_BLK), tab)
        h2 = norm_mod(xs, norm2_w[l].reshape(1, D), tab, 3)
        f = peer_ffn(h2, cast_bf16(pe_wq, l, ROW_BLK), pe_keys[l],
                     cast_bf16(pe_u, l, 1024), cast_bf16(pe_v, l, 1024))
        prev_tab = tab
    return final_norm(xs, f, prev_tab, final_norm_w.reshape(1, D))
```

```python
import functools
import math

import numpy as np
import jax
import jax.numpy as jnp
from jax import lax
from jax.experimental import pallas as pl
from jax.experimental.pallas import tpu as pltpu

D = 2048
NB = 4
T_LAT = 2048
T_CTX = 256
T_ALL = T_CTX + T_LAT
DEPTH = 4
GRID_W = 64
GRID_ROWS = T_LAT // GRID_W
HEAD = 64
RW_W = 512
RW_H = 8
LORA = 64
LORA_G = 128
W_DECAY_SCALE = 0.606531
RW_LN_EPS = 64e-5
NA_W = 512
NA_H = 8
NA_KH = 8
NA_KW = 16
MB_INNER = 1024
MB_H = 16
MB_G = 4
MB_R = 4
MB_N = 128
MB_CONV = 5
MB_CHUNK = 128
ROPE_BASE = 10000.0
PEER_HEADS = 8
N_KEYS = 128
PEER_TOPK = 16
NORM_EPS = 1e-6
NEG_INF = -1e30
RW_COLS = 3 * RW_W + 4 * LORA + LORA_G
NA_COLS = 3 * NA_W
MB_CONV_CH = MB_INNER + 2 * MB_G * MB_N
MB_COLS = MB_INNER + MB_CONV_CH + 2 * MB_H
IN_COLS = RW_COLS + NA_COLS + MB_COLS
NA_OFF = RW_COLS
MB_OFF = RW_COLS + NA_COLS
XBC_OFF = MB_OFF + MB_INNER
DT_OFF = XBC_OFF + MB_CONV_CH

LANES = 128
SUBLANES = 8
VMEM_LIMIT = 56 * 1024 * 1024

ROW_BLK = 256
N_ROW_BLK = T_ALL // ROW_BLK
BF16 = jnp.bfloat16
F32 = jnp.float32
HI = lax.Precision.HIGHEST


def _cparams(sem):
    return pltpu.CompilerParams(dimension_semantics=sem, vmem_limit_bytes=VMEM_LIMIT)


def _dot(a, b, precision=None):
    return jnp.dot(a, b, preferred_element_type=F32, precision=precision)


def _split_bf16(x):
    hi = x.astype(BF16)
    return hi, (x - hi.astype(F32)).astype(BF16)


def _dot_split(a, b):
    ah, al = _split_bf16(a)
    bh, bl = _split_bf16(b)
    return _dot(ah, bh) + _dot(al, bh) + _dot(ah, bl)


def _dot_exact_rhs(a, b_bf16):
    ah, al = _split_bf16(a)
    return _dot(ah, b_bf16) + _dot(al, b_bf16)


def _dot_nt(a, b, precision=None):
    return lax.dot_general(a, b, (((1,), (1,)), ((), ())), preferred_element_type=F32,
                           precision=precision)


def _cast_kernel(x_ref, o_ref):
    o_ref[...] = x_ref[...].astype(BF16)


def cast_bf16(w, layer, rows_blk):
    _, r, c = w.shape
    return pl.pallas_call(
        _cast_kernel,
        grid=(r // rows_blk,),
        in_specs=[pl.BlockSpec((None, rows_blk, c), lambda i: (layer, i, 0))],
        out_specs=pl.BlockSpec((rows_blk, c), lambda i: (i, 0)),
        out_shape=jax.ShapeDtypeStruct((r, c), BF16),
        compiler_params=_cparams(("arbitrary",)),
        name="cast_bf16",
    )(w)


ADA_TN = 1024


def _ada_kernel(c_ref, w_ref, b_ref, o_ref):
    c = c_ref[...]
    s = c * jax.nn.sigmoid(c)
    o_ref[0] = _dot(s, w_ref[0], precision=HI) + b_ref[0]


def ada_rows(cond, ada_w, ada_b):
    nl = ada_w.shape[0]
    return pl.pallas_call(
        _ada_kernel,
        grid=(nl, 6 * D // ADA_TN),
        in_specs=[pl.BlockSpec((SUBLANES, D), lambda l, j: (0, 0)),
                  pl.BlockSpec((1, D, ADA_TN), lambda l, j: (l, 0, j)),
                  pl.BlockSpec((1, 1, ADA_TN), lambda l, j: (l, 0, j))],
        out_specs=pl.BlockSpec((1, SUBLANES, ADA_TN), lambda l, j: (l, 0, j)),
        out_shape=jax.ShapeDtypeStruct((nl, SUBLANES, 6 * D), F32),
        compiler_params=_cparams(("arbitrary", "arbitrary")),
        name="ada_rows",
    )(cond, ada_w, ada_b.reshape(nl, 1, 6 * D))


def _mod_index(b, i):
    return 2 * b + jnp.minimum(i, 1)


def _norm_mod_kernel(which, x_ref, nw_ref, tab_ref, o_ref):
    x = x_ref[0]
    ms = jnp.mean(x * x, axis=-1, keepdims=True)
    y = x * lax.rsqrt(ms + NORM_EPS) * nw_ref[...]
    shift = tab_ref[0, which:which + 1, :]
    scale = tab_ref[0, which + 1:which + 2, :]
    o_ref[0] = (y * (1.0 + scale) + shift).astype(BF16)


def norm_mod(x, nw, tab, which):
    nb = x.shape[0]
    return pl.pallas_call(
        functools.partial(_norm_mod_kernel, which),
        grid=(nb, N_ROW_BLK),
        in_specs=[pl.BlockSpec((1, ROW_BLK, D), lambda b, i: (b, i, 0)),
                  pl.BlockSpec((1, D), lambda b, i: (0, 0)),
                  pl.BlockSpec((1, 6, D), lambda b, i: (_mod_index(b, i), 0, 0))],
        out_specs=pl.BlockSpec((1, ROW_BLK, D), lambda b, i: (b, i, 0)),
        out_shape=jax.ShapeDtypeStruct(x.shape, BF16),
        compiler_params=_cparams(("arbitrary", "arbitrary")),
        name="norm_mod",
    )(x, nw, tab)


def _resid_norm_mod_kernel(x_ref, f_ref, ptab_ref, nw_ref, tab_ref, xo_ref, o_ref):
    x = x_ref[0] + ptab_ref[0, 5:6, :] * f_ref[0]
    xo_ref[0] = x
    ms = jnp.mean(x * x, axis=-1, keepdims=True)
    y = x * lax.rsqrt(ms + NORM_EPS) * nw_ref[...]
    o_ref[0] = (y * (1.0 + tab_ref[0, 1:2, :]) + tab_ref[0, 0:1, :]).astype(BF16)


def resid_norm_mod(x, f, prev_tab, nw, tab):
    nb = x.shape[0]
    row = lambda b, i: (b, i, 0)
    mod = lambda b, i: (_mod_index(b, i), 0, 0)
    return pl.pallas_call(
        _resid_norm_mod_kernel,
        grid=(nb, N_ROW_BLK),
        in_specs=[pl.BlockSpec((1, ROW_BLK, D), row),
                  pl.BlockSpec((1, ROW_BLK, D), row),
                  pl.BlockSpec((1, 6, D), mod),
                  pl.BlockSpec((1, D), lambda b, i: (0, 0)),
                  pl.BlockSpec((1, 6, D), mod)],
        out_specs=[pl.BlockSpec((1, ROW_BLK, D), row), pl.BlockSpec((1, ROW_BLK, D), row)],
        out_shape=[jax.ShapeDtypeStruct(x.shape, F32), jax.ShapeDtypeStruct(x.shape, BF16)],
        input_output_aliases={0: 0},
        compiler_params=_cparams(("arbitrary", "arbitrary")),
        name="resid_norm_mod",
    )(x, f, prev_tab, nw, tab)


def _final_norm_kernel(x_ref, f_ref, ptab_ref, nw_ref, o_ref):
    x = x_ref[0] + ptab_ref[0, 5:6, :] * f_ref[0]
    ms = jnp.mean(x * x, axis=-1, keepdims=True)
    o_ref[0] = x * lax.rsqrt(ms + NORM_EPS) * nw_ref[...]


def final_norm(x, f, prev_tab, nw):
    nb = x.shape[0]
    lat = lambda b, i: (b, i + 1, 0)
    return pl.pallas_call(
        _final_norm_kernel,
        grid=(nb, T_LAT // ROW_BLK),
        in_specs=[pl.BlockSpec((1, ROW_BLK, D), lat),
                  pl.BlockSpec((1, ROW_BLK, D), lat),
                  pl.BlockSpec((1, 6, D), lambda b, i: (2 * b + 1, 0, 0)),
                  pl.BlockSpec((1, D), lambda b, i: (0, 0))],
        out_specs=pl.BlockSpec((1, ROW_BLK, D), lambda b, i: (b, i, 0)),
        out_shape=jax.ShapeDtypeStruct((nb, T_LAT, D), F32),
        compiler_params=_cparams(("arbitrary", "arbitrary")),
        name="final_norm",
    )(x, f, prev_tab, nw)


MM_TM = 1024
MM_TN = 512


def _mm_kernel(a_ref, w_ref, o_ref):
    o_ref[...] = _dot(a_ref[...], w_ref[...])


def matmul(a, w):
    m, k = a.shape
    n = w.shape[1]
    tm = MM_TM if m % MM_TM == 0 else ROW_BLK
    return pl.pallas_call(
        _mm_kernel,
        grid=(m // tm, pl.cdiv(n, MM_TN)),
        in_specs=[pl.BlockSpec((tm, k), lambda i, j: (i, 0)),
                  pl.BlockSpec((k, MM_TN), lambda i, j: (0, j))],
        out_specs=pl.BlockSpec((tm, MM_TN), lambda i, j: (i, j)),
        out_shape=jax.ShapeDtypeStruct((m, n), F32),
        compiler_params=_cparams(("arbitrary", "arbitrary")),
        name="matmul",
    )(a, w)


def _out_proj_kernel(x_ref, rw_ref, na_ref, mb_ref, w_ref, tab_ref, o_ref):
    acc = _dot(rw_ref[0], w_ref[0:RW_W, :])
    acc += _dot(na_ref[0], w_ref[RW_W:RW_W + NA_W, :])
    acc += _dot(mb_ref[0], w_ref[RW_W + NA_W:, :])
    o_ref[0] = x_ref[0] + tab_ref[0, 2:3, :] * acc


def out_proj(x, y_rw, y_na, y_mb, w, tab):
    nb = x.shape[0]
    row = lambda b, i: (b, i, 0)
    return pl.pallas_call(
        _out_proj_kernel,
        grid=(nb, N_ROW_BLK),
        in_specs=[pl.BlockSpec((1, ROW_BLK, D), row),
                  pl.BlockSpec((1, ROW_BLK, RW_W), row),
                  pl.BlockSpec((1, ROW_BLK, NA_W), row),
                  pl.BlockSpec((1, ROW_BLK, MB_INNER), row),
                  pl.BlockSpec((D, D), lambda b, i: (0, 0)),
                  pl.BlockSpec((1, 6, D), lambda b, i: (_mod_index(b, i), 0, 0))],
        out_specs=pl.BlockSpec((1, ROW_BLK, D), row),
        out_shape=jax.ShapeDtypeStruct(x.shape, F32),
        input_output_aliases={0: 0},
        compiler_params=_cparams(("arbitrary", "arbitrary")),
        name="out_proj",
    )(x, y_rw, y_na, y_mb, w, tab)


TOPK_TT = 128


CAND_COUNTS = (16, 8, 5, 4, 3, 2, 2, 2)
CAND_ROWS = 16 + 8 * 7 + 8
N_HALVES = 2 * PEER_HEADS
N_SLOTS = PEER_HEADS * PEER_TOPK


def _first_max(s, rows, n):
    m = jnp.max(s, axis=0, keepdims=True)
    pos = jnp.min(jnp.where(s == m, rows, float(n)), axis=0, keepdims=True)
    return m, pos


def _peer_topk_kernel(q_ref, keys_ref, i1_ref, i2_ref, g_ref,
                      sc_ref, val_ref, idx_ref, cand_ref, c1_ref, c2_ref, best_ref, e1_ref, e2_ref):
    t = TOPK_TT
    for hp in range(N_HALVES):
        c0 = hp * N_KEYS
        qs = q_ref[:, c0:c0 + N_KEYS].astype(BF16)
        sc_ref[hp] = _dot_nt(keys_ref[hp // 2, hp % 2].astype(BF16), qs)

    rows = lax.broadcasted_iota(jnp.int32, (N_KEYS, t), 0).astype(F32)

    def stage1(r, carry):
        for hp in range(N_HALVES):
            s = sc_ref[hp]
            m, pos = _first_max(s, rows, N_KEYS)
            val_ref[hp, pl.ds(r, 1), :] = m
            idx_ref[hp, pl.ds(r, 1), :] = pos
            sc_ref[hp] = jnp.where(rows == pos, -jnp.inf, s)
        return carry

    lax.fori_loop(0, PEER_TOPK, stage1, 0)

    row8 = lax.broadcasted_iota(jnp.int32, (SUBLANES, t), 0)
    for h in range(PEER_HEADS):
        v1, v2 = val_ref[2 * h], val_ref[2 * h + 1]
        k1, k2 = idx_ref[2 * h], idx_ref[2 * h + 1]
        cand = [v1[0:1] + v2]
        c1 = [jnp.broadcast_to(k1[0:1], (PEER_TOPK, t))]
        c2 = [k2]
        for r1 in range(1, SUBLANES):
            cand.append(jnp.where(row8 < CAND_COUNTS[r1], v1[r1:r1 + 1] + v2[0:SUBLANES], -jnp.inf))
            c1.append(jnp.broadcast_to(k1[r1:r1 + 1], (SUBLANES, t)))
            c2.append(k2[0:SUBLANES])
        cand.append(v1[SUBLANES:] + v2[0:1])
        c1.append(k1[SUBLANES:])
        c2.append(jnp.broadcast_to(k2[0:1], (SUBLANES, t)))
        cand_ref[h] = jnp.concatenate(cand, axis=0)
        c1_ref[h] = jnp.concatenate(c1, axis=0)
        c2_ref[h] = jnp.concatenate(c2, axis=0)

    crow = lax.broadcasted_iota(jnp.int32, (CAND_ROWS, t), 0).astype(F32)

    def stage2(r, carry):
        for h in range(PEER_HEADS):
            cd = cand_ref[h]
            m, pos = _first_max(cd, crow, CAND_ROWS)
            sel = crow == pos
            slot = pl.ds(h * PEER_TOPK + r, 1)
            best_ref[slot, :] = m
            e1_ref[slot, :] = jnp.sum(jnp.where(sel, c1_ref[h], 0.0), axis=0, keepdims=True)
            e2_ref[slot, :] = jnp.sum(jnp.where(sel, c2_ref[h], 0.0), axis=0, keepdims=True)
            cand_ref[h] = jnp.where(sel, -jnp.inf, cd)
        return carry

    lax.fori_loop(0, PEER_TOPK, stage2, 0)

    for h in range(PEER_HEADS):
        sl = slice(h * PEER_TOPK, (h + 1) * PEER_TOPK)
        b = best_ref[sl, :]
        ex = jnp.exp(b - jnp.max(b, axis=0, keepdims=True))
        best_ref[sl, :] = ex / jnp.sum(ex, axis=0, keepdims=True)
    i1_ref[...] = e1_ref[...].T
    i2_ref[...] = e2_ref[...].T
    g_ref[...] = best_ref[...].T


def peer_topk(q, keys):
    n = q.shape[0]
    t = TOPK_TT
    out = jax.ShapeDtypeStruct((n, N_SLOTS), F32)
    spec = pl.BlockSpec((t, N_SLOTS), lambda i: (i, 0))
    return pl.pallas_call(
        _peer_topk_kernel,
        grid=(n // t,),
        in_specs=[pl.BlockSpec((t, D), lambda i: (i, 0)),
                  pl.BlockSpec(keys.shape, lambda i: (0, 0, 0, 0))],
        out_specs=[spec, spec, spec],
        out_shape=[out, out, out],
        scratch_shapes=[pltpu.VMEM((N_HALVES, N_KEYS, t), F32),
                        pltpu.VMEM((N_HALVES, PEER_TOPK, t), F32),
                        pltpu.VMEM((N_HALVES, PEER_TOPK, t), F32),
                        pltpu.VMEM((PEER_HEADS, CAND_ROWS, t), F32),
                        pltpu.VMEM((PEER_HEADS, CAND_ROWS, t), F32),
                        pltpu.VMEM((PEER_HEADS, CAND_ROWS, t), F32),
                        pltpu.VMEM((N_SLOTS, t), F32),
                        pltpu.VMEM((N_SLOTS, t), F32),
                        pltpu.VMEM((N_SLOTS, t), F32)],
        compiler_params=_cparams(("arbitrary",)),
        name="peer_topk",
    )(q, keys)


EXP_TM = 512
EXP_J = 8
EXP_BLK = EXP_J * N_KEYS
N_EXP_BLK = N_KEYS * N_KEYS // EXP_BLK
G_PITCH = N_KEYS // 2 + SUBLANES
SQRT_HALF = 0.7071067811865476


def _peer_expert_kernel(h_ref, i1_ref, i2_ref, g_ref, u_ref, v_ref, o_ref, gs_ref, w_ref):
    jj = pl.program_id(1)

    @pl.when(jj == 0)
    def _():
        o_ref[...] = jnp.zeros_like(o_ref)
        w_ref[...] = jnp.zeros_like(w_ref)
        rows = lax.broadcasted_iota(jnp.int32, (N_KEYS, N_KEYS), 0).astype(F32)

        def tok(t, carry):
            i1r = i1_ref[pl.ds(t, 1), :]
            i2r = i2_ref[pl.ds(t, 1), :]
            gr = g_ref[pl.ds(t, 1), :]
            at = jnp.where(rows == i1r, gr, 0.0).astype(BF16)
            bt = jnp.where(rows == i2r, 1.0, 0.0).astype(BF16)
            gm = _dot_nt(at, bt).astype(BF16)
            start = pl.multiple_of(t * G_PITCH, SUBLANES)
            gs_ref[pl.ds(start, N_KEYS // 2), :] = pltpu.bitcast(gm, jnp.uint32)
            return carry

        lax.fori_loop(0, EXP_TM, tok, 0, unroll=32)

    slot = jj % 2
    o_ref[...] += _dot(w_ref[1 - slot], v_ref[...])

    jb = jnp.minimum(jj, N_EXP_BLK - 1)
    s = _dot_nt(h_ref[...], u_ref[...])
    cols = []
    for c in range(EXP_J // 2):
        packed = gs_ref[pl.ds((EXP_J // 2) * jb + c, EXP_TM, stride=G_PITCH), :]
        cols.append(lax.bitcast_convert_type(packed << 16, F32))
        cols.append(lax.bitcast_convert_type(packed & jnp.uint32(0xFFFF0000), F32))
    g = jnp.concatenate(cols, axis=1)
    act = 0.5 * s * (1.0 + lax.erf(s * SQRT_HALF))
    w_ref[slot] = (g * act).astype(BF16)


def peer_expert(h, i1, i2, g, u, v):
    n = h.shape[0]
    row = lambda i, j: (i, 0)
    sel = pl.BlockSpec((EXP_TM, N_SLOTS), row)
    return pl.pallas_call(
        _peer_expert_kernel,
        grid=(n // EXP_TM, N_EXP_BLK + 1),
        in_specs=[pl.BlockSpec((EXP_TM, D), row), sel, sel, sel,
                  pl.BlockSpec((EXP_BLK, D), lambda i, j: (jnp.minimum(j, N_EXP_BLK - 1), 0)),
                  pl.BlockSpec((EXP_BLK, D), lambda i, j: (jnp.maximum(j - 1, 0), 0))],
        out_specs=pl.BlockSpec((EXP_TM, D), row),
        out_shape=jax.ShapeDtypeStruct((n, D), F32),
        scratch_shapes=[pltpu.VMEM((EXP_TM * G_PITCH, N_KEYS), jnp.uint32),
                        pltpu.VMEM((2, EXP_TM, EXP_BLK), BF16)],
        compiler_params=_cparams(("arbitrary", "arbitrary")),
        name="peer_expert",
    )(h, i1, i2, g, u, v)


def peer_ffn(h, wq, keys, u, v):
    nb = h.shape[0]
    hf = h.reshape(nb * T_ALL, D)
    i1, i2, g = peer_topk(matmul(hf, wq), keys)
    return peer_expert(hf, i1, i2, g, u, v).reshape(nb, T_ALL, D)


def _head_ones(width, group):
    r = np.arange(width) // group
    return jnp.asarray((r[:, None] == r[None, :]).astype(np.float32)).astype(BF16)


def _store_head_pairs(o_ref, x, y):
    for h in range(RW_H):
        ln = slice(h * HEAD, (h + 1) * HEAD)
        o_ref[pl.ds(h, ROW_BLK, stride=RW_H), :] = jnp.concatenate([x[:, ln], y[:, ln]], axis=1)


def _rw_pre_kernel(z_ref, hp_ref, hn_ref, mu_ref, w0_ref, w2_ref, a0_ref, a2_ref, g2_ref,
                   kk_ref, ka_ref, ones_ref,
                   r_ref, k_ref, v_ref, g_ref, rv_ref, awf_ref, awb_ref, bkf_ref, bkb_ref):
    z = z_ref[0]
    row = lax.broadcasted_iota(jnp.int32, (ROW_BLK, 1), 0)
    prev = jnp.where(row == 0, hp_ref[0, 0], pltpu.roll(z, 1, 0))
    nxt = jnp.where(row == ROW_BLK - 1, hn_ref[0, 0], pltpu.roll(z, ROW_BLK - 1, 0))
    zs = z + mu_ref[0:1, :] * (prev - z) + mu_ref[1:2, :] * (nxt - z)
    r = zs[:, 0:RW_W]
    k = zs[:, RW_W:2 * RW_W]
    v = zs[:, 2 * RW_W:3 * RW_W]
    o = 3 * RW_W
    lw = jnp.tanh(zs[:, o:o + 2 * LORA])
    la = zs[:, o + 2 * LORA:o + 4 * LORA]
    lg = jax.nn.sigmoid(zs[:, o + 4 * LORA:o + 4 * LORA + LORA_G])
    r_ref[0] = r
    k_ref[0] = k
    v_ref[0] = v
    g_ref[0] = _dot_split(lg, g2_ref[...])
    kkr = k * kk_ref[...]
    ss = _dot_exact_rhs(kkr * kkr, ones_ref[...])
    kkn = kkr / jnp.maximum(jnp.sqrt(ss), 1e-12)
    _store_head_pairs(rv_ref, r, v)
    for d, (aw_ref, bk_ref) in enumerate(((awf_ref, bkf_ref), (awb_ref, bkb_ref))):
        dec = w0_ref[d:d + 1, :] + _dot_split(lw, w2_ref[d])
        a = jax.nn.sigmoid(a0_ref[d:d + 1, :] + _dot_split(la, a2_ref[d]))
        _store_head_pairs(aw_ref, -kkn, jnp.exp(-W_DECAY_SCALE * jax.nn.sigmoid(dec)))
        _store_head_pairs(bk_ref, kkn * a, k * (1.0 + (a - 1.0) * ka_ref[...]))


def rw_pre(z, mu, w0, w2, a0, a2, g2, kk, ka):
    nb = z.shape[0]
    zero = jnp.zeros((nb, 1, RW_COLS), F32)
    last = z[:, ROW_BLK - 1::ROW_BLK, :RW_COLS]
    first = z[:, ::ROW_BLK, :RW_COLS]
    halo_prev = jnp.concatenate([zero, zero, last[:, 1:N_ROW_BLK - 1]], axis=1)
    halo_next = jnp.concatenate([zero, first[:, 2:], zero], axis=1)
    halo_prev = halo_prev.reshape(nb, N_ROW_BLK, 1, RW_COLS)
    halo_next = halo_next.reshape(nb, N_ROW_BLK, 1, RW_COLS)
    zpad = jnp.zeros((LORA, RW_W), F32)
    w2p = jnp.stack([jnp.concatenate([w2[0], zpad]), jnp.concatenate([zpad, w2[1]])])
    a2p = jnp.stack([jnp.concatenate([a2[0], zpad]), jnp.concatenate([zpad, a2[1]])])
    row = lambda b, i: (b, i, 0)
    full = lambda shape: pl.BlockSpec(shape, lambda b, i: (0,) * len(shape))
    out = jax.ShapeDtypeStruct((nb, T_ALL, RW_W), F32)
    ospec = pl.BlockSpec((1, ROW_BLK, RW_W), row)
    pout = jax.ShapeDtypeStruct((nb, T_ALL * RW_H, LANES), F32)
    pspec = pl.BlockSpec((None, ROW_BLK * RW_H, LANES), row)
    return pl.pallas_call(
        _rw_pre_kernel,
        grid=(nb, N_ROW_BLK),
        in_specs=[pl.BlockSpec((1, ROW_BLK, RW_COLS), row),
                  pl.BlockSpec((1, 1, 1, RW_COLS), lambda b, i: (b, i, 0, 0)),
                  pl.BlockSpec((1, 1, 1, RW_COLS), lambda b, i: (b, i, 0, 0)),
                  full((2, RW_COLS)), full((2, RW_W)), full((2, 2 * LORA, RW_W)),
                  full((2, RW_W)), full((2, 2 * LORA, RW_W)), full((LORA_G, RW_W)),
                  full((1, RW_W)), full((1, RW_W)), full((RW_W, RW_W))],
        out_specs=[ospec] * 4 + [pspec] * 5,
        out_shape=[out] * 4 + [pout] * 5,
        compiler_params=_cparams(("arbitrary", "arbitrary")),
        name="rw_pre",
    )(z, halo_prev, halo_next, mu, w0, w2p, a0, a2p, g2, kk.reshape(1, RW_W),
      ka.reshape(1, RW_W), _head_ones(RW_W, HEAD))


SCAN_TB = 32
SCAN_I = HEAD // 2


SCAN_NBLK = T_ALL // SCAN_TB
SCAN_CTX_BLK = T_CTX // SCAN_TB


def _rw_scan_kernel(awf_ref, bkf_ref, rvf_ref, awb_ref, bkb_ref, rvb_ref, yf_ref, yb_ref,
                    s_ref, sa_ref, gam_ref, *tiles):
    @pl.when(pl.program_id(0) == 0)
    def _():
        s_ref[...] = jnp.zeros_like(s_ref)

    low_half = lax.broadcasted_iota(jnp.int32, (1, LANES), 1) < LANES // 2
    sub = lax.broadcasted_iota(jnp.int32, (SUBLANES, LANES), 0)
    sets = (tiles[:5], tiles[5:])

    def prep(t, tile_set):
        a_s, b_s, k_s, r_s, v_s = tile_set
        rf = pl.ds(pl.multiple_of(t * RW_H, RW_H), RW_H)
        rb = pl.ds(pl.multiple_of((SCAN_TB - 1 - t) * RW_H, RW_H), RW_H)

        def transposed(f_ref, b_ref):
            rows = [f_ref[b, rf, :] for b in range(NB)] + [b_ref[b, rb, :] for b in range(NB)]
            return jnp.concatenate(rows + rows, axis=0).T

        aw = transposed(awf_ref, awb_ref)
        gam_prev = gam_ref[...]
        gam = gam_prev * aw[HEAD:]
        gam_ref[...] = gam
        inv = 1.0 / gam
        a_s[...] = aw[:HEAD] * gam_prev
        bk = transposed(bkf_ref, bkb_ref)
        b_s[...] = bk[:HEAD] * inv
        k_s[...] = bk[HEAD:] * inv
        rv = transposed(rvf_ref, rvb_ref)
        r_s[...] = rv[:HEAD] * gam
        v_s[...] = jnp.where(low_half, rv[HEAD:HEAD + SCAN_I], rv[HEAD + SCAN_I:])

    def fold(x):
        return jnp.sum(x.reshape(SUBLANES, SUBLANES, LANES), axis=0)

    def reduce8(p):
        z = [jnp.where(sub < 4, p[k], p[k + 4]) + pltpu.roll(jnp.where(sub < 4, p[k + 4], p[k]), 4, 0)
             for k in range(4)]
        even2 = (sub & 2) == 0
        v = [jnp.where(even2, z[k] + pltpu.roll(z[k], 6, 0), z[k + 2] + pltpu.roll(z[k + 2], 2, 0))
             for k in range(2)]
        return jnp.where((sub & 1) == 0, v[0] + pltpu.roll(v[0], 7, 0), v[1] + pltpu.roll(v[1], 1, 0))

    def step(t, tile_set):
        a_s, b_s, k_s, r_s, v_s = tile_set
        a = a_s[...]
        for c in range(SCAN_I // SUBLANES):
            rows = range(c * SUBLANES, (c + 1) * SUBLANES)
            sa_ref[c * SUBLANES:(c + 1) * SUBLANES, :] = reduce8([fold(s_ref[i] * a) for i in rows])
        b = b_s[...]
        k = k_s[...]
        r = r_s[...]
        for c in range(SCAN_I // SUBLANES):
            parts = []
            for i in range(c * SUBLANES, (c + 1) * SUBLANES):
                sn = s_ref[i] + sa_ref[pl.ds(i, 1), :] * b + v_s[pl.ds(i, 1), :] * k
                s_ref[i] = sn
                parts.append(fold(sn * r))
            y = reduce8(parts)
            yf_ref[t, c * SUBLANES:(c + 1) * SUBLANES, :] = y
            yb_ref[SCAN_TB - 1 - t, c * SUBLANES:(c + 1) * SUBLANES, :] = y

    gam_ref[...] = jnp.ones_like(gam_ref)
    prep(0, sets[0])

    def two_steps(u, carry):
        t = 2 * u
        prep(t + 1, sets[1])
        step(t, sets[0])
        prep(t + 2, sets[0])
        step(t + 1, sets[1])
        return carry

    lax.fori_loop(0, SCAN_TB // 2 - 1, two_steps, 0)
    prep(SCAN_TB - 1, sets[1])
    step(SCAN_TB - 2, sets[0])
    step(SCAN_TB - 1, sets[1])

    gam_end = gam_ref[...]
    for i in range(SCAN_I):
        s_ref[i] = s_ref[i] * gam_end


def _mirror_block(g):
    return jnp.where(g < SCAN_CTX_BLK, SCAN_CTX_BLK - 1 - g, SCAN_NBLK + SCAN_CTX_BLK - 1 - g)


def rw_scan(awf, bkf, awb, bkb, rv):
    assert awf.shape[0] == NB
    fspec = pl.BlockSpec((NB, SCAN_TB * RW_H, LANES), lambda g: (0, g, 0))
    bspec = pl.BlockSpec((NB, SCAN_TB * RW_H, LANES), lambda g: (0, _mirror_block(g), 0))
    yf = pl.BlockSpec((SCAN_TB, SCAN_I, LANES), lambda g: (g, 0, 0))
    yb = pl.BlockSpec((SCAN_TB, SCAN_I, LANES), lambda g: (_mirror_block(g), 0, 0))
    out = jax.ShapeDtypeStruct((T_ALL, SCAN_I, LANES), F32)
    tile_set = [pltpu.VMEM((HEAD, LANES), F32)] * 4 + [pltpu.VMEM((SCAN_I, LANES), F32)]
    return pl.pallas_call(
        _rw_scan_kernel,
        grid=(SCAN_NBLK,),
        in_specs=[fspec] * 3 + [bspec] * 3,
        out_specs=[yf, yb],
        out_shape=[out, out],
        scratch_shapes=[pltpu.VMEM((SCAN_I, HEAD, LANES), F32),
                        pltpu.VMEM((SCAN_I, LANES), F32),
                        pltpu.VMEM((HEAD, LANES), F32)] + tile_set + tile_set,
        compiler_params=_cparams(("arbitrary",)),
        name="rw_scan",
    )(awf, bkf, rv, awb, bkb, rv)


def _scan_unlayout_i(y, d):
    s = y.reshape(T_ALL, SCAN_I, 2, 2, NB, RW_H)[:, :, :, d]
    return s.transpose(3, 0, 4, 2, 1).reshape(NB, T_ALL, RW_W)


def _rw_post_kernel(yf_ref, yb_ref, r_ref, k_ref, v_ref, g_ref, rk_ref, lw_ref, lb_ref, ones_ref,
                    o_ref):
    y = yf_ref[0] + yb_ref[0]
    ones = ones_ref[...]
    mean = _dot_exact_rhs(y, ones) * (1.0 / HEAD)
    yc = y - mean
    var = _dot_exact_rhs(yc * yc, ones) * (1.0 / HEAD)
    yn = yc * lax.rsqrt(var + RW_LN_EPS) * lw_ref[...] + lb_ref[...]
    bonus = _dot_exact_rhs(r_ref[0] * k_ref[0] * rk_ref[...], ones) * v_ref[0]
    o_ref[0] = ((yn + bonus) * g_ref[0]).astype(BF16)


def rw_post(yf, yb, r, k, v, g, rk, ln_w, ln_b):
    nb = yf.shape[0]
    row = lambda b, i: (b, i, 0)
    spec = pl.BlockSpec((1, ROW_BLK, RW_W), row)
    vec = pl.BlockSpec((1, RW_W), lambda b, i: (0, 0))
    return pl.pallas_call(
        _rw_post_kernel,
        grid=(nb, N_ROW_BLK),
        in_specs=[spec] * 6 + [vec] * 3 + [pl.BlockSpec((RW_W, RW_W), lambda b, i: (0, 0))],
        out_specs=spec,
        out_shape=jax.ShapeDtypeStruct((nb, T_ALL, RW_W), BF16),
        compiler_params=_cparams(("arbitrary", "arbitrary")),
        name="rw_post",
    )(yf, yb, r, k, v, g, rk.reshape(1, RW_W), ln_w.reshape(1, RW_W), ln_b.reshape(1, RW_W),
      _head_ones(RW_W, HEAD))


def rwkv_mixer(z, mu, w0, w2, a0, a2, g2, kk, ka, rk, ln_w, ln_b):
    r, k, v, g, rv, awf, awb, bkf, bkb = rw_pre(z, mu, w0, w2, a0, a2, g2, kk, ka)
    y1, y2 = rw_scan(awf, bkf, awb, bkb, rv)
    return rw_post(_scan_unlayout_i(y1, 0), _scan_unlayout_i(y2, 1), r, k, v, g, rk, ln_w, ln_b)


NA_QROWS = ROW_BLK // GRID_W
NA_SLAB = NA_KH + NA_QROWS - 1
NA_SLAB_T = NA_SLAB * GRID_W
NA_SCALE = HEAD ** -0.5


def _na_bias_index():
    a = np.arange(NA_QROWS)[:, None]
    u = np.arange(NA_SLAB)[None, :]
    idx_r, valid = [], []
    for r0, u0 in ((0, 0), (NA_QROWS, 0), (GRID_ROWS - NA_QROWS, GRID_ROWS - NA_SLAB)):
        r = r0 + a
        kr = u0 + u
        kr0 = np.clip(r - NA_KH // 2, 0, GRID_ROWS - NA_KH)
        valid.append((kr >= kr0) & (kr < kr0 + NA_KH))
        idx_r.append(np.clip(kr - r + NA_KH - 1, 0, 2 * NA_KH - 2))
    idx_r = np.stack(idx_r)
    valid = np.stack(valid)[:, :, None, :, None]
    qc = np.arange(GRID_W)[:, None]
    kc = np.arange(GRID_W)[None, :]
    c0 = np.clip(qc - NA_KW // 2, 0, GRID_W - NA_KW)
    in_win = ((kc >= c0) & (kc < c0 + NA_KW))[None, None, :, None, :]
    idx_c = np.clip(kc - qc + NA_KW - 1, 0, 2 * NA_KW - 2)
    col_onehot = (idx_c[None] == np.arange(2 * NA_KW - 1)[:, None, None]).astype(np.float32)
    shape = (3, NA_QROWS, GRID_W, NA_SLAB, GRID_W)
    return idx_r, col_onehot, np.broadcast_to(valid & in_win, shape)


def na_bias_tables(rpb):
    nl = rpb.shape[0]
    idx_r, col_onehot, mask = _na_bias_index()
    rows = rpb[:, :, idx_r.reshape(-1), :].reshape(nl, NA_H, 3, NA_QROWS, NA_SLAB, 2 * NA_KW - 1)
    b = jnp.einsum('lhvauc,cqk->lhvaquk', rows, jnp.asarray(col_onehot), precision=HI)
    b = jnp.where(mask[None, None], b, NEG_INF)
    return b.reshape(nl, NA_H, 3, ROW_BLK, NA_SLAB_T)


def _na_kernel(q_ref, k_ref, v_ref, bias_ref, o_ref):
    qi = pl.program_id(2)

    @pl.when(qi == 0)
    def _():
        ys = []
        for hh in range(2):
            ln = slice(hh * HEAD, (hh + 1) * HEAD)
            q = (q_ref[0, :, ln] * NA_SCALE).astype(BF16)
            s = _dot_nt(q, k_ref[0, 0:T_CTX, ln].astype(BF16))
            p = jnp.exp(s - jnp.max(s, axis=-1, keepdims=True))
            y = _dot(p.astype(BF16), v_ref[0, 0:T_CTX, ln].astype(BF16))
            ys.append(y / jnp.sum(p, axis=-1, keepdims=True))
        o_ref[0] = jnp.concatenate(ys, axis=1).astype(BF16)

    @pl.when(qi > 0)
    def _():
        u0 = jnp.clip(NA_QROWS * (qi - 1) - NA_KH // 2, 0, GRID_ROWS - NA_SLAB)
        start = pl.multiple_of(T_CTX + GRID_W * u0, GRID_W)
        ys = []
        for hh in range(2):
            ln = slice(hh * HEAD, (hh + 1) * HEAD)
            q = (q_ref[0, :, ln] * NA_SCALE).astype(BF16)
            sc = _dot_nt(q, k_ref[0, 0:T_CTX, ln].astype(BF16))
            sw = _dot_nt(q, k_ref[0, pl.ds(start, NA_SLAB_T), ln].astype(BF16)) + bias_ref[hh, 0]
            m = jnp.maximum(jnp.max(sc, axis=-1, keepdims=True), jnp.max(sw, axis=-1, keepdims=True))
            pc = jnp.exp(sc - m)
            pw = jnp.exp(sw - m)
            y = (_dot(pw.astype(BF16), v_ref[0, pl.ds(start, NA_SLAB_T), ln].astype(BF16))
                 + _dot(pc.astype(BF16), v_ref[0, 0:T_CTX, ln].astype(BF16)))
            den = jnp.sum(pc, axis=-1, keepdims=True) + jnp.sum(pw, axis=-1, keepdims=True)
            ys.append(y / den)
        o_ref[0] = jnp.concatenate(ys, axis=1).astype(BF16)


def natten_mixer(z, bias, layer):
    nb = z.shape[0]
    qb, kb, vb = (NA_OFF // LANES, (NA_OFF + NA_W) // LANES, (NA_OFF + 2 * NA_W) // LANES)
    n_blk = N_ROW_BLK - 1

    def bias_idx(b, hp, qi):
        var = jnp.where(qi <= 1, 0, jnp.where(qi == n_blk, 2, 1))
        return (layer, hp, var, 0, 0)

    return pl.pallas_call(
        _na_kernel,
        grid=(nb, NA_H // 2, N_ROW_BLK),
        in_specs=[pl.BlockSpec((1, ROW_BLK, LANES), lambda b, hp, qi: (b, qi, qb + hp)),
                  pl.BlockSpec((1, T_ALL, LANES), lambda b, hp, qi: (b, 0, kb + hp)),
                  pl.BlockSpec((1, T_ALL, LANES), lambda b, hp, qi: (b, 0, vb + hp)),
                  pl.BlockSpec((None, 2, 1, ROW_BLK, NA_SLAB_T), bias_idx)],
        out_specs=pl.BlockSpec((1, ROW_BLK, LANES), lambda b, hp, qi: (b, qi, hp)),
        out_shape=jax.ShapeDtypeStruct((nb, T_ALL, NA_W), BF16),
        compiler_params=_cparams(("arbitrary", "arbitrary", "arbitrary")),
        name="natten",
    )(z, z, z, bias)


ROPE_NF = MB_N // 4


def rope_tables():
    pos = np.arange(T_LAT)
    inv = ROPE_BASE ** (-np.arange(ROPE_NF, dtype=np.float32) / ROPE_NF)
    lane = np.arange(MB_N)
    p = np.where(lane[None, :] < MB_N // 2, (pos // GRID_W)[:, None], (pos % GRID_W)[:, None])
    ang = p.astype(np.float32) * inv[lane % ROPE_NF][None, :]
    sign = np.where((lane % (2 * ROPE_NF)) < ROPE_NF, -1.0, 1.0)[None, :]
    cos = np.concatenate([np.ones((T_CTX, MB_N), np.float32), np.cos(ang)])
    sin = np.concatenate([np.zeros((T_CTX, MB_N), np.float32), np.sin(ang) * sign])
    cos = np.stack([np.ones_like(cos), cos])
    sin = np.stack([np.zeros_like(sin), sin])
    return jnp.asarray(cos, F32), jnp.asarray(sin, F32)


CONV_PAD = SUBLANES


def _mb_pre_kernel(z_ref, w_ref, b_ref, cos_ref, sin_ref, o_ref, zp_ref):
    half = MB_CONV // 2
    zero = jnp.zeros((CONV_PAD, LANES), F32)
    zp_ref[0:CONV_PAD, :] = zero
    zp_ref[CONV_PAD + T_ALL:, :] = zero
    zp_ref[CONV_PAD:CONV_PAD + T_ALL, :] = z_ref[0]
    lane = lax.broadcasted_iota(jnp.int32, (1, MB_N), 1)
    first = (lane % (2 * ROPE_NF)) < ROPE_NF
    row = lax.broadcasted_iota(jnp.int32, (ROW_BLK, 1), 0)

    def chunk(c, carry):
        base = pl.multiple_of(c * ROW_BLK, ROW_BLK)
        seg_lo = jnp.where(c == 0, 0, T_CTX) - base
        seg_hi = jnp.where(c == 0, T_CTX, T_ALL) - base
        acc = zp_ref[pl.ds(base + CONV_PAD, ROW_BLK), :] * w_ref[half:half + 1, :] + b_ref[...]
        for d in (-2, -1, 1, 2):
            win = zp_ref[pl.ds(base + CONV_PAD + d, ROW_BLK), :]
            ok = (row + d >= seg_lo) & (row + d < seg_hi)
            acc = acc + jnp.where(ok, win, 0.0) * w_ref[half + d:half + d + 1, :]
        y = acc * jax.nn.sigmoid(acc)
        partner = jnp.where(first, pltpu.roll(y, MB_N - ROPE_NF, 1), pltpu.roll(y, ROPE_NF, 1))
        rows = pl.ds(base, ROW_BLK)
        o_ref[0, rows, :] = y * cos_ref[0, rows, :] + partner * sin_ref[0, rows, :]
        return carry

    lax.fori_loop(0, N_ROW_BLK, chunk, 0)


def mb_pre(z, conv_w, conv_b, cos, sin):
    nb = z.shape[0]
    c0 = XBC_OFF // LANES
    rot = lambda b, j: (jnp.where(j >= MB_INNER // LANES, 1, 0), 0, 0)
    return pl.pallas_call(
        _mb_pre_kernel,
        grid=(nb, MB_CONV_CH // LANES),
        in_specs=[pl.BlockSpec((1, T_ALL, LANES), lambda b, j: (b, 0, c0 + j)),
                  pl.BlockSpec((MB_CONV, LANES), lambda b, j: (0, j)),
                  pl.BlockSpec((1, LANES), lambda b, j: (0, j)),
                  pl.BlockSpec((1, T_ALL, MB_N), rot),
                  pl.BlockSpec((1, T_ALL, MB_N), rot)],
        out_specs=pl.BlockSpec((1, T_ALL, LANES), lambda b, j: (b, 0, j)),
        out_shape=jax.ShapeDtypeStruct((nb, T_ALL, MB_CONV_CH), F32),
        scratch_shapes=[pltpu.VMEM((T_ALL + 2 * CONV_PAD, LANES), F32)],
        compiler_params=_cparams(("arbitrary", "arbitrary")),
        name="mb_pre",
    )(z, conv_w, conv_b.reshape(1, MB_CONV_CH), cos, sin)


N_CHUNK = T_ALL // MB_CHUNK
N_CTX_CHUNK = T_CTX // MB_CHUNK
XB_BLK = MB_INNER // LANES
XC_BLK = XB_BLK + MB_G


def _ssd_direction(reverse, gs, x_ref, b_ref, c_ref, dtw_ref, dtr_ref, bw_ref, br_ref, aw_ref, ar_ref,
                   st_ref):
    L = MB_CHUNK
    xl = slice(gs * MB_R * HEAD, (gs + 1) * MB_R * HEAD)
    nl = slice(gs * MB_N, (gs + 1) * MB_N)
    dt_head = jax.nn.softplus(dtw_ref[0, 0, :, xl] + bw_ref[0, :, xl])
    da_head = dt_head * (-jnp.exp(aw_ref[0, :, xl]))
    dtr = jax.nn.softplus(dtr_ref[0, 0, gs] + br_ref[0, gs])
    dar = dtr * (-jnp.exp(ar_ref[0, gs]))
    ri = lax.broadcasted_iota(jnp.int32, (L, L), 0)
    ci = lax.broadcasted_iota(jnp.int32, (L, L), 1)
    mask = (ri <= ci) if reverse else (ri >= ci)
    tri_r = (ri >= ci).astype(F32) if reverse else (ri <= ci).astype(F32)
    tot_row = 0 if reverse else L - 1
    cum_head = da_head
    trow = lax.broadcasted_iota(jnp.int32, (L, 1), 0)
    step = 1
    while step < L:
        if reverse:
            shifted = jnp.where(trow < L - step, pltpu.roll(cum_head, L - step, 0), 0.0)
        else:
            shifted = jnp.where(trow >= step, pltpu.roll(cum_head, step, 0), 0.0)
        cum_head = cum_head + shifted
        step *= 2
    cum_r = _dot(dar, tri_r, precision=HI)
    tot = cum_head[tot_row:tot_row + 1, :]
    bm = b_ref[0, :, nl]
    cm = c_ref[0, :, nl].astype(BF16)
    cb = _dot_nt(cm, bm.astype(BF16))
    bt = bm.T.astype(BF16)
    xc = x_ref[0, :, xl] * dt_head
    xdec = (xc * jnp.exp(tot - cum_head)).astype(BF16)
    ecum = jnp.exp(cum_head)
    etot = jnp.exp(tot)
    first = lax.broadcasted_iota(jnp.int32, (1, LANES), 1) < HEAD
    ys = []
    for p in range(MB_R // 2):
        sl = slice(p * LANES, (p + 1) * LANES)
        xp = xc[:, sl].astype(BF16)
        per_head = []
        for r in (2 * p, 2 * p + 1):
            diff = cum_head[:, r * HEAD:r * HEAD + 1] - cum_r[r:r + 1, :]
            lmat = jnp.exp(jnp.where(mask, diff, -jnp.inf))
            per_head.append(_dot((cb * lmat).astype(BF16), xp))
        y_diag = jnp.where(first, per_head[0], per_head[1])
        sp = gs * (MB_R // 2) + p
        st = st_ref[sp]
        y_off = _dot(cm, st.astype(BF16)) * ecum[:, sl]
        st_ref[sp] = etot[:, sl] * st + _dot(bt, xdec[:, sl])
        ys.append(y_diag + y_off)
    return jnp.concatenate(ys, axis=1)


def _mb_ssd_kernel(*refs):
    fwd, bwd = refs[0:9], refs[9:18]
    yf_ref, yb_ref, stf_ref, stb_ref = refs[18:22]

    @pl.when(pl.program_id(2) == 0)
    def _():
        stf_ref[...] = jnp.zeros_like(stf_ref)
        stb_ref[...] = jnp.zeros_like(stb_ref)

    yf_ref[0] = jnp.concatenate([_ssd_direction(False, gs, *fwd, stf_ref) for gs in range(2)], axis=1)
    yb_ref[0] = jnp.concatenate([_ssd_direction(True, gs, *bwd, stb_ref) for gs in range(2)], axis=1)


def _bwd_chunk(i):
    return jnp.where(i < N_CTX_CHUNK, N_CTX_CHUNK - 1 - i, N_CHUNK + N_CTX_CHUNK - 1 - i)


def mb_ssd(xbc, dt_raw, dt_bias, a_log):
    nb = xbc.shape[0]
    dt4 = dt_raw.reshape(nb, T_ALL, 2, MB_H).transpose(0, 2, 1, 3)
    dtw = jnp.repeat(dt4, HEAD, axis=-1)
    dtr = dt4.reshape(nb, 2, T_ALL, MB_G, MB_R).transpose(0, 1, 3, 4, 2)
    bw = jnp.repeat(dt_bias, HEAD, axis=-1).reshape(2, 1, MB_INNER)
    aw = jnp.repeat(a_log, HEAD, axis=-1).reshape(2, 1, MB_INNER)
    b4 = dt_bias.reshape(2, MB_G, MB_R, 1)
    a4 = a_log.reshape(2, MB_G, MB_R, 1)
    xw = 2 * MB_R * HEAD
    ins, specs = [], []
    for d, cidx in ((0, lambda i: i), (1, _bwd_chunk)):
        ins += [xbc, xbc, xbc, dtw, dtr, bw, b4, aw, a4]
        specs += [
            pl.BlockSpec((1, MB_CHUNK, xw), lambda b, g, i, c=cidx: (b, c(i), g)),
            pl.BlockSpec((1, MB_CHUNK, 2 * MB_N), lambda b, g, i, c=cidx: (b, c(i), XB_BLK // 2 + g)),
            pl.BlockSpec((1, MB_CHUNK, 2 * MB_N), lambda b, g, i, c=cidx: (b, c(i), XC_BLK // 2 + g)),
            pl.BlockSpec((1, 1, MB_CHUNK, xw), lambda b, g, i, c=cidx, d=d: (b, d, c(i), g)),
            pl.BlockSpec((1, 1, 2, MB_R, MB_CHUNK), lambda b, g, i, c=cidx, d=d: (b, d, g, 0, c(i))),
            pl.BlockSpec((1, 1, xw), lambda b, g, i, d=d: (d, 0, g)),
            pl.BlockSpec((1, 2, MB_R, 1), lambda b, g, i, d=d: (d, g, 0, 0)),
            pl.BlockSpec((1, 1, xw), lambda b, g, i, d=d: (d, 0, g)),
            pl.BlockSpec((1, 2, MB_R, 1), lambda b, g, i, d=d: (d, g, 0, 0)),
        ]
    out = jax.ShapeDtypeStruct((nb, T_ALL, MB_INNER), F32)
    return pl.pallas_call(
        _mb_ssd_kernel,
        grid=(nb, MB_G // 2, N_CHUNK),
        in_specs=specs,
        out_specs=[pl.BlockSpec((1, MB_CHUNK, xw), lambda b, g, i: (b, i, g)),
                   pl.BlockSpec((1, MB_CHUNK, xw), lambda b, g, i: (b, _bwd_chunk(i), g))],
        out_shape=[out, out],
        scratch_shapes=[pltpu.VMEM((MB_R, MB_N, 2 * HEAD), F32),
                        pltpu.VMEM((MB_R, MB_N, 2 * HEAD), F32)],
        compiler_params=_cparams(("arbitrary", "arbitrary", "arbitrary")),
        name="mb_ssd",
    )(*ins)


MB_GW = MB_INNER // MB_G


def _mb_post_kernel(yf_ref, yb_ref, x_ref, glo_ref, ghi_ref, d_ref, nw_ref, o_ref):
    y = yf_ref[0] + yb_ref[0] + x_ref[0] * d_ref[...]
    gate = jnp.concatenate([glo_ref[0], ghi_ref[0]], axis=1)
    y = y * (gate * jax.nn.sigmoid(gate))
    ms = jnp.mean(y * y, axis=-1, keepdims=True)
    o_ref[0] = (y * lax.rsqrt(ms + NORM_EPS) * nw_ref[...]).astype(BF16)


def mb_post(yf, yb, xbc, z, d_skip, norm_w):
    nb = yf.shape[0]
    g0 = MB_OFF // LANES
    grp = lambda b, i, g: (b, i, g)
    return pl.pallas_call(
        _mb_post_kernel,
        grid=(nb, N_ROW_BLK, MB_G),
        in_specs=[pl.BlockSpec((1, ROW_BLK, MB_GW), grp),
                  pl.BlockSpec((1, ROW_BLK, MB_GW), grp),
                  pl.BlockSpec((1, ROW_BLK, MB_GW), grp),
                  pl.BlockSpec((1, ROW_BLK, LANES), lambda b, i, g: (b, i, g0 + 2 * g)),
                  pl.BlockSpec((1, ROW_BLK, LANES), lambda b, i, g: (b, i, g0 + 2 * g + 1)),
                  pl.BlockSpec((1, MB_GW), lambda b, i, g: (0, g)),
                  pl.BlockSpec((1, MB_GW), lambda b, i, g: (0, g))],
        out_specs=pl.BlockSpec((1, ROW_BLK, MB_GW), grp),
        out_shape=jax.ShapeDtypeStruct((nb, T_ALL, MB_INNER), BF16),
        compiler_params=_cparams(("arbitrary", "arbitrary", "arbitrary")),
        name="mb_post",
    )(yf, yb, xbc, z, z, jnp.repeat(d_skip, HEAD).reshape(1, MB_INNER), norm_w.reshape(1, MB_INNER))


def mamba_mixer(z, conv_w, conv_b, dt_bias, a_log, d_skip, norm_w, cos, sin):
    xbc = mb_pre(z, conv_w, conv_b, cos, sin)
    yf, yb = mb_ssd(xbc, z[:, :, DT_OFF:DT_OFF + 2 * MB_H], dt_bias, a_log)
    return mb_post(yf, yb, xbc, z, d_skip, norm_w)


def kernel(x, c, ctx, c_ctx, ada_w, ada_b, norm1_w, norm2_w, w_in, w_out, rw_mu, rw_w0, rw_w2, rw_a0, rw_a2, rw_g2, rw_kk, rw_ka, rw_rk, rw_ln_w, rw_ln_b, na_rpb, mb_conv_w, mb_conv_b, mb_dt_bias, mb_a_log, mb_d, mb_norm_w, pe_wq, pe_keys, pe_u, pe_v, final_norm_w):
    nb = x.shape[0]
    xs = jnp.concatenate([ctx, x], axis=1)
    cond = jnp.zeros((SUBLANES, D), F32).at[:nb].set(c).at[nb].set(c_ctx)
    mods = ada_rows(cond, ada_w, ada_b)
    cos, sin = rope_tables()
    na_bias = na_bias_tables(na_rpb)
    f = prev_tab = None
    for l in range(DEPTH):
        m = mods[l].reshape(SUBLANES, 6, D)
        tab = jnp.stack([jnp.broadcast_to(m[nb], (nb, 6, D)), m[:nb]], axis=1).reshape(2 * nb, 6, D)
        if f is None:
            h1 = norm_mod(xs, norm1_w[l].reshape(1, D), tab, 0)
        else:
            xs, h1 = resid_norm_mod(xs, f, prev_tab, norm1_w[l].reshape(1, D), tab)
        z = matmul(h1.reshape(nb * T_ALL, D), cast_bf16(w_in, l, ROW_BLK)).reshape(nb, T_ALL, IN_COLS)
        y_rw = rwkv_mixer(z, rw_mu[l], rw_w0[l], rw_w2[l], rw_a0[l], rw_a2[l], rw_g2[l], rw_kk[l],
                          rw_ka[l], rw_rk[l].reshape(RW_W), rw_ln_w[l], rw_ln_b[l])
        y_na = natten_mixer(z, na_bias, l)
        y_mb = mamba_mixer(z, mb_conv_w[l], mb_conv_b[l], mb_dt_bias[l], mb_a_log[l], mb_d[l],
                           mb_norm_w[l], cos, sin)
        xs = out_proj(xs, y_rw, y_na, y_mb, cast_bf16(w_out, l, ROW_BLK), tab)
        h2 = norm_mod(xs, norm2_w[l].reshape(1, D), tab, 3)
        f = peer_ffn(h2, cast_bf16(pe_wq, l, ROW_BLK), pe_keys[l],
                     cast_bf16(pe_u, l, 1024), cast_bf16(pe_v, l, 1024))
        prev_tab = tab
    return final_norm(xs, f, prev_tab, final_norm_w.reshape(1, D))
```

```python
import functools
import math

import numpy as np
import jax
import jax.numpy as jnp
from jax import lax
from jax.experimental import pallas as pl
from jax.experimental.pallas import tpu as pltpu

D = 2048
NB = 4
T_LAT = 2048
T_CTX = 256
T_ALL = T_CTX + T_LAT
DEPTH = 4
GRID_W = 64
GRID_ROWS = T_LAT // GRID_W
HEAD = 64
RW_W = 512
RW_H = 8
LORA = 64
LORA_G = 128
W_DECAY_SCALE = 0.606531
RW_LN_EPS = 64e-5
NA_W = 512
NA_H = 8
NA_KH = 8
NA_KW = 16
MB_INNER = 1024
MB_H = 16
MB_G = 4
MB_R = 4
MB_N = 128
MB_CONV = 5
MB_CHUNK = 128
ROPE_BASE = 10000.0
PEER_HEADS = 8
N_KEYS = 128
PEER_TOPK = 16
NORM_EPS = 1e-6
NEG_INF = -1e30
RW_COLS = 3 * RW_W + 4 * LORA + LORA_G
NA_COLS = 3 * NA_W
MB_CONV_CH = MB_INNER + 2 * MB_G * MB_N
MB_COLS = MB_INNER + MB_CONV_CH + 2 * MB_H
IN_COLS = RW_COLS + NA_COLS + MB_COLS
NA_OFF = RW_COLS
MB_OFF = RW_COLS + NA_COLS
XBC_OFF = MB_OFF + MB_INNER
DT_OFF = XBC_OFF + MB_CONV_CH

LANES = 128
SUBLANES = 8
VMEM_LIMIT = 56 * 1024 * 1024

ROW_BLK = 256
N_ROW_BLK = T_ALL // ROW_BLK
BF16 = jnp.bfloat16
F32 = jnp.float32
HI = lax.Precision.HIGHEST


def _cparams(sem):
    return pltpu.CompilerParams(dimension_semantics=sem, vmem_limit_bytes=VMEM_LIMIT)


def _dot(a, b, precision=None):
    return jnp.dot(a, b, preferred_element_type=F32, precision=precision)


def _split_bf16(x):
    hi = x.astype(BF16)
    return hi, (x - hi.astype(F32)).astype(BF16)


def _dot_split(a, b):
    ah, al = _split_bf16(a)
    bh, bl = _split_bf16(b)
    return _dot(ah, bh) + _dot(al, bh) + _dot(ah, bl)


def _dot_exact_rhs(a, b_bf16):
    ah, al = _split_bf16(a)
    return _dot(ah, b_bf16) + _dot(al, b_bf16)


def _dot_nt(a, b, precision=None):
    return lax.dot_general(a, b, (((1,), (1,)), ((), ())), preferred_element_type=F32,
                           precision=precision)


def _cast_kernel(x_ref, o_ref):
    o_ref[...] = x_ref[...].astype(BF16)


def cast_bf16(w, layer, rows_blk):
    _, r, c = w.shape
    return pl.pallas_call(
        _cast_kernel,
        grid=(r // rows_blk,),
        in_specs=[pl.BlockSpec((None, rows_blk, c), lambda i: (layer, i, 0))],
        out_specs=pl.BlockSpec((rows_blk, c), lambda i: (i, 0)),
        out_shape=jax.ShapeDtypeStruct((r, c), BF16),
        compiler_params=_cparams(("arbitrary",)),
        name="cast_bf16",
    )(w)


ADA_TN = 1024


def _ada_kernel(c_ref, w_ref, b_ref, o_ref):
    c = c_ref[...]
    s = c * jax.nn.sigmoid(c)
    o_ref[0] = _dot(s, w_ref[0], precision=HI) + b_ref[0]


def ada_rows(cond, ada_w, ada_b):
    nl = ada_w.shape[0]
    return pl.pallas_call(
        _ada_kernel,
        grid=(nl, 6 * D // ADA_TN),
        in_specs=[pl.BlockSpec((SUBLANES, D), lambda l, j: (0, 0)),
                  pl.BlockSpec((1, D, ADA_TN), lambda l, j: (l, 0, j)),
                  pl.BlockSpec((1, 1, ADA_TN), lambda l, j: (l, 0, j))],
        out_specs=pl.BlockSpec((1, SUBLANES, ADA_TN), lambda l, j: (l, 0, j)),
        out_shape=jax.ShapeDtypeStruct((nl, SUBLANES, 6 * D), F32),
        compiler_params=_cparams(("arbitrary", "arbitrary")),
        name="ada_rows",
    )(cond, ada_w, ada_b.reshape(nl, 1, 6 * D))


def _mod_index(b, i):
    return 2 * b + jnp.minimum(i, 1)


def _norm_mod_kernel(which, x_ref, nw_ref, tab_ref, o_ref):
    x = x_ref[0]
    ms = jnp.mean(x * x, axis=-1, keepdims=True)
    y = x * lax.rsqrt(ms + NORM_EPS) * nw_ref[...]
    shift = tab_ref[0, which:which + 1, :]
    scale = tab_ref[0, which + 1:which + 2, :]
    o_ref[0] = (y * (1.0 + scale) + shift).astype(BF16)


def norm_mod(x, nw, tab, which):
    nb = x.shape[0]
    return pl.pallas_call(
        functools.partial(_norm_mod_kernel, which),
        grid=(nb, N_ROW_BLK),
        in_specs=[pl.BlockSpec((1, ROW_BLK, D), lambda b, i: (b, i, 0)),
                  pl.BlockSpec((1, D), lambda b, i: (0, 0)),
                  pl.BlockSpec((1, 6, D), lambda b, i: (_mod_index(b, i), 0, 0))],
        out_specs=pl.BlockSpec((1, ROW_BLK, D), lambda b, i: (b, i, 0)),
        out_shape=jax.ShapeDtypeStruct(x.shape, BF16),
        compiler_params=_cparams(("arbitrary", "arbitrary")),
        name="norm_mod",
    )(x, nw, tab)


def _resid_norm_mod_kernel(x_ref, f_ref, ptab_ref, nw_ref, tab_ref, xo_ref, o_ref):
    x = x_ref[0] + ptab_ref[0, 5:6, :] * f_ref[0]
    xo_ref[0] = x
    ms = jnp.mean(x * x, axis=-1, keepdims=True)
    y = x * lax.rsqrt(ms + NORM_EPS) * nw_ref[...]
    o_ref[0] = (y * (1.0 + tab_ref[0, 1:2, :]) + tab_ref[0, 0:1, :]).astype(BF16)


def resid_norm_mod(x, f, prev_tab, nw, tab):
    nb = x.shape[0]
    row = lambda b, i: (b, i, 0)
    mod = lambda b, i: (_mod_index(b, i), 0, 0)
    return pl.pallas_call(
        _resid_norm_mod_kernel,
        grid=(nb, N_ROW_BLK),
        in_specs=[pl.BlockSpec((1, ROW_BLK, D), row),
                  pl.BlockSpec((1, ROW_BLK, D), row),
                  pl.BlockSpec((1, 6, D), mod),
                  pl.BlockSpec((1, D), lambda b, i: (0, 0)),
                  pl.BlockSpec((1, 6, D), mod)],
        out_specs=[pl.BlockSpec((1, ROW_BLK, D), row), pl.BlockSpec((1, ROW_BLK, D), row)],
        out_shape=[jax.ShapeDtypeStruct(x.shape, F32), jax.ShapeDtypeStruct(x.shape, BF16)],
        input_output_aliases={0: 0},
        compiler_params=_cparams(("arbitrary", "arbitrary")),
        name="resid_norm_mod",
    )(x, f, prev_tab, nw, tab)


def _final_norm_kernel(x_ref, f_ref, ptab_ref, nw_ref, o_ref):
    x = x_ref[0] + ptab_ref[0, 5:6, :] * f_ref[0]
    ms = jnp.mean(x * x, axis=-1, keepdims=True)
    o_ref[0] = x * lax.rsqrt(ms + NORM_EPS) * nw_ref[...]


def final_norm(x, f, prev_tab, nw):
    nb = x.shape[0]
    lat = lambda b, i: (b, i + 1, 0)
    return pl.pallas_call(
        _final_norm_kernel,
        grid=(nb, T_LAT // ROW_BLK),
        in_specs=[pl.BlockSpec((1, ROW_BLK, D), lat),
                  pl.BlockSpec((1, ROW_BLK, D), lat),
                  pl.BlockSpec((1, 6, D), lambda b, i: (2 * b + 1, 0, 0)),
                  pl.BlockSpec((1, D), lambda b, i: (0, 0))],
        out_specs=pl.BlockSpec((1, ROW_BLK, D), lambda b, i: (b, i, 0)),
        out_shape=jax.ShapeDtypeStruct((nb, T_LAT, D), F32),
        compiler_params=_cparams(("arbitrary", "arbitrary")),
        name="final_norm",
    )(x, f, prev_tab, nw)


MM_TM = 1024
MM_TN = 1024


def _mm_kernel(a_ref, w_ref, o_ref):
    o_ref[...] = _dot(a_ref[...], w_ref[...])


def matmul(a, w):
    m, k = a.shape
    n = w.shape[1]
    tm = MM_TM if m % MM_TM == 0 else ROW_BLK
    return pl.pallas_call(
        _mm_kernel,
        grid=(m // tm, pl.cdiv(n, MM_TN)),
        in_specs=[pl.BlockSpec((tm, k), lambda i, j: (i, 0)),
                  pl.BlockSpec((k, MM_TN), lambda i, j: (0, j))],
        out_specs=pl.BlockSpec((tm, MM_TN), lambda i, j: (i, j)),
        out_shape=jax.ShapeDtypeStruct((m, n), F32),
        compiler_params=_cparams(("arbitrary", "arbitrary")),
        name="matmul",
    )(a, w)


def _out_proj_kernel(x_ref, rw_ref, na_ref, mb_ref, w_ref, tab_ref, o_ref):
    acc = _dot(rw_ref[0], w_ref[0:RW_W, :])
    acc += _dot(na_ref[0], w_ref[RW_W:RW_W + NA_W, :])
    acc += _dot(mb_ref[0], w_ref[RW_W + NA_W:, :])
    o_ref[0] = x_ref[0] + tab_ref[0, 2:3, :] * acc


def out_proj(x, y_rw, y_na, y_mb, w, tab):
    nb = x.shape[0]
    row = lambda b, i: (b, i, 0)
    return pl.pallas_call(
        _out_proj_kernel,
        grid=(nb, N_ROW_BLK),
        in_specs=[pl.BlockSpec((1, ROW_BLK, D), row),
                  pl.BlockSpec((1, ROW_BLK, RW_W), row),
                  pl.BlockSpec((1, ROW_BLK, NA_W), row),
                  pl.BlockSpec((1, ROW_BLK, MB_INNER), row),
                  pl.BlockSpec((D, D), lambda b, i: (0, 0)),
                  pl.BlockSpec((1, 6, D), lambda b, i: (_mod_index(b, i), 0, 0))],
        out_specs=pl.BlockSpec((1, ROW_BLK, D), row),
        out_shape=jax.ShapeDtypeStruct(x.shape, F32),
        input_output_aliases={0: 0},
        compiler_params=_cparams(("arbitrary", "arbitrary")),
        name="out_proj",
    )(x, y_rw, y_na, y_mb, w, tab)


TOPK_TT = 128


CAND_COUNTS = (16, 8, 5, 4, 3, 2, 2, 2)
CAND_ROWS = 16 + 8 * 7 + 8
N_HALVES = 2 * PEER_HEADS
N_SLOTS = PEER_HEADS * PEER_TOPK


def _first_max(s, rows, n):
    m = jnp.max(s, axis=0, keepdims=True)
    pos = jnp.min(jnp.where(s == m, rows, float(n)), axis=0, keepdims=True)
    return m, pos


def _peer_topk_kernel(q_ref, keys_ref, i1_ref, i2_ref, g_ref,
                      sc_ref, val_ref, idx_ref, cand_ref, c1_ref, c2_ref, best_ref, e1_ref, e2_ref):
    t = TOPK_TT
    for hp in range(N_HALVES):
        c0 = hp * N_KEYS
        qs = q_ref[:, c0:c0 + N_KEYS].astype(BF16)
        sc_ref[hp] = _dot_nt(keys_ref[hp // 2, hp % 2].astype(BF16), qs)

    rows = lax.broadcasted_iota(jnp.int32, (N_KEYS, t), 0).astype(F32)

    def stage1(r, carry):
        for hp in range(N_HALVES):
            s = sc_ref[hp]
            m, pos = _first_max(s, rows, N_KEYS)
            val_ref[hp, pl.ds(r, 1), :] = m
            idx_ref[hp, pl.ds(r, 1), :] = pos
            sc_ref[hp] = jnp.where(rows == pos, -jnp.inf, s)
        return carry

    lax.fori_loop(0, PEER_TOPK, stage1, 0)

    row8 = lax.broadcasted_iota(jnp.int32, (SUBLANES, t), 0)
    for h in range(PEER_HEADS):
        v1, v2 = val_ref[2 * h], val_ref[2 * h + 1]
        k1, k2 = idx_ref[2 * h], idx_ref[2 * h + 1]
        cand = [v1[0:1] + v2]
        c1 = [jnp.broadcast_to(k1[0:1], (PEER_TOPK, t))]
        c2 = [k2]
        for r1 in range(1, SUBLANES):
            cand.append(jnp.where(row8 < CAND_COUNTS[r1], v1[r1:r1 + 1] + v2[0:SUBLANES], -jnp.inf))
            c1.append(jnp.broadcast_to(k1[r1:r1 + 1], (SUBLANES, t)))
            c2.append(k2[0:SUBLANES])
        cand.append(v1[SUBLANES:] + v2[0:1])
        c1.append(k1[SUBLANES:])
        c2.append(jnp.broadcast_to(k2[0:1], (SUBLANES, t)))
        cand_ref[h] = jnp.concatenate(cand, axis=0)
        c1_ref[h] = jnp.concatenate(c1, axis=0)
        c2_ref[h] = jnp.concatenate(c2, axis=0)

    crow = lax.broadcasted_iota(jnp.int32, (CAND_ROWS, t), 0).astype(F32)

    def stage2(r, carry):
        for h in range(PEER_HEADS):
            cd = cand_ref[h]
            m, pos = _first_max(cd, crow, CAND_ROWS)
            sel = crow == pos
            slot = pl.ds(h * PEER_TOPK + r, 1)
            best_ref[slot, :] = m
            e1_ref[slot, :] = jnp.sum(jnp.where(sel, c1_ref[h], 0.0), axis=0, keepdims=True)
            e2_ref[slot, :] = jnp.sum(jnp.where(sel, c2_ref[h], 0.0), axis=0, keepdims=True)
            cand_ref[h] = jnp.where(sel, -jnp.inf, cd)
        return carry

    lax.fori_loop(0, PEER_TOPK, stage2, 0)

    for h in range(PEER_HEADS):
        sl = slice(h * PEER_TOPK, (h + 1) * PEER_TOPK)
        b = best_ref[sl, :]
        ex = jnp.exp(b - jnp.max(b, axis=0, keepdims=True))
        best_ref[sl, :] = ex / jnp.sum(ex, axis=0, keepdims=True)
    i1_ref[...] = e1_ref[...].T
    i2_ref[...] = e2_ref[...].T
    g_ref[...] = best_ref[...].T


def peer_topk(q, keys):
    n = q.shape[0]
    t = TOPK_TT
    out = jax.ShapeDtypeStruct((n, N_SLOTS), F32)
    spec = pl.BlockSpec((t, N_SLOTS), lambda i: (i, 0))
    return pl.pallas_call(
        _peer_topk_kernel,
        grid=(n // t,),
        in_specs=[pl.BlockSpec((t, D), lambda i: (i, 0)),
                  pl.BlockSpec(keys.shape, lambda i: (0, 0, 0, 0))],
        out_specs=[spec, spec, spec],
        out_shape=[out, out, out],
        scratch_shapes=[pltpu.VMEM((N_HALVES, N_KEYS, t), F32),
                        pltpu.VMEM((N_HALVES, PEER_TOPK, t), F32),
                        pltpu.VMEM((N_HALVES, PEER_TOPK, t), F32),
                        pltpu.VMEM((PEER_HEADS, CAND_ROWS, t), F32),
                        pltpu.VMEM((PEER_HEADS, CAND_ROWS, t), F32),
                        pltpu.VMEM((PEER_HEADS, CAND_ROWS, t), F32),
                        pltpu.VMEM((N_SLOTS, t), F32),
                        pltpu.VMEM((N_SLOTS, t), F32),
                        pltpu.VMEM((N_SLOTS, t), F32)],
        compiler_params=_cparams(("arbitrary",)),
        name="peer_topk",
    )(q, keys)


EXP_TM = 512
EXP_J = 8
EXP_BLK = EXP_J * N_KEYS
N_EXP_BLK = N_KEYS * N_KEYS // EXP_BLK
G_PITCH = N_KEYS // 2 + SUBLANES
SQRT_HALF = 0.7071067811865476


def _peer_expert_kernel(h_ref, i1_ref, i2_ref, g_ref, u_ref, v_ref, o_ref, gs_ref, w_ref):
    jj = pl.program_id(1)

    @pl.when(jj == 0)
    def _():
        o_ref[...] = jnp.zeros_like(o_ref)
        w_ref[...] = jnp.zeros_like(w_ref)
        rows = lax.broadcasted_iota(jnp.int32, (N_KEYS, N_KEYS), 0).astype(F32)

        def tok(t, carry):
            i1r = i1_ref[pl.ds(t, 1), :]
            i2r = i2_ref[pl.ds(t, 1), :]
            gr = g_ref[pl.ds(t, 1), :]
            at = jnp.where(rows == i1r, gr, 0.0).astype(BF16)
            bt = jnp.where(rows == i2r, 1.0, 0.0).astype(BF16)
            gm = _dot_nt(at, bt).astype(BF16)
            start = pl.multiple_of(t * G_PITCH, SUBLANES)
            gs_ref[pl.ds(start, N_KEYS // 2), :] = pltpu.bitcast(gm, jnp.uint32)
            return carry

        lax.fori_loop(0, EXP_TM, tok, 0, unroll=32)

    slot = jj % 2
    o_ref[...] += _dot(w_ref[1 - slot], v_ref[...])

    jb = jnp.minimum(jj, N_EXP_BLK - 1)
    s = _dot_nt(h_ref[...], u_ref[...])
    cols = []
    for c in range(EXP_J // 2):
        packed = gs_ref[pl.ds((EXP_J // 2) * jb + c, EXP_TM, stride=G_PITCH), :]
        cols.append(lax.bitcast_convert_type(packed << 16, F32))
        cols.append(lax.bitcast_convert_type(packed & jnp.uint32(0xFFFF0000), F32))
    g = jnp.concatenate(cols, axis=1)
    act = 0.5 * s * (1.0 + lax.erf(s * SQRT_HALF))
    w_ref[slot] = (g * act).astype(BF16)


def peer_expert(h, i1, i2, g, u, v):
    n = h.shape[0]
    row = lambda i, j: (i, 0)
    sel = pl.BlockSpec((EXP_TM, N_SLOTS), row)
    return pl.pallas_call(
        _peer_expert_kernel,
        grid=(n // EXP_TM, N_EXP_BLK + 1),
        in_specs=[pl.BlockSpec((EXP_TM, D), row), sel, sel, sel,
                  pl.BlockSpec((EXP_BLK, D), lambda i, j: (jnp.minimum(j, N_EXP_BLK - 1), 0)),
                  pl.BlockSpec((EXP_BLK, D), lambda i, j: (jnp.maximum(j - 1, 0), 0))],
        out_specs=pl.BlockSpec((EXP_TM, D), row),
        out_shape=jax.ShapeDtypeStruct((n, D), F32),
        scratch_shapes=[pltpu.VMEM((EXP_TM * G_PITCH, N_KEYS), jnp.uint32),
                        pltpu.VMEM((2, EXP_TM, EXP_BLK), BF16)],
        compiler_params=_cparams(("arbitrary", "arbitrary")),
        name="peer_expert",
    )(h, i1, i2, g, u, v)


def peer_ffn(h, wq, keys, u, v):
    nb = h.shape[0]
    hf = h.reshape(nb * T_ALL, D)
    i1, i2, g = peer_topk(matmul(hf, wq), keys)
    return peer_expert(hf, i1, i2, g, u, v).reshape(nb, T_ALL, D)


def _head_ones(width, group):
    r = np.arange(width) // group
    return jnp.asarray((r[:, None] == r[None, :]).astype(np.float32)).astype(BF16)


def _store_head_pairs(o_ref, x, y):
    for h in range(RW_H):
        ln = slice(h * HEAD, (h + 1) * HEAD)
        o_ref[pl.ds(h, ROW_BLK, stride=RW_H), :] = jnp.concatenate([x[:, ln], y[:, ln]], axis=1)


def _rw_pre_kernel(z_ref, hp_ref, hn_ref, mu_ref, w0_ref, w2_ref, a0_ref, a2_ref, g2_ref,
                   kk_ref, ka_ref, ones_ref,
                   r_ref, k_ref, v_ref, g_ref, rv_ref, awf_ref, awb_ref, bkf_ref, bkb_ref):
    z = z_ref[0]
    row = lax.broadcasted_iota(jnp.int32, (ROW_BLK, 1), 0)
    prev = jnp.where(row == 0, hp_ref[0, 0], pltpu.roll(z, 1, 0))
    nxt = jnp.where(row == ROW_BLK - 1, hn_ref[0, 0], pltpu.roll(z, ROW_BLK - 1, 0))
    zs = z + mu_ref[0:1, :] * (prev - z) + mu_ref[1:2, :] * (nxt - z)
    r = zs[:, 0:RW_W]
    k = zs[:, RW_W:2 * RW_W]
    v = zs[:, 2 * RW_W:3 * RW_W]
    o = 3 * RW_W
    lw = jnp.tanh(zs[:, o:o + 2 * LORA])
    la = zs[:, o + 2 * LORA:o + 4 * LORA]
    lg = jax.nn.sigmoid(zs[:, o + 4 * LORA:o + 4 * LORA + LORA_G])
    r_ref[0] = r
    k_ref[0] = k
    v_ref[0] = v
    g_ref[0] = _dot_split(lg, g2_ref[...])
    kkr = k * kk_ref[...]
    ss = _dot_exact_rhs(kkr * kkr, ones_ref[...])
    kkn = kkr / jnp.maximum(jnp.sqrt(ss), 1e-12)
    _store_head_pairs(rv_ref, r, v)
    for d, (aw_ref, bk_ref) in enumerate(((awf_ref, bkf_ref), (awb_ref, bkb_ref))):
        dec = w0_ref[d:d + 1, :] + _dot_split(lw, w2_ref[d])
        a = jax.nn.sigmoid(a0_ref[d:d + 1, :] + _dot_split(la, a2_ref[d]))
        _store_head_pairs(aw_ref, -kkn, jnp.exp(-W_DECAY_SCALE * jax.nn.sigmoid(dec)))
        _store_head_pairs(bk_ref, kkn * a, k * (1.0 + (a - 1.0) * ka_ref[...]))


def rw_pre(z, mu, w0, w2, a0, a2, g2, kk, ka):
    nb = z.shape[0]
    zero = jnp.zeros((nb, 1, RW_COLS), F32)
    last = z[:, ROW_BLK - 1::ROW_BLK, :RW_COLS]
    first = z[:, ::ROW_BLK, :RW_COLS]
    halo_prev = jnp.concatenate([zero, zero, last[:, 1:N_ROW_BLK - 1]], axis=1)
    halo_next = jnp.concatenate([zero, first[:, 2:], zero], axis=1)
    halo_prev = halo_prev.reshape(nb, N_ROW_BLK, 1, RW_COLS)
    halo_next = halo_next.reshape(nb, N_ROW_BLK, 1, RW_COLS)
    zpad = jnp.zeros((LORA, RW_W), F32)
    w2p = jnp.stack([jnp.concatenate([w2[0], zpad]), jnp.concatenate([zpad, w2[1]])])
    a2p = jnp.stack([jnp.concatenate([a2[0], zpad]), jnp.concatenate([zpad, a2[1]])])
    row = lambda b, i: (b, i, 0)
    full = lambda shape: pl.BlockSpec(shape, lambda b, i: (0,) * len(shape))
    out = jax.ShapeDtypeStruct((nb, T_ALL, RW_W), F32)
    ospec = pl.BlockSpec((1, ROW_BLK, RW_W), row)
    pout = jax.ShapeDtypeStruct((nb, T_ALL * RW_H, LANES), F32)
    pspec = pl.BlockSpec((None, ROW_BLK * RW_H, LANES), row)
    return pl.pallas_call(
        _rw_pre_kernel,
        grid=(nb, N_ROW_BLK),
        in_specs=[pl.BlockSpec((1, ROW_BLK, RW_COLS), row),
                  pl.BlockSpec((1, 1, 1, RW_COLS), lambda b, i: (b, i, 0, 0)),
                  pl.BlockSpec((1, 1, 1, RW_COLS), lambda b, i: (b, i, 0, 0)),
                  full((2, RW_COLS)), full((2, RW_W)), full((2, 2 * LORA, RW_W)),
                  full((2, RW_W)), full((2, 2 * LORA, RW_W)), full((LORA_G, RW_W)),
                  full((1, RW_W)), full((1, RW_W)), full((RW_W, RW_W))],
        out_specs=[ospec] * 4 + [pspec] * 5,
        out_shape=[out] * 4 + [pout] * 5,
        compiler_params=_cparams(("arbitrary", "arbitrary")),
        name="rw_pre",
    )(z, halo_prev, halo_next, mu, w0, w2p, a0, a2p, g2, kk.reshape(1, RW_W),
      ka.reshape(1, RW_W), _head_ones(RW_W, HEAD))


SCAN_TB = 32
SCAN_I = HEAD // 2


SCAN_NBLK = T_ALL // SCAN_TB
SCAN_CTX_BLK = T_CTX // SCAN_TB


def _rw_scan_kernel(awf_ref, bkf_ref, rvf_ref, awb_ref, bkb_ref, rvb_ref, yf_ref, yb_ref,
                    s_ref, sa_ref, gam_ref, ybuf_ref, *tiles):
    @pl.when(pl.program_id(0) == 0)
    def _():
        s_ref[...] = jnp.zeros_like(s_ref)

    low_half = lax.broadcasted_iota(jnp.int32, (1, LANES), 1) < LANES // 2
    sub = lax.broadcasted_iota(jnp.int32, (SUBLANES, LANES), 0)
    sets = (tiles[:5], tiles[5:])

    def prep(t, tile_set):
        a_s, b_s, k_s, r_s, v_s = tile_set
        rf = pl.ds(pl.multiple_of(t * RW_H, RW_H), RW_H)
        rb = pl.ds(pl.multiple_of((SCAN_TB - 1 - t) * RW_H, RW_H), RW_H)

        def transposed(f_ref, b_ref):
            rows = [f_ref[b, rf, :] for b in range(NB)] + [b_ref[b, rb, :] for b in range(NB)]
            return jnp.concatenate(rows + rows, axis=0).T

        aw = transposed(awf_ref, awb_ref)
        gam_prev = gam_ref[...]
        gam = gam_prev * aw[HEAD:]
        gam_ref[...] = gam
        inv = 1.0 / gam
        a_s[...] = aw[:HEAD] * gam_prev
        bk = transposed(bkf_ref, bkb_ref)
        b_s[...] = bk[:HEAD] * inv
        k_s[...] = bk[HEAD:] * inv
        rv = transposed(rvf_ref, rvb_ref)
        r_s[...] = rv[:HEAD] * gam
        v_s[...] = jnp.where(low_half, rv[HEAD:HEAD + SCAN_I], rv[HEAD + SCAN_I:])

    def fold(x):
        return jnp.sum(x.reshape(SUBLANES, SUBLANES, LANES), axis=0)

    def reduce8(p):
        z = [jnp.where(sub < 4, p[k], p[k + 4]) + pltpu.roll(jnp.where(sub < 4, p[k + 4], p[k]), 4, 0)
             for k in range(4)]
        even2 = (sub & 2) == 0
        v = [jnp.where(even2, z[k] + pltpu.roll(z[k], 6, 0), z[k + 2] + pltpu.roll(z[k + 2], 2, 0))
             for k in range(2)]
        return jnp.where((sub & 1) == 0, v[0] + pltpu.roll(v[0], 7, 0), v[1] + pltpu.roll(v[1], 1, 0))

    def step(t, tile_set):
        a_s, b_s, k_s, r_s, v_s = tile_set
        a = a_s[...]
        for c in range(SCAN_I // SUBLANES):
            rows = range(c * SUBLANES, (c + 1) * SUBLANES)
            sa_ref[c * SUBLANES:(c + 1) * SUBLANES, :] = reduce8([fold(s_ref[i] * a) for i in rows])
        b = b_s[...]
        k = k_s[...]
        r = r_s[...]
        for c in range(SCAN_I // SUBLANES):
            parts = []
            for i in range(c * SUBLANES, (c + 1) * SUBLANES):
                sn = s_ref[i] + sa_ref[pl.ds(i, 1), :] * b + v_s[pl.ds(i, 1), :] * k
                s_ref[i] = sn
                parts.append(fold(sn * r))
            ybuf_ref[pl.ds(pl.multiple_of(t * SCAN_I + c * SUBLANES, SUBLANES), SUBLANES), :] = reduce8(parts)

    gam_ref[...] = jnp.ones_like(gam_ref)
    prep(0, sets[0])

    def two_steps(u, carry):
        t = 2 * u
        prep(t + 1, sets[1])
        step(t, sets[0])
        prep(t + 2, sets[0])
        step(t + 1, sets[1])
        return carry

    lax.fori_loop(0, SCAN_TB // 2 - 1, two_steps, 0)
    prep(SCAN_TB - 1, sets[1])
    step(SCAN_TB - 2, sets[0])
    step(SCAN_TB - 1, sets[1])

    gam_end = gam_ref[...]
    for i in range(SCAN_I):
        s_ref[i] = s_ref[i] * gam_end

    grp = lax.broadcasted_iota(jnp.int32, (1, LANES), 1) // SCAN_I

    def emit(q, carry):
        tile = ybuf_ref[pl.ds(pl.multiple_of(q * LANES, LANES), LANES), :].T
        for t4 in range(LANES // SCAN_I):
            t = q * (LANES // SCAN_I) + t4
            for b in range(NB):
                out = jnp.zeros((RW_H, LANES), F32)
                for x in range(2):
                    for d in range(2):
                        r0 = x * (LANES // 2) + d * (LANES // 4) + b * RW_H
                        dst = d * 2 + x
                        piece = pltpu.roll(tile[r0:r0 + RW_H, :], ((dst - t4) % 4) * SCAN_I, 1)
                        out = jnp.where(grp == dst, piece, out)
                yf_ref[b, pl.ds(pl.multiple_of(t * RW_H, RW_H), RW_H), :] = out
                yb_ref[b, pl.ds(pl.multiple_of((SCAN_TB - 1 - t) * RW_H, RW_H), RW_H), :] = out
        return carry

    lax.fori_loop(0, SCAN_TB * SCAN_I // LANES, emit, 0, unroll=4)


def _mirror_block(g):
    return jnp.where(g < SCAN_CTX_BLK, SCAN_CTX_BLK - 1 - g, SCAN_NBLK + SCAN_CTX_BLK - 1 - g)


def rw_scan(awf, bkf, awb, bkb, rv):
    assert awf.shape[0] == NB
    fspec = pl.BlockSpec((NB, SCAN_TB * RW_H, LANES), lambda g: (0, g, 0))
    bspec = pl.BlockSpec((NB, SCAN_TB * RW_H, LANES), lambda g: (0, _mirror_block(g), 0))
    out = jax.ShapeDtypeStruct((NB, T_ALL * RW_H, LANES), F32)
    tile_set = [pltpu.VMEM((HEAD, LANES), F32)] * 4 + [pltpu.VMEM((SCAN_I, LANES), F32)]
    return pl.pallas_call(
        _rw_scan_kernel,
        grid=(SCAN_NBLK,),
        in_specs=[fspec] * 3 + [bspec] * 3,
        out_specs=[fspec, bspec],
        out_shape=[out, out],
        scratch_shapes=[pltpu.VMEM((SCAN_I, HEAD, LANES), F32),
                        pltpu.VMEM((SCAN_I, LANES), F32),
                        pltpu.VMEM((HEAD, LANES), F32),
                        pltpu.VMEM((SCAN_TB * SCAN_I, LANES), F32)] + tile_set + tile_set,
        compiler_params=_cparams(("arbitrary",)),
        name="rw_scan",
    )(awf, bkf, rv, awb, bkb, rv)


def _rw_post_kernel(yf_ref, yb_ref, r_ref, k_ref, v_ref, g_ref, rk_ref, lw_ref, lb_ref, ones_ref,
                    o_ref):
    low = lax.broadcasted_iota(jnp.int32, (1, LANES), 1) < HEAD
    cols = []
    for hp in range(RW_H // 2):
        pair = []
        for h in (2 * hp, 2 * hp + 1):
            yf = yf_ref[pl.ds(h, ROW_BLK, stride=RW_H), :]
            yb = yb_ref[pl.ds(h, ROW_BLK, stride=RW_H), :]
            if h % 2 == 0:
                pair.append(yf + pltpu.roll(yb, HEAD, 1))
            else:
                pair.append(pltpu.roll(yf, HEAD, 1) + yb)
        cols.append(jnp.where(low, pair[0], pair[1]))
    y = jnp.concatenate(cols, axis=1)
    ones = ones_ref[...]
    mean = _dot_exact_rhs(y, ones) * (1.0 / HEAD)
    yc = y - mean
    var = _dot_exact_rhs(yc * yc, ones) * (1.0 / HEAD)
    yn = yc * lax.rsqrt(var + RW_LN_EPS) * lw_ref[...] + lb_ref[...]
    bonus = _dot_exact_rhs(r_ref[0] * k_ref[0] * rk_ref[...], ones) * v_ref[0]
    o_ref[0] = ((yn + bonus) * g_ref[0]).astype(BF16)


def rw_post(yf, yb, r, k, v, g, rk, ln_w, ln_b):
    nb = yf.shape[0]
    row = lambda b, i: (b, i, 0)
    spec = pl.BlockSpec((1, ROW_BLK, RW_W), row)
    vec = pl.BlockSpec((1, RW_W), lambda b, i: (0, 0))
    yspec = pl.BlockSpec((None, ROW_BLK * RW_H, LANES), row)
    return pl.pallas_call(
        _rw_post_kernel,
        grid=(nb, N_ROW_BLK),
        in_specs=[yspec] * 2 + [spec] * 4 + [vec] * 3 + [pl.BlockSpec((RW_W, RW_W), lambda b, i: (0, 0))],
        out_specs=spec,
        out_shape=jax.ShapeDtypeStruct((nb, T_ALL, RW_W), BF16),
        compiler_params=_cparams(("arbitrary", "arbitrary")),
        name="rw_post",
    )(yf, yb, r, k, v, g, rk.reshape(1, RW_W), ln_w.reshape(1, RW_W), ln_b.reshape(1, RW_W),
      _head_ones(RW_W, HEAD))


def rwkv_mixer(z, mu, w0, w2, a0, a2, g2, kk, ka, rk, ln_w, ln_b):
    r, k, v, g, rv, awf, awb, bkf, bkb = rw_pre(z, mu, w0, w2, a0, a2, g2, kk, ka)
    yf, yb = rw_scan(awf, bkf, awb, bkb, rv)
    return rw_post(yf, yb, r, k, v, g, rk, ln_w, ln_b)


NA_QROWS = ROW_BLK // GRID_W
NA_SLAB = NA_KH + NA_QROWS - 1
NA_SLAB_T = NA_SLAB * GRID_W
NA_SCALE = HEAD ** -0.5


def _na_bias_index():
    a = np.arange(NA_QROWS)[:, None]
    u = np.arange(NA_SLAB)[None, :]
    idx_r, valid = [], []
    for r0, u0 in ((0, 0), (NA_QROWS, 0), (GRID_ROWS - NA_QROWS, GRID_ROWS - NA_SLAB)):
        r = r0 + a
        kr = u0 + u
        kr0 = np.clip(r - NA_KH // 2, 0, GRID_ROWS - NA_KH)
        valid.append((kr >= kr0) & (kr < kr0 + NA_KH))
        idx_r.append(np.clip(kr - r + NA_KH - 1, 0, 2 * NA_KH - 2))
    idx_r = np.stack(idx_r)
    valid = np.stack(valid)[:, :, None, :, None]
    qc = np.arange(GRID_W)[:, None]
    kc = np.arange(GRID_W)[None, :]
    c0 = np.clip(qc - NA_KW // 2, 0, GRID_W - NA_KW)
    in_win = ((kc >= c0) & (kc < c0 + NA_KW))[None, None, :, None, :]
    idx_c = np.clip(kc - qc + NA_KW - 1, 0, 2 * NA_KW - 2)
    col_onehot = (idx_c[None] == np.arange(2 * NA_KW - 1)[:, None, None]).astype(np.float32)
    shape = (3, NA_QROWS, GRID_W, NA_SLAB, GRID_W)
    return idx_r, col_onehot, np.broadcast_to(valid & in_win, shape)


def na_bias_tables(rpb):
    nl = rpb.shape[0]
    idx_r, col_onehot, mask = _na_bias_index()
    rows = rpb[:, :, idx_r.reshape(-1), :].reshape(nl, NA_H, 3, NA_QROWS, NA_SLAB, 2 * NA_KW - 1)
    b = jnp.einsum('lhvauc,cqk->lhvaquk', rows, jnp.asarray(col_onehot), precision=HI)
    b = jnp.where(mask[None, None], b, NEG_INF)
    return b.reshape(nl, NA_H, 3, ROW_BLK, NA_SLAB_T)


def _na_kernel(q_ref, k_ref, v_ref, bias_ref, o_ref):
    qi = pl.program_id(2)

    @pl.when(qi == 0)
    def _():
        ys = []
        for hh in range(2):
            ln = slice(hh * HEAD, (hh + 1) * HEAD)
            q = (q_ref[0, :, ln] * NA_SCALE).astype(BF16)
            s = _dot_nt(q, k_ref[0, 0:T_CTX, ln].astype(BF16))
            p = jnp.exp(s - jnp.max(s, axis=-1, keepdims=True))
            y = _dot(p.astype(BF16), v_ref[0, 0:T_CTX, ln].astype(BF16))
            ys.append(y / jnp.sum(p, axis=-1, keepdims=True))
        o_ref[0] = jnp.concatenate(ys, axis=1).astype(BF16)

    @pl.when(qi > 0)
    def _():
        u0 = jnp.clip(NA_QROWS * (qi - 1) - NA_KH // 2, 0, GRID_ROWS - NA_SLAB)
        start = pl.multiple_of(T_CTX + GRID_W * u0, GRID_W)
        ys = []
        for hh in range(2):
            ln = slice(hh * HEAD, (hh + 1) * HEAD)
            q = (q_ref[0, :, ln] * NA_SCALE).astype(BF16)
            sc = _dot_nt(q, k_ref[0, 0:T_CTX, ln].astype(BF16))
            sw = _dot_nt(q, k_ref[0, pl.ds(start, NA_SLAB_T), ln].astype(BF16)) + bias_ref[hh, 0]
            m = jnp.maximum(jnp.max(sc, axis=-1, keepdims=True), jnp.max(sw, axis=-1, keepdims=True))
            pc = jnp.exp(sc - m)
            pw = jnp.exp(sw - m)
            y = (_dot(pw.astype(BF16), v_ref[0, pl.ds(start, NA_SLAB_T), ln].astype(BF16))
                 + _dot(pc.astype(BF16), v_ref[0, 0:T_CTX, ln].astype(BF16)))
            den = jnp.sum(pc, axis=-1, keepdims=True) + jnp.sum(pw, axis=-1, keepdims=True)
            ys.append(y / den)
        o_ref[0] = jnp.concatenate(ys, axis=1).astype(BF16)


def natten_mixer(z, bias, layer):
    nb = z.shape[0]
    qb, kb, vb = (NA_OFF // LANES, (NA_OFF + NA_W) // LANES, (NA_OFF + 2 * NA_W) // LANES)
    n_blk = N_ROW_BLK - 1

    def bias_idx(b, hp, qi):
        var = jnp.where(qi <= 1, 0, jnp.where(qi == n_blk, 2, 1))
        return (layer, hp, var, 0, 0)

    return pl.pallas_call(
        _na_kernel,
        grid=(nb, NA_H // 2, N_ROW_BLK),
        in_specs=[pl.BlockSpec((1, ROW_BLK, LANES), lambda b, hp, qi: (b, qi, qb + hp)),
                  pl.BlockSpec((1, T_ALL, LANES), lambda b, hp, qi: (b, 0, kb + hp)),
                  pl.BlockSpec((1, T_ALL, LANES), lambda b, hp, qi: (b, 0, vb + hp)),
                  pl.BlockSpec((None, 2, 1, ROW_BLK, NA_SLAB_T), bias_idx)],
        out_specs=pl.BlockSpec((1, ROW_BLK, LANES), lambda b, hp, qi: (b, qi, hp)),
        out_shape=jax.ShapeDtypeStruct((nb, T_ALL, NA_W), BF16),
        compiler_params=_cparams(("arbitrary", "arbitrary", "arbitrary")),
        name="natten",
    )(z, z, z, bias)


ROPE_NF = MB_N // 4


def rope_tables():
    pos = np.arange(T_LAT)
    inv = ROPE_BASE ** (-np.arange(ROPE_NF, dtype=np.float32) / ROPE_NF)
    lane = np.arange(MB_N)
    p = np.where(lane[None, :] < MB_N // 2, (pos // GRID_W)[:, None], (pos % GRID_W)[:, None])
    ang = p.astype(np.float32) * inv[lane % ROPE_NF][None, :]
    sign = np.where((lane % (2 * ROPE_NF)) < ROPE_NF, -1.0, 1.0)[None, :]
    cos = np.concatenate([np.ones((T_CTX, MB_N), np.float32), np.cos(ang)])
    sin = np.concatenate([np.zeros((T_CTX, MB_N), np.float32), np.sin(ang) * sign])
    cos = np.stack([np.ones_like(cos), cos])
    sin = np.stack([np.zeros_like(sin), sin])
    return jnp.asarray(cos, F32), jnp.asarray(sin, F32)


CONV_PAD = SUBLANES


def _mb_pre_kernel(z_ref, w_ref, b_ref, cos_ref, sin_ref, o_ref, zp_ref):
    half = MB_CONV // 2
    zero = jnp.zeros((CONV_PAD, LANES), F32)
    zp_ref[0:CONV_PAD, :] = zero
    zp_ref[CONV_PAD + T_ALL:, :] = zero
    zp_ref[CONV_PAD:CONV_PAD + T_ALL, :] = z_ref[0]
    lane = lax.broadcasted_iota(jnp.int32, (1, MB_N), 1)
    first = (lane % (2 * ROPE_NF)) < ROPE_NF
    row = lax.broadcasted_iota(jnp.int32, (ROW_BLK, 1), 0)

    def chunk(c, carry):
        base = pl.multiple_of(c * ROW_BLK, ROW_BLK)
        seg_lo = jnp.where(c == 0, 0, T_CTX) - base
        seg_hi = jnp.where(c == 0, T_CTX, T_ALL) - base
        acc = zp_ref[pl.ds(base + CONV_PAD, ROW_BLK), :] * w_ref[half:half + 1, :] + b_ref[...]
        for d in (-2, -1, 1, 2):
            win = zp_ref[pl.ds(base + CONV_PAD + d, ROW_BLK), :]
            ok = (row + d >= seg_lo) & (row + d < seg_hi)
            acc = acc + jnp.where(ok, win, 0.0) * w_ref[half + d:half + d + 1, :]
        y = acc * jax.nn.sigmoid(acc)
        partner = jnp.where(first, pltpu.roll(y, MB_N - ROPE_NF, 1), pltpu.roll(y, ROPE_NF, 1))
        rows = pl.ds(base, ROW_BLK)
        o_ref[0, rows, :] = y * cos_ref[0, rows, :] + partner * sin_ref[0, rows, :]
        return carry

    lax.fori_loop(0, N_ROW_BLK, chunk, 0)


def mb_pre(z, conv_w, conv_b, cos, sin):
    nb = z.shape[0]
    c0 = XBC_OFF // LANES
    rot = lambda b, j: (jnp.where(j >= MB_INNER // LANES, 1, 0), 0, 0)
    return pl.pallas_call(
        _mb_pre_kernel,
        grid=(nb, MB_CONV_CH // LANES),
        in_specs=[pl.BlockSpec((1, T_ALL, LANES), lambda b, j: (b, 0, c0 + j)),
                  pl.BlockSpec((MB_CONV, LANES), lambda b, j: (0, j)),
                  pl.BlockSpec((1, LANES), lambda b, j: (0, j)),
                  pl.BlockSpec((1, T_ALL, MB_N), rot),
                  pl.BlockSpec((1, T_ALL, MB_N), rot)],
        out_specs=pl.BlockSpec((1, T_ALL, LANES), lambda b, j: (b, 0, j)),
        out_shape=jax.ShapeDtypeStruct((nb, T_ALL, MB_CONV_CH), F32),
        scratch_shapes=[pltpu.VMEM((T_ALL + 2 * CONV_PAD, LANES), F32)],
        compiler_params=_cparams(("arbitrary", "arbitrary")),
        name="mb_pre",
    )(z, conv_w, conv_b.reshape(1, MB_CONV_CH), cos, sin)


N_CHUNK = T_ALL // MB_CHUNK
N_CTX_CHUNK = T_CTX // MB_CHUNK
XB_BLK = MB_INNER // LANES
XC_BLK = XB_BLK + MB_G


def _ssd_direction(reverse, gs, x_ref, b_ref, c_ref, dtw_ref, dtr_ref, bw_ref, br_ref, aw_ref, ar_ref,
                   st_ref):
    L = MB_CHUNK
    xl = slice(gs * MB_R * HEAD, (gs + 1) * MB_R * HEAD)
    nl = slice(gs * MB_N, (gs + 1) * MB_N)
    dt_head = jax.nn.softplus(dtw_ref[0, 0, :, xl] + bw_ref[0, :, xl])
    da_head = dt_head * (-jnp.exp(aw_ref[0, :, xl]))
    dtr = jax.nn.softplus(dtr_ref[0, 0, gs] + br_ref[0, gs])
    dar = dtr * (-jnp.exp(ar_ref[0, gs]))
    ri = lax.broadcasted_iota(jnp.int32, (L, L), 0)
    ci = lax.broadcasted_iota(jnp.int32, (L, L), 1)
    mask = (ri <= ci) if reverse else (ri >= ci)
    tri_r = (ri >= ci).astype(F32) if reverse else (ri <= ci).astype(F32)
    tot_row = 0 if reverse else L - 1
    cum_head = da_head
    trow = lax.broadcasted_iota(jnp.int32, (L, 1), 0)
    step = 1
    while step < L:
        if reverse:
            shifted = jnp.where(trow < L - step, pltpu.roll(cum_head, L - step, 0), 0.0)
        else:
            shifted = jnp.where(trow >= step, pltpu.roll(cum_head, step, 0), 0.0)
        cum_head = cum_head + shifted
        step *= 2
    cum_r = _dot(dar, tri_r, precision=HI)
    tot = cum_head[tot_row:tot_row + 1, :]
    bm = b_ref[0, :, nl]
    cm = c_ref[0, :, nl].astype(BF16)
    cb = _dot_nt(cm, bm.astype(BF16))
    bt = bm.T.astype(BF16)
    xc = x_ref[0, :, xl] * dt_head
    xdec = (xc * jnp.exp(tot - cum_head)).astype(BF16)
    ecum = jnp.exp(cum_head)
    etot = jnp.exp(tot)
    first = lax.broadcasted_iota(jnp.int32, (1, LANES), 1) < HEAD
    ys = []
    for p in range(MB_R // 2):
        sl = slice(p * LANES, (p + 1) * LANES)
        xp = xc[:, sl].astype(BF16)
        per_head = []
        for r in (2 * p, 2 * p + 1):
            diff = cum_head[:, r * HEAD:r * HEAD + 1] - cum_r[r:r + 1, :]
            lmat = jnp.exp(jnp.where(mask, diff, -jnp.inf))
            per_head.append(_dot((cb * lmat).astype(BF16), xp))
        y_diag = jnp.where(first, per_head[0], per_head[1])
        sp = gs * (MB_R // 2) + p
        st = st_ref[sp]
        y_off = _dot(cm, st.astype(BF16)) * ecum[:, sl]
        st_ref[sp] = etot[:, sl] * st + _dot(bt, xdec[:, sl])
        ys.append(y_diag + y_off)
    return jnp.concatenate(ys, axis=1)


def _mb_ssd_kernel(*refs):
    fwd, bwd = refs[0:9], refs[9:18]
    yf_ref, yb_ref, stf_ref, stb_ref = refs[18:22]

    @pl.when(pl.program_id(2) == 0)
    def _():
        stf_ref[...] = jnp.zeros_like(stf_ref)
        stb_ref[...] = jnp.zeros_like(stb_ref)

    yf_ref[0] = jnp.concatenate([_ssd_direction(False, gs, *fwd, stf_ref) for gs in range(2)], axis=1)
    yb_ref[0] = jnp.concatenate([_ssd_direction(True, gs, *bwd, stb_ref) for gs in range(2)], axis=1)


def _bwd_chunk(i):
    return jnp.where(i < N_CTX_CHUNK, N_CTX_CHUNK - 1 - i, N_CHUNK + N_CTX_CHUNK - 1 - i)


def mb_ssd(xbc, dt_raw, dt_bias, a_log):
    nb = xbc.shape[0]
    dt4 = dt_raw.reshape(nb, T_ALL, 2, MB_H).transpose(0, 2, 1, 3)
    dtw = jnp.repeat(dt4, HEAD, axis=-1)
    dtr = dt4.reshape(nb, 2, T_ALL, MB_G, MB_R).transpose(0, 1, 3, 4, 2)
    bw = jnp.repeat(dt_bias, HEAD, axis=-1).reshape(2, 1, MB_INNER)
    aw = jnp.repeat(a_log, HEAD, axis=-1).reshape(2, 1, MB_INNER)
    b4 = dt_bias.reshape(2, MB_G, MB_R, 1)
    a4 = a_log.reshape(2, MB_G, MB_R, 1)
    xw = 2 * MB_R * HEAD
    ins, specs = [], []
    for d, cidx in ((0, lambda i: i), (1, _bwd_chunk)):
        ins += [xbc, xbc, xbc, dtw, dtr, bw, b4, aw, a4]
        specs += [
            pl.BlockSpec((1, MB_CHUNK, xw), lambda b, g, i, c=cidx: (b, c(i), g)),
            pl.BlockSpec((1, MB_CHUNK, 2 * MB_N), lambda b, g, i, c=cidx: (b, c(i), XB_BLK // 2 + g)),
            pl.BlockSpec((1, MB_CHUNK, 2 * MB_N), lambda b, g, i, c=cidx: (b, c(i), XC_BLK // 2 + g)),
            pl.BlockSpec((1, 1, MB_CHUNK, xw), lambda b, g, i, c=cidx, d=d: (b, d, c(i), g)),
            pl.BlockSpec((1, 1, 2, MB_R, MB_CHUNK), lambda b, g, i, c=cidx, d=d: (b, d, g, 0, c(i))),
            pl.BlockSpec((1, 1, xw), lambda b, g, i, d=d: (d, 0, g)),
            pl.BlockSpec((1, 2, MB_R, 1), lambda b, g, i, d=d: (d, g, 0, 0)),
            pl.BlockSpec((1, 1, xw), lambda b, g, i, d=d: (d, 0, g)),
            pl.BlockSpec((1, 2, MB_R, 1), lambda b, g, i, d=d: (d, g, 0, 0)),
        ]
    out = jax.ShapeDtypeStruct((nb, T_ALL, MB_INNER), F32)
    return pl.pallas_call(
        _mb_ssd_kernel,
        grid=(nb, MB_G // 2, N_CHUNK),
        in_specs=specs,
        out_specs=[pl.BlockSpec((1, MB_CHUNK, xw), lambda b, g, i: (b, i, g)),
                   pl.BlockSpec((1, MB_CHUNK, xw), lambda b, g, i: (b, _bwd_chunk(i), g))],
        out_shape=[out, out],
        scratch_shapes=[pltpu.VMEM((MB_R, MB_N, 2 * HEAD), F32),
                        pltpu.VMEM((MB_R, MB_N, 2 * HEAD), F32)],
        compiler_params=_cparams(("arbitrary", "arbitrary", "arbitrary")),
        name="mb_ssd",
    )(*ins)


MB_GW = MB_INNER // MB_G


def _mb_post_kernel(yf_ref, yb_ref, x_ref, glo_ref, ghi_ref, d_ref, nw_ref, o_ref):
    y = yf_ref[0] + yb_ref[0] + x_ref[0] * d_ref[...]
    gate = jnp.concatenate([glo_ref[0], ghi_ref[0]], axis=1)
    y = y * (gate * jax.nn.sigmoid(gate))
    ms = jnp.mean(y * y, axis=-1, keepdims=True)
    o_ref[0] = (y * lax.rsqrt(ms + NORM_EPS) * nw_ref[...]).astype(BF16)


def mb_post(yf, yb, xbc, z, d_skip, norm_w):
    nb = yf.shape[0]
    g0 = MB_OFF // LANES
    grp = lambda b, i, g: (b, i, g)
    return pl.pallas_call(
        _mb_post_kernel,
        grid=(nb, N_ROW_BLK, MB_G),
        in_specs=[pl.BlockSpec((1, ROW_BLK, MB_GW), grp),
                  pl.BlockSpec((1, ROW_BLK, MB_GW), grp),
                  pl.BlockSpec((1, ROW_BLK, MB_GW), grp),
                  pl.BlockSpec((1, ROW_BLK, LANES), lambda b, i, g: (b, i, g0 + 2 * g)),
                  pl.BlockSpec((1, ROW_BLK, LANES), lambda b, i, g: (b, i, g0 + 2 * g + 1)),
                  pl.BlockSpec((1, MB_GW), lambda b, i, g: (0, g)),
                  pl.BlockSpec((1, MB_GW), lambda b, i, g: (0, g))],
        out_specs=pl.BlockSpec((1, ROW_BLK, MB_GW), grp),
        out_shape=jax.ShapeDtypeStruct((nb, T_ALL, MB_INNER), BF16),
        compiler_params=_cparams(("arbitrary", "arbitrary", "arbitrary")),
        name="mb_post",
    )(yf, yb, xbc, z, z, jnp.repeat(d_skip, HEAD).reshape(1, MB_INNER), norm_w.reshape(1, MB_INNER))


def mamba_mixer(z, conv_w, conv_b, dt_bias, a_log, d_skip, norm_w, cos, sin):
    xbc = mb_pre(z, conv_w, conv_b, cos, sin)
    yf, yb = mb_ssd(xbc, z[:, :, DT_OFF:DT_OFF + 2 * MB_H], dt_bias, a_log)
    return mb_post(yf, yb, xbc, z, d_skip, norm_w)


def kernel(x, c, ctx, c_ctx, ada_w, ada_b, norm1_w, norm2_w, w_in, w_out, rw_mu, rw_w0, rw_w2, rw_a0, rw_a2, rw_g2, rw_kk, rw_ka, rw_rk, rw_ln_w, rw_ln_b, na_rpb, mb_conv_w, mb_conv_b, mb_dt_bias, mb_a_log, mb_d, mb_norm_w, pe_wq, pe_keys, pe_u, pe_v, final_norm_w):
    nb = x.shape[0]
    xs = jnp.concatenate([ctx, x], axis=1)
    cond = jnp.zeros((SUBLANES, D), F32).at[:nb].set(c).at[nb].set(c_ctx)
    mods = ada_rows(cond, ada_w, ada_b)
    cos, sin = rope_tables()
    na_bias = na_bias_tables(na_rpb)
    f = prev_tab = None
    for l in range(DEPTH):
        m = mods[l].reshape(SUBLANES, 6, D)
        tab = jnp.stack([jnp.broadcast_to(m[nb], (nb, 6, D)), m[:nb]], axis=1).reshape(2 * nb, 6, D)
        if f is None:
            h1 = norm_mod(xs, norm1_w[l].reshape(1, D), tab, 0)
        else:
            xs, h1 = resid_norm_mod(xs, f, prev_tab, norm1_w[l].reshape(1, D), tab)
        z = matmul(h1.reshape(nb * T_ALL, D), cast_bf16(w_in, l, ROW_BLK)).reshape(nb, T_ALL, IN_COLS)
        y_rw = rwkv_mixer(z, rw_mu[l], rw_w0[l], rw_w2[l], rw_a0[l], rw_a2[l], rw_g2[l], rw_kk[l],
                          rw_ka[l], rw_rk[l].reshape(RW_W), rw_ln_w[l], rw_ln_b[l])
        y_na = natten_mixer(z, na_bias, l)
        y_mb = mamba_mixer(z, mb_conv_w[l], mb_conv_b[l], mb_dt_bias[l], mb_a_log[l], mb_d[l],
                           mb_norm_w[l], cos, sin)
        xs = out_proj(xs, y_rw, y_na, y_mb, cast_bf16(w_out, l, ROW_BLK), tab)
        h2 = norm_mod(xs, norm2_w[l].reshape(1, D), tab, 3)
        f = peer_ffn(h2, cast_bf16(pe_wq, l, ROW_BLK), pe_keys[l],
                     cast_bf16(pe_u, l, 1024), cast_bf16(pe_v, l, 1024))
        prev_tab = tab
    return final_norm(xs, f, prev_tab, final_norm_w.reshape(1, D))
```

```python
import functools
import math

import numpy as np
import jax
import jax.numpy as jnp
from jax import lax
from jax.experimental import pallas as pl
from jax.experimental.pallas import tpu as pltpu

D = 2048
NB = 4
T_LAT = 2048
T_CTX = 256
T_ALL = T_CTX + T_LAT
DEPTH = 4
GRID_W = 64
GRID_ROWS = T_LAT // GRID_W
HEAD = 64
RW_W = 512
RW_H = 8
LORA = 64
LORA_G = 128
W_DECAY_SCALE = 0.606531
RW_LN_EPS = 64e-5
NA_W = 512
NA_H = 8
NA_KH = 8
NA_KW = 16
MB_INNER = 1024
MB_H = 16
MB_G = 4
MB_R = 4
MB_N = 128
MB_CONV = 5
MB_CHUNK = 128
ROPE_BASE = 10000.0
PEER_HEADS = 8
N_KEYS = 128
PEER_TOPK = 16
NORM_EPS = 1e-6
NEG_INF = -1e30
RW_COLS = 3 * RW_W + 4 * LORA + LORA_G
NA_COLS = 3 * NA_W
MB_CONV_CH = MB_INNER + 2 * MB_G * MB_N
MB_COLS = MB_INNER + MB_CONV_CH + 2 * MB_H
IN_COLS = RW_COLS + NA_COLS + MB_COLS
NA_OFF = RW_COLS
MB_OFF = RW_COLS + NA_COLS
XBC_OFF = MB_OFF + MB_INNER
DT_OFF = XBC_OFF + MB_CONV_CH

LANES = 128
SUBLANES = 8
VMEM_LIMIT = 56 * 1024 * 1024

ROW_BLK = 256
N_ROW_BLK = T_ALL // ROW_BLK
BF16 = jnp.bfloat16
F32 = jnp.float32
HI = lax.Precision.HIGHEST


def _cparams(sem):
    return pltpu.CompilerParams(dimension_semantics=sem, vmem_limit_bytes=VMEM_LIMIT)


def _dot(a, b, precision=None):
    return jnp.dot(a, b, preferred_element_type=F32, precision=precision)


def _split_bf16(x):
    hi = x.astype(BF16)
    return hi, (x - hi.astype(F32)).astype(BF16)


def _dot_split(a, b):
    ah, al = _split_bf16(a)
    bh, bl = _split_bf16(b)
    return _dot(ah, bh) + _dot(al, bh) + _dot(ah, bl)


def _dot_exact_rhs(a, b_bf16):
    ah, al = _split_bf16(a)
    return _dot(ah, b_bf16) + _dot(al, b_bf16)


def _dot_nt(a, b, precision=None):
    return lax.dot_general(a, b, (((1,), (1,)), ((), ())), preferred_element_type=F32,
                           precision=precision)


def _cast_kernel(x_ref, o_ref):
    o_ref[...] = x_ref[...].astype(BF16)


def cast_bf16(w, layer, rows_blk):
    _, r, c = w.shape
    return pl.pallas_call(
        _cast_kernel,
        grid=(r // rows_blk,),
        in_specs=[pl.BlockSpec((None, rows_blk, c), lambda i: (layer, i, 0))],
        out_specs=pl.BlockSpec((rows_blk, c), lambda i: (i, 0)),
        out_shape=jax.ShapeDtypeStruct((r, c), BF16),
        compiler_params=_cparams(("arbitrary",)),
        name="cast_bf16",
    )(w)


ADA_TN = 1024


def _ada_kernel(c_ref, w_ref, b_ref, o_ref):
    c = c_ref[...]
    s = c * jax.nn.sigmoid(c)
    o_ref[0] = _dot(s, w_ref[0], precision=HI) + b_ref[0]


def ada_rows(cond, ada_w, ada_b):
    nl = ada_w.shape[0]
    return pl.pallas_call(
        _ada_kernel,
        grid=(nl, 6 * D // ADA_TN),
        in_specs=[pl.BlockSpec((SUBLANES, D), lambda l, j: (0, 0)),
                  pl.BlockSpec((1, D, ADA_TN), lambda l, j: (l, 0, j)),
                  pl.BlockSpec((1, 1, ADA_TN), lambda l, j: (l, 0, j))],
        out_specs=pl.BlockSpec((1, SUBLANES, ADA_TN), lambda l, j: (l, 0, j)),
        out_shape=jax.ShapeDtypeStruct((nl, SUBLANES, 6 * D), F32),
        compiler_params=_cparams(("arbitrary", "arbitrary")),
        name="ada_rows",
    )(cond, ada_w, ada_b.reshape(nl, 1, 6 * D))


def _mod_index(b, i):
    return 2 * b + jnp.minimum(i, 1)


def _norm_mod_kernel(which, x_ref, nw_ref, tab_ref, o_ref):
    x = x_ref[0]
    ms = jnp.mean(x * x, axis=-1, keepdims=True)
    y = x * lax.rsqrt(ms + NORM_EPS) * nw_ref[...]
    shift = tab_ref[0, which:which + 1, :]
    scale = tab_ref[0, which + 1:which + 2, :]
    o_ref[0] = (y * (1.0 + scale) + shift).astype(BF16)


def norm_mod(x, nw, tab, which):
    nb = x.shape[0]
    return pl.pallas_call(
        functools.partial(_norm_mod_kernel, which),
        grid=(nb, N_ROW_BLK),
        in_specs=[pl.BlockSpec((1, ROW_BLK, D), lambda b, i: (b, i, 0)),
                  pl.BlockSpec((1, D), lambda b, i: (0, 0)),
                  pl.BlockSpec((1, 6, D), lambda b, i: (_mod_index(b, i), 0, 0))],
        out_specs=pl.BlockSpec((1, ROW_BLK, D), lambda b, i: (b, i, 0)),
        out_shape=jax.ShapeDtypeStruct(x.shape, BF16),
        compiler_params=_cparams(("arbitrary", "arbitrary")),
        name="norm_mod",
    )(x, nw, tab)


def _resid_norm_mod_kernel(x_ref, f_ref, ptab_ref, nw_ref, tab_ref, xo_ref, o_ref):
    x = x_ref[0] + ptab_ref[0, 5:6, :] * f_ref[0]
    xo_ref[0] = x
    ms = jnp.mean(x * x, axis=-1, keepdims=True)
    y = x * lax.rsqrt(ms + NORM_EPS) * nw_ref[...]
    o_ref[0] = (y * (1.0 + tab_ref[0, 1:2, :]) + tab_ref[0, 0:1, :]).astype(BF16)


def resid_norm_mod(x, f, prev_tab, nw, tab):
    nb = x.shape[0]
    row = lambda b, i: (b, i, 0)
    mod = lambda b, i: (_mod_index(b, i), 0, 0)
    return pl.pallas_call(
        _resid_norm_mod_kernel,
        grid=(nb, N_ROW_BLK),
        in_specs=[pl.BlockSpec((1, ROW_BLK, D), row),
                  pl.BlockSpec((1, ROW_BLK, D), row),
                  pl.BlockSpec((1, 6, D), mod),
                  pl.BlockSpec((1, D), lambda b, i: (0, 0)),
                  pl.BlockSpec((1, 6, D), mod)],
        out_specs=[pl.BlockSpec((1, ROW_BLK, D), row), pl.BlockSpec((1, ROW_BLK, D), row)],
        out_shape=[jax.ShapeDtypeStruct(x.shape, F32), jax.ShapeDtypeStruct(x.shape, BF16)],
        input_output_aliases={0: 0},
        compiler_params=_cparams(("arbitrary", "arbitrary")),
        name="resid_norm_mod",
    )(x, f, prev_tab, nw, tab)


def _final_norm_kernel(x_ref, f_ref, ptab_ref, nw_ref, o_ref):
    x = x_ref[0] + ptab_ref[0, 5:6, :] * f_ref[0]
    ms = jnp.mean(x * x, axis=-1, keepdims=True)
    o_ref[0] = x * lax.rsqrt(ms + NORM_EPS) * nw_ref[...]


def final_norm(x, f, prev_tab, nw):
    nb = x.shape[0]
    lat = lambda b, i: (b, i + 1, 0)
    return pl.pallas_call(
        _final_norm_kernel,
        grid=(nb, T_LAT // ROW_BLK),
        in_specs=[pl.BlockSpec((1, ROW_BLK, D), lat),
                  pl.BlockSpec((1, ROW_BLK, D), lat),
                  pl.BlockSpec((1, 6, D), lambda b, i: (2 * b + 1, 0, 0)),
                  pl.BlockSpec((1, D), lambda b, i: (0, 0))],
        out_specs=pl.BlockSpec((1, ROW_BLK, D), lambda b, i: (b, i, 0)),
        out_shape=jax.ShapeDtypeStruct((nb, T_LAT, D), F32),
        compiler_params=_cparams(("arbitrary", "arbitrary")),
        name="final_norm",
    )(x, f, prev_tab, nw)


MM_TM = 1024
MM_TN = 1024


def _mm_kernel(a_ref, w_ref, o_ref):
    o_ref[...] = _dot(a_ref[...], w_ref[...])


def matmul(a, w):
    m, k = a.shape
    n = w.shape[1]
    tm = MM_TM if m % MM_TM == 0 else ROW_BLK
    return pl.pallas_call(
        _mm_kernel,
        grid=(m // tm, pl.cdiv(n, MM_TN)),
        in_specs=[pl.BlockSpec((tm, k), lambda i, j: (i, 0)),
                  pl.BlockSpec((k, MM_TN), lambda i, j: (0, j))],
        out_specs=pl.BlockSpec((tm, MM_TN), lambda i, j: (i, j)),
        out_shape=jax.ShapeDtypeStruct((m, n), F32),
        compiler_params=_cparams(("arbitrary", "arbitrary")),
        name="matmul",
    )(a, w)


def _out_proj_kernel(x_ref, rw_ref, na_ref, mb_ref, w_ref, tab_ref, o_ref):
    acc = _dot(rw_ref[0], w_ref[0:RW_W, :])
    acc += _dot(na_ref[0], w_ref[RW_W:RW_W + NA_W, :])
    acc += _dot(mb_ref[0], w_ref[RW_W + NA_W:, :])
    o_ref[0] = x_ref[0] + tab_ref[0, 2:3, :] * acc


def out_proj(x, y_rw, y_na, y_mb, w, tab):
    nb = x.shape[0]
    row = lambda b, i: (b, i, 0)
    return pl.pallas_call(
        _out_proj_kernel,
        grid=(nb, N_ROW_BLK),
        in_specs=[pl.BlockSpec((1, ROW_BLK, D), row),
                  pl.BlockSpec((1, ROW_BLK, RW_W), row),
                  pl.BlockSpec((1, ROW_BLK, NA_W), row),
                  pl.BlockSpec((1, ROW_BLK, MB_INNER), row),
                  pl.BlockSpec((D, D), lambda b, i: (0, 0)),
                  pl.BlockSpec((1, 6, D), lambda b, i: (_mod_index(b, i), 0, 0))],
        out_specs=pl.BlockSpec((1, ROW_BLK, D), row),
        out_shape=jax.ShapeDtypeStruct(x.shape, F32),
        input_output_aliases={0: 0},
        compiler_params=_cparams(("arbitrary", "arbitrary")),
        name="out_proj",
    )(x, y_rw, y_na, y_mb, w, tab)


TOPK_TT = 128


CAND_COUNTS = (16, 8, 5, 4, 3, 2, 2, 2)
CAND_ROWS = 16 + 8 * 7 + 8
N_HALVES = 2 * PEER_HEADS
N_SLOTS = PEER_HEADS * PEER_TOPK


def _first_max(s, rows, n):
    m = jnp.max(s, axis=0, keepdims=True)
    pos = jnp.min(jnp.where(s == m, rows, float(n)), axis=0, keepdims=True)
    return m, pos


def _peer_topk_kernel(q_ref, keys_ref, i1_ref, i2_ref, g_ref,
                      sc_ref, val_ref, idx_ref, cand_ref, c1_ref, c2_ref, best_ref, e1_ref, e2_ref):
    t = TOPK_TT
    for hp in range(N_HALVES):
        c0 = hp * N_KEYS
        qs = q_ref[:, c0:c0 + N_KEYS].astype(BF16)
        sc_ref[hp] = _dot_nt(keys_ref[hp // 2, hp % 2].astype(BF16), qs)

    rows = lax.broadcasted_iota(jnp.int32, (N_KEYS, t), 0).astype(F32)

    def stage1(r, carry):
        for hp in range(N_HALVES):
            s = sc_ref[hp]
            m, pos = _first_max(s, rows, N_KEYS)
            val_ref[hp, pl.ds(r, 1), :] = m
            idx_ref[hp, pl.ds(r, 1), :] = pos
            sc_ref[hp] = jnp.where(rows == pos, -jnp.inf, s)
        return carry

    lax.fori_loop(0, PEER_TOPK, stage1, 0)

    row8 = lax.broadcasted_iota(jnp.int32, (SUBLANES, t), 0)
    for h in range(PEER_HEADS):
        v1, v2 = val_ref[2 * h], val_ref[2 * h + 1]
        k1, k2 = idx_ref[2 * h], idx_ref[2 * h + 1]
        cand = [v1[0:1] + v2]
        c1 = [jnp.broadcast_to(k1[0:1], (PEER_TOPK, t))]
        c2 = [k2]
        for r1 in range(1, SUBLANES):
            cand.append(jnp.where(row8 < CAND_COUNTS[r1], v1[r1:r1 + 1] + v2[0:SUBLANES], -jnp.inf))
            c1.append(jnp.broadcast_to(k1[r1:r1 + 1], (SUBLANES, t)))
            c2.append(k2[0:SUBLANES])
        cand.append(v1[SUBLANES:] + v2[0:1])
        c1.append(k1[SUBLANES:])
        c2.append(jnp.broadcast_to(k2[0:1], (SUBLANES, t)))
        cand_ref[h] = jnp.concatenate(cand, axis=0)
        c1_ref[h] = jnp.concatenate(c1, axis=0)
        c2_ref[h] = jnp.concatenate(c2, axis=0)

    crow = lax.broadcasted_iota(jnp.int32, (CAND_ROWS, t), 0).astype(F32)

    def stage2(r, carry):
        for h in range(PEER_HEADS):
            cd = cand_ref[h]
            m, pos = _first_max(cd, crow, CAND_ROWS)
            sel = crow == pos
            slot = pl.ds(h * PEER_TOPK + r, 1)
            best_ref[slot, :] = m
            e1_ref[slot, :] = jnp.sum(jnp.where(sel, c1_ref[h], 0.0), axis=0, keepdims=True)
            e2_ref[slot, :] = jnp.sum(jnp.where(sel, c2_ref[h], 0.0), axis=0, keepdims=True)
            cand_ref[h] = jnp.where(sel, -jnp.inf, cd)
        return carry

    lax.fori_loop(0, PEER_TOPK, stage2, 0)

    for h in range(PEER_HEADS):
        sl = slice(h * PEER_TOPK, (h + 1) * PEER_TOPK)
        b = best_ref[sl, :]
        ex = jnp.exp(b - jnp.max(b, axis=0, keepdims=True))
        best_ref[sl, :] = ex / jnp.sum(ex, axis=0, keepdims=True)
    i1_ref[...] = e1_ref[...].T
    i2_ref[...] = e2_ref[...].T
    g_ref[...] = best_ref[...].T


def peer_topk(q, keys):
    n = q.shape[0]
    t = TOPK_TT
    out = jax.ShapeDtypeStruct((n, N_SLOTS), F32)
    spec = pl.BlockSpec((t, N_SLOTS), lambda i: (i, 0))
    return pl.pallas_call(
        _peer_topk_kernel,
        grid=(n // t,),
        in_specs=[pl.BlockSpec((t, D), lambda i: (i, 0)),
                  pl.BlockSpec(keys.shape, lambda i: (0, 0, 0, 0))],
        out_specs=[spec, spec, spec],
        out_shape=[out, out, out],
        scratch_shapes=[pltpu.VMEM((N_HALVES, N_KEYS, t), F32),
                        pltpu.VMEM((N_HALVES, PEER_TOPK, t), F32),
                        pltpu.VMEM((N_HALVES, PEER_TOPK, t), F32),
                        pltpu.VMEM((PEER_HEADS, CAND_ROWS, t), F32),
                        pltpu.VMEM((PEER_HEADS, CAND_ROWS, t), F32),
                        pltpu.VMEM((PEER_HEADS, CAND_ROWS, t), F32),
                        pltpu.VMEM((N_SLOTS, t), F32),
                        pltpu.VMEM((N_SLOTS, t), F32),
                        pltpu.VMEM((N_SLOTS, t), F32)],
        compiler_params=_cparams(("arbitrary",)),
        name="peer_topk",
    )(q, keys)


EXP_TM = 512
EXP_J = 8
EXP_BLK = EXP_J * N_KEYS
N_EXP_BLK = N_KEYS * N_KEYS // EXP_BLK
G_PITCH = N_KEYS // 2 + SUBLANES
SQRT_HALF = 0.7071067811865476


def _peer_expert_kernel(h_ref, i1_ref, i2_ref, g_ref, u_ref, v_ref, o_ref, gs_ref, w_ref):
    jj = pl.program_id(1)

    @pl.when(jj == 0)
    def _():
        o_ref[...] = jnp.zeros_like(o_ref)
        w_ref[...] = jnp.zeros_like(w_ref)
        rows = lax.broadcasted_iota(jnp.int32, (N_KEYS, N_KEYS), 0).astype(F32)

        def tok(t, carry):
            i1r = i1_ref[pl.ds(t, 1), :]
            i2r = i2_ref[pl.ds(t, 1), :]
            gr = g_ref[pl.ds(t, 1), :]
            at = jnp.where(rows == i1r, gr, 0.0).astype(BF16)
            bt = jnp.where(rows == i2r, 1.0, 0.0).astype(BF16)
            gm = _dot_nt(at, bt).astype(BF16)
            start = pl.multiple_of(t * G_PITCH, SUBLANES)
            gs_ref[pl.ds(start, N_KEYS // 2), :] = pltpu.bitcast(gm, jnp.uint32)
            return carry

        lax.fori_loop(0, EXP_TM, tok, 0, unroll=32)

    slot = jj % 2
    o_ref[...] += _dot(w_ref[1 - slot], v_ref[...])

    jb = jnp.minimum(jj, N_EXP_BLK - 1)
    s = _dot_nt(h_ref[...], u_ref[...])
    cols = []
    for c in range(EXP_J // 2):
        packed = gs_ref[pl.ds((EXP_J // 2) * jb + c, EXP_TM, stride=G_PITCH), :]
        cols.append(lax.bitcast_convert_type(packed << 16, F32))
        cols.append(lax.bitcast_convert_type(packed & jnp.uint32(0xFFFF0000), F32))
    g = jnp.concatenate(cols, axis=1)
    act = 0.5 * s * (1.0 + lax.erf(s * SQRT_HALF))
    w_ref[slot] = (g * act).astype(BF16)


def peer_expert(h, i1, i2, g, u, v):
    n = h.shape[0]
    row = lambda i, j: (i, 0)
    sel = pl.BlockSpec((EXP_TM, N_SLOTS), row)
    return pl.pallas_call(
        _peer_expert_kernel,
        grid=(n // EXP_TM, N_EXP_BLK + 1),
        in_specs=[pl.BlockSpec((EXP_TM, D), row), sel, sel, sel,
                  pl.BlockSpec((EXP_BLK, D), lambda i, j: (jnp.minimum(j, N_EXP_BLK - 1), 0)),
                  pl.BlockSpec((EXP_BLK, D), lambda i, j: (jnp.maximum(j - 1, 0), 0))],
        out_specs=pl.BlockSpec((EXP_TM, D), row),
        out_shape=jax.ShapeDtypeStruct((n, D), F32),
        scratch_shapes=[pltpu.VMEM((EXP_TM * G_PITCH, N_KEYS), jnp.uint32),
                        pltpu.VMEM((2, EXP_TM, EXP_BLK), BF16)],
        compiler_params=_cparams(("arbitrary", "arbitrary")),
        name="peer_expert",
    )(h, i1, i2, g, u, v)


def peer_ffn(h, wq, keys, u, v):
    nb = h.shape[0]
    hf = h.reshape(nb * T_ALL, D)
    i1, i2, g = peer_topk(matmul(hf, wq), keys)
    return peer_expert(hf, i1, i2, g, u, v).reshape(nb, T_ALL, D)


def _head_ones(width, group):
    r = np.arange(width) // group
    return jnp.asarray((r[:, None] == r[None, :]).astype(np.float32)).astype(BF16)


def _store_head_pairs(o_ref, x, y):
    for h in range(RW_H):
        ln = slice(h * HEAD, (h + 1) * HEAD)
        o_ref[pl.ds(h, ROW_BLK, stride=RW_H), :] = jnp.concatenate([x[:, ln], y[:, ln]], axis=1)


def _rw_pre_kernel(z_ref, hp_ref, hn_ref, mu_ref, w0_ref, w2_ref, a0_ref, a2_ref, g2_ref,
                   kk_ref, ka_ref, ones_ref,
                   r_ref, k_ref, v_ref, g_ref, rv_ref, awf_ref, awb_ref, bkf_ref, bkb_ref):
    z = z_ref[0]
    row = lax.broadcasted_iota(jnp.int32, (ROW_BLK, 1), 0)
    i = pl.program_id(1)
    halo_prev = jnp.where(i >= 2, hp_ref[0, SUBLANES - 1:SUBLANES, :], 0.0)
    halo_next = jnp.where((i >= 1) & (i < N_ROW_BLK - 1), hn_ref[0, 0:1, :], 0.0)
    prev = jnp.where(row == 0, halo_prev, pltpu.roll(z, 1, 0))
    nxt = jnp.where(row == ROW_BLK - 1, halo_next, pltpu.roll(z, ROW_BLK - 1, 0))
    zs = z + mu_ref[0:1, :] * (prev - z) + mu_ref[1:2, :] * (nxt - z)
    r = zs[:, 0:RW_W]
    k = zs[:, RW_W:2 * RW_W]
    v = zs[:, 2 * RW_W:3 * RW_W]
    o = 3 * RW_W
    lw = jnp.tanh(zs[:, o:o + 2 * LORA])
    la = zs[:, o + 2 * LORA:o + 4 * LORA]
    lg = jax.nn.sigmoid(zs[:, o + 4 * LORA:o + 4 * LORA + LORA_G])
    r_ref[0] = r
    k_ref[0] = k
    v_ref[0] = v
    g_ref[0] = _dot_split(lg, g2_ref[...])
    kkr = k * kk_ref[...]
    ss = _dot_exact_rhs(kkr * kkr, ones_ref[...])
    kkn = kkr / jnp.maximum(jnp.sqrt(ss), 1e-12)
    _store_head_pairs(rv_ref, r, v)
    for d, (aw_ref, bk_ref) in enumerate(((awf_ref, bkf_ref), (awb_ref, bkb_ref))):
        dec = w0_ref[d:d + 1, :] + _dot_split(lw, w2_ref[d])
        a = jax.nn.sigmoid(a0_ref[d:d + 1, :] + _dot_split(la, a2_ref[d]))
        _store_head_pairs(aw_ref, -kkn, jnp.exp(-W_DECAY_SCALE * jax.nn.sigmoid(dec)))
        _store_head_pairs(bk_ref, kkn * a, k * (1.0 + (a - 1.0) * ka_ref[...]))


def rw_pre(z, mu, w0, w2, a0, a2, g2, kk, ka):
    nb = z.shape[0]
    sub_per_blk = ROW_BLK // SUBLANES
    last_sub = T_ALL // SUBLANES - 1
    zpad = jnp.zeros((LORA, RW_W), F32)
    w2p = jnp.stack([jnp.concatenate([w2[0], zpad]), jnp.concatenate([zpad, w2[1]])])
    a2p = jnp.stack([jnp.concatenate([a2[0], zpad]), jnp.concatenate([zpad, a2[1]])])
    row = lambda b, i: (b, i, 0)
    full = lambda shape: pl.BlockSpec(shape, lambda b, i: (0,) * len(shape))
    out = jax.ShapeDtypeStruct((nb, T_ALL, RW_W), F32)
    ospec = pl.BlockSpec((1, ROW_BLK, RW_W), row)
    pout = jax.ShapeDtypeStruct((nb, T_ALL * RW_H, LANES), F32)
    pspec = pl.BlockSpec((None, ROW_BLK * RW_H, LANES), row)
    return pl.pallas_call(
        _rw_pre_kernel,
        grid=(nb, N_ROW_BLK),
        in_specs=[pl.BlockSpec((1, ROW_BLK, RW_COLS), row),
                  pl.BlockSpec((1, SUBLANES, RW_COLS),
                               lambda b, i: (b, jnp.maximum(i * sub_per_blk - 1, 0), 0)),
                  pl.BlockSpec((1, SUBLANES, RW_COLS),
                               lambda b, i: (b, jnp.minimum((i + 1) * sub_per_blk, last_sub), 0)),
                  full((2, RW_COLS)), full((2, RW_W)), full((2, 2 * LORA, RW_W)),
                  full((2, RW_W)), full((2, 2 * LORA, RW_W)), full((LORA_G, RW_W)),
                  full((1, RW_W)), full((1, RW_W)), full((RW_W, RW_W))],
        out_specs=[ospec] * 4 + [pspec] * 5,
        out_shape=[out] * 4 + [pout] * 5,
        compiler_params=_cparams(("arbitrary", "arbitrary")),
        name="rw_pre",
    )(z, z, z, mu, w0, w2p, a0, a2p, g2, kk.reshape(1, RW_W),
      ka.reshape(1, RW_W), _head_ones(RW_W, HEAD))


SCAN_TB = 32
SCAN_I = HEAD // 2


SCAN_NBLK = T_ALL // SCAN_TB
SCAN_CTX_BLK = T_CTX // SCAN_TB


def _rw_scan_kernel(awf_ref, bkf_ref, rvf_ref, awb_ref, bkb_ref, rvb_ref, yf_ref, yb_ref,
                    s_ref, sa_ref, gam_ref, ybuf_ref, *tiles):
    @pl.when(pl.program_id(0) == 0)
    def _():
        s_ref[...] = jnp.zeros_like(s_ref)

    low_half = lax.broadcasted_iota(jnp.int32, (1, LANES), 1) < LANES // 2
    sub = lax.broadcasted_iota(jnp.int32, (SUBLANES, LANES), 0)
    sets = (tiles[:5], tiles[5:])

    def prep(t, tile_set):
        a_s, b_s, k_s, r_s, v_s = tile_set
        rf = pl.ds(pl.multiple_of(t * RW_H, RW_H), RW_H)
        rb = pl.ds(pl.multiple_of((SCAN_TB - 1 - t) * RW_H, RW_H), RW_H)

        def transposed(f_ref, b_ref):
            rows = [f_ref[b, rf, :] for b in range(NB)] + [b_ref[b, rb, :] for b in range(NB)]
            return jnp.concatenate(rows + rows, axis=0).T

        aw = transposed(awf_ref, awb_ref)
        gam_prev = gam_ref[...]
        gam = gam_prev * aw[HEAD:]
        gam_ref[...] = gam
        inv = 1.0 / gam
        a_s[...] = aw[:HEAD] * gam_prev
        bk = transposed(bkf_ref, bkb_ref)
        b_s[...] = bk[:HEAD] * inv
        k_s[...] = bk[HEAD:] * inv
        rv = transposed(rvf_ref, rvb_ref)
        r_s[...] = rv[:HEAD] * gam
        v_s[...] = jnp.where(low_half, rv[HEAD:HEAD + SCAN_I], rv[HEAD + SCAN_I:])

    def fold(x):
        return jnp.sum(x.reshape(SUBLANES, SUBLANES, LANES), axis=0)

    def reduce8(p):
        z = [jnp.where(sub < 4, p[k], p[k + 4]) + pltpu.roll(jnp.where(sub < 4, p[k + 4], p[k]), 4, 0)
             for k in range(4)]
        even2 = (sub & 2) == 0
        v = [jnp.where(even2, z[k] + pltpu.roll(z[k], 6, 0), z[k + 2] + pltpu.roll(z[k + 2], 2, 0))
             for k in range(2)]
        return jnp.where((sub & 1) == 0, v[0] + pltpu.roll(v[0], 7, 0), v[1] + pltpu.roll(v[1], 1, 0))

    def step(t, tile_set):
        a_s, b_s, k_s, r_s, v_s = tile_set
        a = a_s[...]
        for c in range(SCAN_I // SUBLANES):
            rows = range(c * SUBLANES, (c + 1) * SUBLANES)
            sa_ref[c * SUBLANES:(c + 1) * SUBLANES, :] = reduce8([fold(s_ref[i] * a) for i in rows])
        b = b_s[...]
        k = k_s[...]
        r = r_s[...]
        for c in range(SCAN_I // SUBLANES):
            parts = []
            for i in range(c * SUBLANES, (c + 1) * SUBLANES):
                sn = s_ref[i] + sa_ref[pl.ds(i, 1), :] * b + v_s[pl.ds(i, 1), :] * k
                s_ref[i] = sn
                parts.append(fold(sn * r))
            ybuf_ref[pl.ds(pl.multiple_of(t * SCAN_I + c * SUBLANES, SUBLANES), SUBLANES), :] = reduce8(parts)

    gam_ref[...] = jnp.ones_like(gam_ref)
    prep(0, sets[0])

    def two_steps(u, carry):
        t = 2 * u
        prep(t + 1, sets[1])
        step(t, sets[0])
        prep(t + 2, sets[0])
        step(t + 1, sets[1])
        return carry

    lax.fori_loop(0, SCAN_TB // 2 - 1, two_steps, 0)
    prep(SCAN_TB - 1, sets[1])
    step(SCAN_TB - 2, sets[0])
    step(SCAN_TB - 1, sets[1])

    gam_end = gam_ref[...]
    for i in range(SCAN_I):
        s_ref[i] = s_ref[i] * gam_end

    grp = lax.broadcasted_iota(jnp.int32, (1, LANES), 1) // SCAN_I

    def emit(q, carry):
        tile = ybuf_ref[pl.ds(pl.multiple_of(q * LANES, LANES), LANES), :].T
        for t4 in range(LANES // SCAN_I):
            t = q * (LANES // SCAN_I) + t4
            for b in range(NB):
                out = jnp.zeros((RW_H, LANES), F32)
                for x in range(2):
                    for d in range(2):
                        r0 = x * (LANES // 2) + d * (LANES // 4) + b * RW_H
                        dst = d * 2 + x
                        piece = pltpu.roll(tile[r0:r0 + RW_H, :], ((dst - t4) % 4) * SCAN_I, 1)
                        out = jnp.where(grp == dst, piece, out)
                yf_ref[b, pl.ds(pl.multiple_of(t * RW_H, RW_H), RW_H), :] = out
                yb_ref[b, pl.ds(pl.multiple_of((SCAN_TB - 1 - t) * RW_H, RW_H), RW_H), :] = out
        return carry

    lax.fori_loop(0, SCAN_TB * SCAN_I // LANES, emit, 0, unroll=4)


def _mirror_block(g):
    return jnp.where(g < SCAN_CTX_BLK, SCAN_CTX_BLK - 1 - g, SCAN_NBLK + SCAN_CTX_BLK - 1 - g)


def rw_scan(awf, bkf, awb, bkb, rv):
    assert awf.shape[0] == NB
    fspec = pl.BlockSpec((NB, SCAN_TB * RW_H, LANES), lambda g: (0, g, 0))
    bspec = pl.BlockSpec((NB, SCAN_TB * RW_H, LANES), lambda g: (0, _mirror_block(g), 0))
    out = jax.ShapeDtypeStruct((NB, T_ALL * RW_H, LANES), F32)
    tile_set = [pltpu.VMEM((HEAD, LANES), F32)] * 4 + [pltpu.VMEM((SCAN_I, LANES), F32)]
    return pl.pallas_call(
        _rw_scan_kernel,
        grid=(SCAN_NBLK,),
        in_specs=[fspec] * 3 + [bspec] * 3,
        out_specs=[fspec, bspec],
        out_shape=[out, out],
        scratch_shapes=[pltpu.VMEM((SCAN_I, HEAD, LANES), F32),
                        pltpu.VMEM((SCAN_I, LANES), F32),
                        pltpu.VMEM((HEAD, LANES), F32),
                        pltpu.VMEM((SCAN_TB * SCAN_I, LANES), F32)] + tile_set + tile_set,
        compiler_params=_cparams(("arbitrary",)),
        name="rw_scan",
    )(awf, bkf, rv, awb, bkb, rv)


def _rw_post_kernel(yf_ref, yb_ref, r_ref, k_ref, v_ref, g_ref, rk_ref, lw_ref, lb_ref, ones_ref,
                    o_ref):
    low = lax.broadcasted_iota(jnp.int32, (1, LANES), 1) < HEAD
    cols = []
    for hp in range(RW_H // 2):
        pair = []
        for h in (2 * hp, 2 * hp + 1):
            yf = yf_ref[pl.ds(h, ROW_BLK, stride=RW_H), :]
            yb = yb_ref[pl.ds(h, ROW_BLK, stride=RW_H), :]
            if h % 2 == 0:
                pair.append(yf + pltpu.roll(yb, HEAD, 1))
            else:
                pair.append(pltpu.roll(yf, HEAD, 1) + yb)
        cols.append(jnp.where(low, pair[0], pair[1]))
    y = jnp.concatenate(cols, axis=1)
    ones = ones_ref[...]
    mean = _dot_exact_rhs(y, ones) * (1.0 / HEAD)
    yc = y - mean
    var = _dot_exact_rhs(yc * yc, ones) * (1.0 / HEAD)
    yn = yc * lax.rsqrt(var + RW_LN_EPS) * lw_ref[...] + lb_ref[...]
    bonus = _dot_exact_rhs(r_ref[0] * k_ref[0] * rk_ref[...], ones) * v_ref[0]
    o_ref[0] = ((yn + bonus) * g_ref[0]).astype(BF16)


def rw_post(yf, yb, r, k, v, g, rk, ln_w, ln_b):
    nb = yf.shape[0]
    row = lambda b, i: (b, i, 0)
    spec = pl.BlockSpec((1, ROW_BLK, RW_W), row)
    vec = pl.BlockSpec((1, RW_W), lambda b, i: (0, 0))
    yspec = pl.BlockSpec((None, ROW_BLK * RW_H, LANES), row)
    return pl.pallas_call(
        _rw_post_kernel,
        grid=(nb, N_ROW_BLK),
        in_specs=[yspec] * 2 + [spec] * 4 + [vec] * 3 + [pl.BlockSpec((RW_W, RW_W), lambda b, i: (0, 0))],
        out_specs=spec,
        out_shape=jax.ShapeDtypeStruct((nb, T_ALL, RW_W), BF16),
        compiler_params=_cparams(("arbitrary", "arbitrary")),
        name="rw_post",
    )(yf, yb, r, k, v, g, rk.reshape(1, RW_W), ln_w.reshape(1, RW_W), ln_b.reshape(1, RW_W),
      _head_ones(RW_W, HEAD))


def rwkv_mixer(z, mu, w0, w2, a0, a2, g2, kk, ka, rk, ln_w, ln_b):
    r, k, v, g, rv, awf, awb, bkf, bkb = rw_pre(z, mu, w0, w2, a0, a2, g2, kk, ka)
    yf, yb = rw_scan(awf, bkf, awb, bkb, rv)
    return rw_post(yf, yb, r, k, v, g, rk, ln_w, ln_b)


NA_QROWS = ROW_BLK // GRID_W
NA_SLAB = NA_KH + NA_QROWS - 1
NA_SLAB_T = NA_SLAB * GRID_W
NA_SCALE = HEAD ** -0.5


def _na_bias_index():
    a = np.arange(NA_QROWS)[:, None]
    u = np.arange(NA_SLAB)[None, :]
    idx_r, valid = [], []
    for r0, u0 in ((0, 0), (NA_QROWS, 0), (GRID_ROWS - NA_QROWS, GRID_ROWS - NA_SLAB)):
        r = r0 + a
        kr = u0 + u
        kr0 = np.clip(r - NA_KH // 2, 0, GRID_ROWS - NA_KH)
        valid.append((kr >= kr0) & (kr < kr0 + NA_KH))
        idx_r.append(np.clip(kr - r + NA_KH - 1, 0, 2 * NA_KH - 2))
    idx_r = np.stack(idx_r)
    valid = np.stack(valid)[:, :, None, :, None]
    qc = np.arange(GRID_W)[:, None]
    kc = np.arange(GRID_W)[None, :]
    c0 = np.clip(qc - NA_KW // 2, 0, GRID_W - NA_KW)
    in_win = ((kc >= c0) & (kc < c0 + NA_KW))[None, None, :, None, :]
    idx_c = np.clip(kc - qc + NA_KW - 1, 0, 2 * NA_KW - 2)
    col_onehot = (idx_c[None] == np.arange(2 * NA_KW - 1)[:, None, None]).astype(np.float32)
    shape = (3, NA_QROWS, GRID_W, NA_SLAB, GRID_W)
    return idx_r, col_onehot, np.broadcast_to(valid & in_win, shape)


def na_bias_tables(rpb):
    nl = rpb.shape[0]
    idx_r, col_onehot, mask = _na_bias_index()
    rows = rpb[:, :, idx_r.reshape(-1), :].reshape(nl, NA_H, 3, NA_QROWS, NA_SLAB, 2 * NA_KW - 1)
    b = jnp.einsum('lhvauc,cqk->lhvaquk', rows, jnp.asarray(col_onehot), precision=HI)
    b = jnp.where(mask[None, None], b, NEG_INF)
    return b.reshape(nl, NA_H, 3, ROW_BLK, NA_SLAB_T)


def _na_kernel(q_ref, k_ref, v_ref, bias_ref, o_ref):
    qi = pl.program_id(2)

    @pl.when(qi == 0)
    def _():
        ys = []
        for hh in range(2):
            ln = slice(hh * HEAD, (hh + 1) * HEAD)
            q = (q_ref[0, :, ln] * NA_SCALE).astype(BF16)
            s = _dot_nt(q, k_ref[0, 0:T_CTX, ln].astype(BF16))
            p = jnp.exp(s - jnp.max(s, axis=-1, keepdims=True))
            y = _dot(p.astype(BF16), v_ref[0, 0:T_CTX, ln].astype(BF16))
            ys.append(y / jnp.sum(p, axis=-1, keepdims=True))
        o_ref[0] = jnp.concatenate(ys, axis=1).astype(BF16)

    @pl.when(qi > 0)
    def _():
        u0 = jnp.clip(NA_QROWS * (qi - 1) - NA_KH // 2, 0, GRID_ROWS - NA_SLAB)
        start = pl.multiple_of(T_CTX + GRID_W * u0, GRID_W)
        ys = []
        for hh in range(2):
            ln = slice(hh * HEAD, (hh + 1) * HEAD)
            q = (q_ref[0, :, ln] * NA_SCALE).astype(BF16)
            sc = _dot_nt(q, k_ref[0, 0:T_CTX, ln].astype(BF16))
            sw = _dot_nt(q, k_ref[0, pl.ds(start, NA_SLAB_T), ln].astype(BF16)) + bias_ref[hh, 0]
            m = jnp.maximum(jnp.max(sc, axis=-1, keepdims=True), jnp.max(sw, axis=-1, keepdims=True))
            pc = jnp.exp(sc - m)
            pw = jnp.exp(sw - m)
            y = (_dot(pw.astype(BF16), v_ref[0, pl.ds(start, NA_SLAB_T), ln].astype(BF16))
                 + _dot(pc.astype(BF16), v_ref[0, 0:T_CTX, ln].astype(BF16)))
            den = jnp.sum(pc, axis=-1, keepdims=True) + jnp.sum(pw, axis=-1, keepdims=True)
            ys.append(y / den)
        o_ref[0] = jnp.concatenate(ys, axis=1).astype(BF16)


def natten_mixer(z, bias, layer):
    nb = z.shape[0]
    qb, kb, vb = (NA_OFF // LANES, (NA_OFF + NA_W) // LANES, (NA_OFF + 2 * NA_W) // LANES)
    n_blk = N_ROW_BLK - 1

    def bias_idx(b, hp, qi):
        var = jnp.where(qi <= 1, 0, jnp.where(qi == n_blk, 2, 1))
        return (layer, hp, var, 0, 0)

    return pl.pallas_call(
        _na_kernel,
        grid=(nb, NA_H // 2, N_ROW_BLK),
        in_specs=[pl.BlockSpec((1, ROW_BLK, LANES), lambda b, hp, qi: (b, qi, qb + hp)),
                  pl.BlockSpec((1, T_ALL, LANES), lambda b, hp, qi: (b, 0, kb + hp)),
                  pl.BlockSpec((1, T_ALL, LANES), lambda b, hp, qi: (b, 0, vb + hp)),
                  pl.BlockSpec((None, 2, 1, ROW_BLK, NA_SLAB_T), bias_idx)],
        out_specs=pl.BlockSpec((1, ROW_BLK, LANES), lambda b, hp, qi: (b, qi, hp)),
        out_shape=jax.ShapeDtypeStruct((nb, T_ALL, NA_W), BF16),
        compiler_params=_cparams(("arbitrary", "arbitrary", "arbitrary")),
        name="natten",
    )(z, z, z, bias)


ROPE_NF = MB_N // 4


def rope_tables():
    pos = np.arange(T_LAT)
    inv = ROPE_BASE ** (-np.arange(ROPE_NF, dtype=np.float32) / ROPE_NF)
    lane = np.arange(MB_N)
    p = np.where(lane[None, :] < MB_N // 2, (pos // GRID_W)[:, None], (pos % GRID_W)[:, None])
    ang = p.astype(np.float32) * inv[lane % ROPE_NF][None, :]
    sign = np.where((lane % (2 * ROPE_NF)) < ROPE_NF, -1.0, 1.0)[None, :]
    cos = np.concatenate([np.ones((T_CTX, MB_N), np.float32), np.cos(ang)])
    sin = np.concatenate([np.zeros((T_CTX, MB_N), np.float32), np.sin(ang) * sign])
    cos = np.stack([np.ones_like(cos), cos])
    sin = np.stack([np.zeros_like(sin), sin])
    return jnp.asarray(cos, F32), jnp.asarray(sin, F32)


CONV_PAD = SUBLANES


def _mb_pre_kernel(z_ref, w_ref, b_ref, cos_ref, sin_ref, o_ref, zp_ref):
    half = MB_CONV // 2
    zero = jnp.zeros((CONV_PAD, LANES), F32)
    zp_ref[0:CONV_PAD, :] = zero
    zp_ref[CONV_PAD + T_ALL:, :] = zero
    zp_ref[CONV_PAD:CONV_PAD + T_ALL, :] = z_ref[0]
    lane = lax.broadcasted_iota(jnp.int32, (1, MB_N), 1)
    first = (lane % (2 * ROPE_NF)) < ROPE_NF
    row = lax.broadcasted_iota(jnp.int32, (ROW_BLK, 1), 0)

    def chunk(c, carry):
        base = pl.multiple_of(c * ROW_BLK, ROW_BLK)
        seg_lo = jnp.where(c == 0, 0, T_CTX) - base
        seg_hi = jnp.where(c == 0, T_CTX, T_ALL) - base
        acc = zp_ref[pl.ds(base + CONV_PAD, ROW_BLK), :] * w_ref[half:half + 1, :] + b_ref[...]
        for d in (-2, -1, 1, 2):
            win = zp_ref[pl.ds(base + CONV_PAD + d, ROW_BLK), :]
            ok = (row + d >= seg_lo) & (row + d < seg_hi)
            acc = acc + jnp.where(ok, win, 0.0) * w_ref[half + d:half + d + 1, :]
        y = acc * jax.nn.sigmoid(acc)
        partner = jnp.where(first, pltpu.roll(y, MB_N - ROPE_NF, 1), pltpu.roll(y, ROPE_NF, 1))
        rows = pl.ds(base, ROW_BLK)
        o_ref[0, rows, :] = y * cos_ref[0, rows, :] + partner * sin_ref[0, rows, :]
        return carry

    lax.fori_loop(0, N_ROW_BLK, chunk, 0)


def mb_pre(z, conv_w, conv_b, cos, sin):
    nb = z.shape[0]
    c0 = XBC_OFF // LANES
    rot = lambda b, j: (jnp.where(j >= MB_INNER // LANES, 1, 0), 0, 0)
    return pl.pallas_call(
        _mb_pre_kernel,
        grid=(nb, MB_CONV_CH // LANES),
        in_specs=[pl.BlockSpec((1, T_ALL, LANES), lambda b, j: (b, 0, c0 + j)),
                  pl.BlockSpec((MB_CONV, LANES), lambda b, j: (0, j)),
                  pl.BlockSpec((1, LANES), lambda b, j: (0, j)),
                  pl.BlockSpec((1, T_ALL, MB_N), rot),
                  pl.BlockSpec((1, T_ALL, MB_N), rot)],
        out_specs=pl.BlockSpec((1, T_ALL, LANES), lambda b, j: (b, 0, j)),
        out_shape=jax.ShapeDtypeStruct((nb, T_ALL, MB_CONV_CH), F32),
        scratch_shapes=[pltpu.VMEM((T_ALL + 2 * CONV_PAD, LANES), F32)],
        compiler_params=_cparams(("arbitrary", "arbitrary")),
        name="mb_pre",
    )(z, conv_w, conv_b.reshape(1, MB_CONV_CH), cos, sin)


N_CHUNK = T_ALL // MB_CHUNK
N_CTX_CHUNK = T_CTX // MB_CHUNK
XB_BLK = MB_INNER // LANES
XC_BLK = XB_BLK + MB_G


def _ssd_direction(reverse, gs, x_ref, b_ref, c_ref, dtw_ref, dtr_ref, bw_ref, br_ref, aw_ref, ar_ref,
                   st_ref):
    L = MB_CHUNK
    xl = slice(gs * MB_R * HEAD, (gs + 1) * MB_R * HEAD)
    nl = slice(gs * MB_N, (gs + 1) * MB_N)
    dt_head = jax.nn.softplus(dtw_ref[0, 0, :, xl] + bw_ref[0, :, xl])
    da_head = dt_head * (-jnp.exp(aw_ref[0, :, xl]))
    dtr = jax.nn.softplus(dtr_ref[0, 0, gs] + br_ref[0, gs])
    dar = dtr * (-jnp.exp(ar_ref[0, gs]))
    ri = lax.broadcasted_iota(jnp.int32, (L, L), 0)
    ci = lax.broadcasted_iota(jnp.int32, (L, L), 1)
    mask = (ri <= ci) if reverse else (ri >= ci)
    tri_r = (ri >= ci).astype(F32) if reverse else (ri <= ci).astype(F32)
    tot_row = 0 if reverse else L - 1
    cum_head = da_head
    trow = lax.broadcasted_iota(jnp.int32, (L, 1), 0)
    step = 1
    while step < L:
        if reverse:
            shifted = jnp.where(trow < L - step, pltpu.roll(cum_head, L - step, 0), 0.0)
        else:
            shifted = jnp.where(trow >= step, pltpu.roll(cum_head, step, 0), 0.0)
        cum_head = cum_head + shifted
        step *= 2
    cum_r = _dot(dar, tri_r, precision=HI)
    tot = cum_head[tot_row:tot_row + 1, :]
    bm = b_ref[0, :, nl]
    cm = c_ref[0, :, nl].astype(BF16)
    cb = _dot_nt(cm, bm.astype(BF16))
    bt = bm.T.astype(BF16)
    xc = x_ref[0, :, xl] * dt_head
    xdec = (xc * jnp.exp(tot - cum_head)).astype(BF16)
    ecum = jnp.exp(cum_head)
    etot = jnp.exp(tot)
    first = lax.broadcasted_iota(jnp.int32, (1, LANES), 1) < HEAD
    ys = []
    for p in range(MB_R // 2):
        sl = slice(p * LANES, (p + 1) * LANES)
        xp = xc[:, sl].astype(BF16)
        per_head = []
        for r in (2 * p, 2 * p + 1):
            diff = cum_head[:, r * HEAD:r * HEAD + 1] - cum_r[r:r + 1, :]
            lmat = jnp.exp(jnp.where(mask, diff, -jnp.inf))
            per_head.append(_dot((cb * lmat).astype(BF16), xp))
        y_diag = jnp.where(first, per_head[0], per_head[1])
        sp = gs * (MB_R // 2) + p
        st = st_ref[sp]
        y_off = _dot(cm, st.astype(BF16)) * ecum[:, sl]
        st_ref[sp] = etot[:, sl] * st + _dot(bt, xdec[:, sl])
        ys.append(y_diag + y_off)
    return jnp.concatenate(ys, axis=1)


def _mb_ssd_kernel(*refs):
    fwd, bwd = refs[0:9], refs[9:18]
    yf_ref, yb_ref, stf_ref, stb_ref = refs[18:22]

    @pl.when(pl.program_id(2) == 0)
    def _():
        stf_ref[...] = jnp.zeros_like(stf_ref)
        stb_ref[...] = jnp.zeros_like(stb_ref)

    yf_ref[0] = jnp.concatenate([_ssd_direction(False, gs, *fwd, stf_ref) for gs in range(2)], axis=1)
    yb_ref[0] = jnp.concatenate([_ssd_direction(True, gs, *bwd, stb_ref) for gs in range(2)], axis=1)


def _bwd_chunk(i):
    return jnp.where(i < N_CTX_CHUNK, N_CTX_CHUNK - 1 - i, N_CHUNK + N_CTX_CHUNK - 1 - i)


def mb_ssd(xbc, dt_raw, dt_bias, a_log):
    nb = xbc.shape[0]
    dt4 = dt_raw.reshape(nb, T_ALL, 2, MB_H).transpose(0, 2, 1, 3)
    dtw = jnp.repeat(dt4, HEAD, axis=-1)
    dtr = dt4.reshape(nb, 2, T_ALL, MB_G, MB_R).transpose(0, 1, 3, 4, 2)
    bw = jnp.repeat(dt_bias, HEAD, axis=-1).reshape(2, 1, MB_INNER)
    aw = jnp.repeat(a_log, HEAD, axis=-1).reshape(2, 1, MB_INNER)
    b4 = dt_bias.reshape(2, MB_G, MB_R, 1)
    a4 = a_log.reshape(2, MB_G, MB_R, 1)
    xw = 2 * MB_R * HEAD
    ins, specs = [], []
    for d, cidx in ((0, lambda i: i), (1, _bwd_chunk)):
        ins += [xbc, xbc, xbc, dtw, dtr, bw, b4, aw, a4]
        specs += [
            pl.BlockSpec((1, MB_CHUNK, xw), lambda b, g, i, c=cidx: (b, c(i), g)),
            pl.BlockSpec((1, MB_CHUNK, 2 * MB_N), lambda b, g, i, c=cidx: (b, c(i), XB_BLK // 2 + g)),
            pl.BlockSpec((1, MB_CHUNK, 2 * MB_N), lambda b, g, i, c=cidx: (b, c(i), XC_BLK // 2 + g)),
            pl.BlockSpec((1, 1, MB_CHUNK, xw), lambda b, g, i, c=cidx, d=d: (b, d, c(i), g)),
            pl.BlockSpec((1, 1, 2, MB_R, MB_CHUNK), lambda b, g, i, c=cidx, d=d: (b, d, g, 0, c(i))),
            pl.BlockSpec((1, 1, xw), lambda b, g, i, d=d: (d, 0, g)),
            pl.BlockSpec((1, 2, MB_R, 1), lambda b, g, i, d=d: (d, g, 0, 0)),
            pl.BlockSpec((1, 1, xw), lambda b, g, i, d=d: (d, 0, g)),
            pl.BlockSpec((1, 2, MB_R, 1), lambda b, g, i, d=d: (d, g, 0, 0)),
        ]
    out = jax.ShapeDtypeStruct((nb, T_ALL, MB_INNER), F32)
    return pl.pallas_call(
        _mb_ssd_kernel,
        grid=(nb, MB_G // 2, N_CHUNK),
        in_specs=specs,
        out_specs=[pl.BlockSpec((1, MB_CHUNK, xw), lambda b, g, i: (b, i, g)),
                   pl.BlockSpec((1, MB_CHUNK, xw), lambda b, g, i: (b, _bwd_chunk(i), g))],
        out_shape=[out, out],
        scratch_shapes=[pltpu.VMEM((MB_R, MB_N, 2 * HEAD), F32),
                        pltpu.VMEM((MB_R, MB_N, 2 * HEAD), F32)],
        compiler_params=_cparams(("arbitrary", "arbitrary", "arbitrary")),
        name="mb_ssd",
    )(*ins)


MB_GW = MB_INNER // MB_G


def _mb_post_kernel(yf_ref, yb_ref, x_ref, glo_ref, ghi_ref, d_ref, nw_ref, o_ref):
    y = yf_ref[0] + yb_ref[0] + x_ref[0] * d_ref[...]
    gate = jnp.concatenate([glo_ref[0], ghi_ref[0]], axis=1)
    y = y * (gate * jax.nn.sigmoid(gate))
    ms = jnp.mean(y * y, axis=-1, keepdims=True)
    o_ref[0] = (y * lax.rsqrt(ms + NORM_EPS) * nw_ref[...]).astype(BF16)


def mb_post(yf, yb, xbc, z, d_skip, norm_w):
    nb = yf.shape[0]
    g0 = MB_OFF // LANES
    grp = lambda b, i, g: (b, i, g)
    return pl.pallas_call(
        _mb_post_kernel,
        grid=(nb, N_ROW_BLK, MB_G),
        in_specs=[pl.BlockSpec((1, ROW_BLK, MB_GW), grp),
                  pl.BlockSpec((1, ROW_BLK, MB_GW), grp),
                  pl.BlockSpec((1, ROW_BLK, MB_GW), grp),
                  pl.BlockSpec((1, ROW_BLK, LANES), lambda b, i, g: (b, i, g0 + 2 * g)),
                  pl.BlockSpec((1, ROW_BLK, LANES), lambda b, i, g: (b, i, g0 + 2 * g + 1)),
                  pl.BlockSpec((1, MB_GW), lambda b, i, g: (0, g)),
                  pl.BlockSpec((1, MB_GW), lambda b, i, g: (0, g))],
        out_specs=pl.BlockSpec((1, ROW_BLK, MB_GW), grp),
        out_shape=jax.ShapeDtypeStruct((nb, T_ALL, MB_INNER), BF16),
        compiler_params=_cparams(("arbitrary", "arbitrary", "arbitrary")),
        name="mb_post",
    )(yf, yb, xbc, z, z, jnp.repeat(d_skip, HEAD).reshape(1, MB_INNER), norm_w.reshape(1, MB_INNER))


def mamba_mixer(z, conv_w, conv_b, dt_bias, a_log, d_skip, norm_w, cos, sin):
    xbc = mb_pre(z, conv_w, conv_b, cos, sin)
    yf, yb = mb_ssd(xbc, z[:, :, DT_OFF:DT_OFF + 2 * MB_H], dt_bias, a_log)
    return mb_post(yf, yb, xbc, z, d_skip, norm_w)


def kernel(x, c, ctx, c_ctx, ada_w, ada_b, norm1_w, norm2_w, w_in, w_out, rw_mu, rw_w0, rw_w2, rw_a0, rw_a2, rw_g2, rw_kk, rw_ka, rw_rk, rw_ln_w, rw_ln_b, na_rpb, mb_conv_w, mb_conv_b, mb_dt_bias, mb_a_log, mb_d, mb_norm_w, pe_wq, pe_keys, pe_u, pe_v, final_norm_w):
    nb = x.shape[0]
    xs = jnp.concatenate([ctx, x], axis=1)
    cond = jnp.zeros((SUBLANES, D), F32).at[:nb].set(c).at[nb].set(c_ctx)
    mods = ada_rows(cond, ada_w, ada_b)
    cos, sin = rope_tables()
    na_bias = na_bias_tables(na_rpb)
    f = prev_tab = None
    for l in range(DEPTH):
        m = mods[l].reshape(SUBLANES, 6, D)
        tab = jnp.stack([jnp.broadcast_to(m[nb], (nb, 6, D)), m[:nb]], axis=1).reshape(2 * nb, 6, D)
        if f is None:
            h1 = norm_mod(xs, norm1_w[l].reshape(1, D), tab, 0)
        else:
            xs, h1 = resid_norm_mod(xs, f, prev_tab, norm1_w[l].reshape(1, D), tab)
        z = matmul(h1.reshape(nb * T_ALL, D), w_in[l].astype(BF16)).reshape(nb, T_ALL, IN_COLS)
        y_rw = rwkv_mixer(z, rw_mu[l], rw_w0[l], rw_w2[l], rw_a0[l], rw_a2[l], rw_g2[l], rw_kk[l],
                          rw_ka[l], rw_rk[l].reshape(RW_W), rw_ln_w[l], rw_ln_b[l])
        y_na = natten_mixer(z, na_bias, l)
        y_mb = mamba_mixer(z, mb_conv_w[l], mb_conv_b[l], mb_dt_bias[l], mb_a_log[l], mb_d[l],
                           mb_norm_w[l], cos, sin)
        xs = out_proj(xs, y_rw, y_na, y_mb, cast_bf16(w_out, l, ROW_BLK), tab)
        h2 = norm_mod(xs, norm2_w[l].reshape(1, D), tab, 3)
        f = peer_ffn(h2, cast_bf16(pe_wq, l, ROW_BLK), pe_keys[l],
                     cast_bf16(pe_u, l, 1024), cast_bf16(pe_v, l, 1024))
        prev_tab = tab
    return final_norm(xs, f, prev_tab, final_norm_w.reshape(1, D))
```
